```python
import jax, jax.numpy as jnp
from jax import lax
import numpy as np

D_MODEL = 2048
BATCH = 4
SEQ = 4096
DEPTH = 4

N_A_LAYERS = DEPTH // 2
N_B_LAYERS = DEPTH - N_A_LAYERS
A_HEADS = 16
A_DK = D_MODEL // A_HEADS
A_DV = D_MODEL // A_HEADS
A_CHUNK = 64
B_HEADS = 16
B_HD = D_MODEL // B_HEADS
B_CONFIGS = ((128, 1), (512, 4), (2048, 16))
N_GROUPS = len(B_CONFIGS)
B_QBLOCK = 128
ROPE_THETA = 500000.0
ROPE_DIM = B_HD // 4
FFN_HIDDEN = ((8 * D_MODEL // 3 + 255) // 256) * 256
EPS = 1e-6
F32 = jnp.float32

kernel_name = 'yoco_hgrn2_dilated_swa'


def rms_norm(x, g):
    xf = x.astype(F32)
    y = xf * lax.rsqrt(jnp.mean(xf * xf, axis=-1, keepdims=True) + EPS)
    return (y * g.astype(F32)).astype(x.dtype)


def swiglu(h, w_in, w_out):
    gate, up = jnp.split(h @ w_in, 2, axis=-1)
    return (jax.nn.silu(gate) * up) @ w_out


def apply_partial_rope(x, positions):
    half = ROPE_DIM // 2
    inv_freq = ROPE_THETA ** (-jnp.arange(half, dtype=F32) * 2.0 / ROPE_DIM)
    ang = positions.astype(F32)[:, None] * inv_freq[None, :]
    shape = (ang.shape[0],) + (1,) * (x.ndim - 3) + (half,)
    cos, sin = jnp.cos(ang).reshape(shape), jnp.sin(ang).reshape(shape)
    xf = x.astype(F32)
    x1, x2, rest = xf[..., :half], xf[..., half:ROPE_DIM], xf[..., ROPE_DIM:]
    out = jnp.concatenate([x1 * cos - x2 * sin, x2 * cos + x1 * sin, rest], axis=-1)
    return out.astype(x.dtype)


def hgrn2_mixer(h, w_in, o_gain, w_out, lb):
    Bn, S, _ = h.shape
    q, fz, i, g = jnp.split(h @ w_in, 4, axis=-1)
    fz = fz.astype(F32)
    lbf = jnp.clip(lb.astype(F32), 0.0, 1.0 - 1e-6)
    log_f = jnp.logaddexp(jnp.log(lbf), jnp.log1p(-lbf) + jax.nn.log_sigmoid(fz))
    k = (1.0 - lbf) * jax.nn.sigmoid(-fz)
    n_chunks = S // A_CHUNK

    def heads(t, d):
        return t.reshape(Bn, n_chunks, A_CHUNK, A_HEADS, d).transpose(1, 0, 3, 2, 4).astype(F32)

    qc, kc, vc, lc = heads(q, A_DK), heads(k, A_DK), heads(i, A_DV), heads(log_f, A_DK)
    causal = jnp.tril(jnp.ones((A_CHUNK, A_CHUNK), dtype=bool))

    def step(state, inp):
        qb, kb, vb, lfb = inp
        b = jnp.cumsum(lfb, axis=-2)
        o_inter = jnp.einsum('bhtk,bhkv->bhtv', qb * jnp.exp(b), state)
        diff = b[:, :, :, None, :] - b[:, :, None, :, :]
        decay = jnp.exp(jnp.where(causal[:, :, None], diff, -jnp.inf))
        scores = jnp.einsum('bhtk,bhtsk,bhsk->bhts', qb, decay, kb)
        o = o_inter + jnp.einsum('bhts,bhsv->bhtv', scores, vb)
        b_last = b[:, :, -1:, :]
        state = (jnp.exp(b_last[:, :, 0, :])[..., None] * state
                 + jnp.einsum('bhsk,bhsv->bhkv', kb * jnp.exp(b_last - b), vb))
        return state, o

    s0 = jnp.zeros((Bn, A_HEADS, A_DK, A_DV), F32)
    _, o = lax.scan(step, s0, (qc, kc, vc, lc))
    o = o.transpose(1, 0, 3, 2, 4).reshape(Bn, S, A_HEADS, A_DV)
    o = rms_norm(o, o_gain).reshape(Bn, S, A_HEADS * A_DV) * jax.nn.silu(g.astype(F32))
    return o.astype(h.dtype) @ w_out


def dilated_window_attention(q, k, v, window, dilation):
    Bn, S, H, hd = q.shape
    L = S // dilation
    w_sub = window // dilation
    n_blk = -(-L // B_QBLOCK)
    Lp = n_blk * B_QBLOCK

    def strided(t):
        t = t.reshape(Bn, L, dilation, H, hd).transpose(0, 2, 3, 1, 4)
        t = jnp.pad(t, ((0, 0), (0, 0), (0, 0), (0, Lp - L), (0, 0)))
        return t.reshape(Bn, dilation, H, n_blk, B_QBLOCK, hd)

    def band(t):
        prev = jnp.pad(t, ((0, 0), (0, 0), (0, 0), (1, 0), (0, 0), (0, 0)))[:, :, :, :-1]
        return jnp.concatenate([prev, t], axis=4)

    qb = strided(q)
    kk, vv = band(strided(k)), band(strided(v))
    scores = jnp.einsum('bdhnqe,bdhnke->bdhnqk', qb, kk).astype(F32) * (hd ** -0.5)
    qi = jnp.arange(B_QBLOCK)[:, None]
    kj = jnp.arange(2 * B_QBLOCK)[None, :]
    dist = B_QBLOCK + qi - kj
    key_pos = jnp.arange(n_blk)[:, None, None] * B_QBLOCK + kj[None] - B_QBLOCK
    valid = (dist >= 0) & (dist <= w_sub) & (key_pos >= 0)
    scores = jnp.where(valid, scores, -jnp.inf)
    m = jnp.max(scores, axis=-1, keepdims=True)
    p = jnp.exp(scores - m)
    l = jnp.sum(p, axis=-1, keepdims=True)
    o = jnp.einsum('bdhnqk,bdhnke->bdhnqe', p, vv.astype(F32)) / l
    lse = (m + jnp.log(l))[..., 0]
    o = o.reshape(Bn, dilation, H, Lp, hd)[:, :, :, :L].transpose(0, 3, 1, 2, 4).reshape(Bn, S, H, hd)
    lse = lse.reshape(Bn, dilation, H, Lp)[..., :L].transpose(0, 3, 1, 2).reshape(Bn, S, H)
    return o, lse


def shared_kv(x, kv_norm, w_kv, k_gain, positions):
    Bn, S, _ = x.shape
    kv = (rms_norm(x, kv_norm) @ w_kv).reshape(Bn, S, 2, N_GROUPS, B_HEADS, B_HD)
    k = apply_partial_rope(rms_norm(kv[:, :, 0], k_gain[:, None, :]), positions)
    return k, kv[:, :, 1]


def dilated_mixer(h, w_q, q_gain, w_out, k_sh, v_sh, positions):
    Bn, S, _ = h.shape
    q = (h @ w_q).reshape(Bn, S, N_GROUPS, B_HEADS, B_HD)
    q = apply_partial_rope(rms_norm(q, q_gain[:, None, :]), positions)
    outs, lses = [], []
    for gi, (window, dilation) in enumerate(B_CONFIGS):
        o_g, lse_g = dilated_window_attention(q[:, :, gi], k_sh[:, :, gi], v_sh[:, :, gi], window, dilation)
        outs.append(o_g)
        lses.append(lse_g)
    wts = jax.nn.softmax(jnp.stack(lses, axis=0), axis=0)
    o = jnp.sum(wts[..., None] * jnp.stack(outs, axis=0), axis=0)
    return o.reshape(Bn, S, B_HEADS * B_HD).astype(h.dtype) @ w_out


def setup_inputs(seed: int = 0) -> dict:
    key = jax.random.key(seed)
    ks = jax.random.split(key, 16)
    D = D_MODEL

    def w(k, shape, fan_in):
        return jax.random.normal(k, shape, F32) * (fan_in ** -0.5)

    def gain(k, shape):
        return 1.0 + 0.02 * jax.random.normal(k, shape, F32)

    return {
        'x': jax.random.normal(ks[0], (BATCH, SEQ, D), F32),
        'a_norm': gain(ks[1], (N_A_LAYERS, D)),
        'a_w_in': w(ks[2], (N_A_LAYERS, D, 4 * D), D),
        'a_o_gain': gain(ks[3], (N_A_LAYERS, A_DV)),
        'a_w_out': w(ks[4], (N_A_LAYERS, A_HEADS * A_DV, D), A_HEADS * A_DV),
        'a_lower_bounds': 0.1 * jax.random.normal(ks[5], (N_A_LAYERS, A_HEADS * A_DK), F32),
        'kv_norm': gain(ks[6], (D,)),
        'w_kv': w(ks[7], (D, 2 * N_GROUPS * B_HEADS * B_HD), D),
        'k_gain': gain(ks[8], (N_GROUPS, B_HD)),
        'b_norm': gain(ks[9], (N_B_LAYERS, D)),
        'b_w_q': w(ks[10], (N_B_LAYERS, D, N_GROUPS * B_HEADS * B_HD), D),
        'b_q_gain': gain(ks[11], (N_B_LAYERS, N_GROUPS, B_HD)),
        'b_w_out': w(ks[12], (N_B_LAYERS, B_HEADS * B_HD, D), B_HEADS * B_HD),
        'ffn_norm': gain(ks[13], (DEPTH, D)),
        'ffn_w_in': w(ks[14], (DEPTH, D, 2 * FFN_HIDDEN), D),
        'ffn_w_out': w(ks[15], (DEPTH, FFN_HIDDEN, D), FFN_HIDDEN),
    }


def reference(x, a_norm, a_w_in, a_o_gain, a_w_out, a_lower_bounds, kv_norm, w_kv, k_gain,
              b_norm, b_w_q, b_q_gain, b_w_out, ffn_norm, ffn_w_in, ffn_w_out):
    S = x.shape[1]
    positions = jnp.arange(S)
    lbs = jax.nn.softmax(a_lower_bounds.astype(F32), axis=0)
    lbs = jnp.cumsum(lbs, axis=0) - lbs[0]
    k_sh, v_sh = None, None
    for layer in range(DEPTH):
        if layer < N_A_LAYERS:
            x = x + hgrn2_mixer(rms_norm(x, a_norm[layer]), a_w_in[layer], a_o_gain[layer],
                                a_w_out[layer], lbs[layer])
        else:
            if layer == N_A_LAYERS:
                k_sh, v_sh = shared_kv(x, kv_norm, w_kv, k_gain, positions)
            j = layer - N_A_LAYERS
            x = x + dilated_mixer(rms_norm(x, b_norm[j]), b_w_q[j], b_q_gain[j], b_w_out[j],
                                  k_sh, v_sh, positions)
        x = x + swiglu(rms_norm(x, ffn_norm[layer]), ffn_w_in[layer], ffn_w_out[layer])
    return x
```

```python
import functools

import jax
import jax.numpy as jnp
from jax import lax
from jax.experimental import pallas as pl
from jax.experimental.pallas import tpu as pltpu

F32 = jnp.float32
BF16 = jnp.bfloat16

D_MODEL = 2048
DEPTH = 4
N_A_LAYERS = DEPTH // 2
A_HEADS = 16
A_DK = D_MODEL // A_HEADS
A_CHUNK = 64
B_HEADS = 16
B_HD = D_MODEL // B_HEADS
B_CONFIGS = ((128, 1), (512, 4), (2048, 16))
N_GROUPS = len(B_CONFIGS)
B_QBLOCK = 128
ROPE_THETA = 500000.0
ROPE_DIM = B_HD // 4
ROPE_HALF = ROPE_DIM // 2
FFN_HIDDEN = ((8 * D_MODEL // 3 + 255) // 256) * 256
EPS = 1e-6

LANES = 128
VMEM_LIMIT_BYTES = 56 * 1024 * 1024
MASK_VALUE = -1e30

NT_DIMS = (((1,), (1,)), ((), ()))
TN_DIMS = (((0,), (0,)), ((), ()))


def _params(*semantics):
    return pltpu.CompilerParams(dimension_semantics=semantics,
                                vmem_limit_bytes=VMEM_LIMIT_BYTES)


def _rms_rows(x, gain):
    ms = jnp.mean(x * x, axis=-1, keepdims=True)
    return x * lax.rsqrt(ms + EPS) * gain


def _norm_matmul_kernel(x_ref, g_ref, w_ref, o_ref, xn_ref):
    @pl.when(pl.program_id(1) == 0)
    def _():
        xn_ref[...] = _rms_rows(x_ref[...], g_ref[...]).astype(BF16)

    o_ref[...] = jnp.dot(xn_ref[...], w_ref[...],
                         preferred_element_type=F32).astype(o_ref.dtype)


def _norm_matmul_rope_kernel(x_ref, g_ref, w_ref, hg_ref, cos_ref, sa_ref, sb_ref,
                             o_ref, xn_ref, *, n_rope_blocks):
    j = pl.program_id(1)

    @pl.when(j == 0)
    def _():
        xn_ref[...] = _rms_rows(x_ref[...], g_ref[...]).astype(BF16)

    acc = jnp.dot(xn_ref[...], w_ref[...], preferred_element_type=F32)
    heads_per_block = acc.shape[1] // LANES

    @pl.when(j < n_rope_blocks)
    def _():
        cos, sa, sb = cos_ref[...], sa_ref[...], sb_ref[...]
        for h in range(heads_per_block):
            cols = slice(h * LANES, (h + 1) * LANES)
            y = _rms_rows(acc[:, cols], hg_ref[:, cols])
            y = (y * cos + pltpu.roll(y, LANES - ROPE_HALF, 1) * sa
                 + pltpu.roll(y, ROPE_HALF, 1) * sb)
            o_ref[:, cols] = y.astype(o_ref.dtype)

    @pl.when(j >= n_rope_blocks)
    def _():
        o_ref[...] = acc.astype(o_ref.dtype)


def _norm_matmul(x, gain, w, layer, out_dtype, *, tm=1024, tn=1024, rope=None):
    m, d = x.shape
    n = w.shape[-1]
    grid = (m // tm, n // tn)
    x_spec = pl.BlockSpec((tm, d), lambda i, j: (i, 0))
    g_spec = pl.BlockSpec((1, d), lambda i, j: (0, 0))
    w_spec = pl.BlockSpec((None, d, tn), lambda i, j: (layer, 0, j))
    o_spec = pl.BlockSpec((tm, tn), lambda i, j: (i, j))
    scratch = [pltpu.VMEM((tm, d), BF16)]
    out_shape = jax.ShapeDtypeStruct((m, n), out_dtype)
    if rope is None:
        return pl.pallas_call(
            _norm_matmul_kernel, grid=grid, in_specs=[x_spec, g_spec, w_spec],
            out_specs=o_spec, out_shape=out_shape, scratch_shapes=scratch,
            compiler_params=_params("parallel", "arbitrary"), name="norm_matmul",
        )(x, gain, w)
    head_gain, cos_t, sa_t, sb_t, n_rope_cols = rope
    seq_blocks = cos_t.shape[0] // tm
    tab_spec = pl.BlockSpec((tm, LANES), lambda i, j: (i % seq_blocks, 0))
    hg_spec = pl.BlockSpec((1, tn), lambda i, j: (0, j))
    kern = functools.partial(_norm_matmul_rope_kernel, n_rope_blocks=n_rope_cols // tn)
    return pl.pallas_call(
        kern, grid=grid,
        in_specs=[x_spec, g_spec, w_spec, hg_spec, tab_spec, tab_spec, tab_spec],
        out_specs=o_spec, out_shape=out_shape, scratch_shapes=scratch,
        compiler_params=_params("parallel", "arbitrary"), name="norm_matmul_rope",
    )(x, gain, w, head_gain, cos_t, sa_t, sb_t)


def _matmul_residual_kernel(a_ref, w_ref, x_ref, o_ref):
    o_ref[...] = x_ref[...] + jnp.dot(a_ref[...], w_ref[...], preferred_element_type=F32)


def _matmul_residual(a, w, layer, x, *, tm=512):
    m, k = a.shape
    n = w.shape[-1]
    return pl.pallas_call(
        _matmul_residual_kernel, grid=(m // tm,),
        in_specs=[pl.BlockSpec((tm, k), lambda i: (i, 0)),
                  pl.BlockSpec((None, k, n), lambda i: (layer, 0, 0)),
                  pl.BlockSpec((tm, n), lambda i: (i, 0))],
        out_specs=pl.BlockSpec((tm, n), lambda i: (i, 0)),
        out_shape=jax.ShapeDtypeStruct((m, n), F32),
        compiler_params=_params("parallel"), name="matmul_residual",
    )(a, w, x)


def _ffn_kernel(x_ref, g_ref, wg_ref, wu_ref, wo_ref, o_ref, xn_ref, acc_ref):
    j = pl.program_id(1)

    @pl.when(j == 0)
    def _():
        xn_ref[...] = _rms_rows(x_ref[...], g_ref[...]).astype(BF16)
        acc_ref[...] = jnp.zeros_like(acc_ref)

    xn = xn_ref[...]
    gate = jnp.dot(xn, wg_ref[...], preferred_element_type=F32)
    up = jnp.dot(xn, wu_ref[...], preferred_element_type=F32)
    act = (gate / (1.0 + jnp.exp(-gate)) * up).astype(BF16)
    acc_ref[...] += jnp.dot(act, wo_ref[...], preferred_element_type=F32)

    @pl.when(j == pl.num_programs(1) - 1)
    def _():
        o_ref[...] = x_ref[...] + acc_ref[...]


def _ffn(x, gain, w_in, w_out, layer, *, tm=512, th=512):
    m, d = x.shape
    hidden = w_out.shape[1]
    n_h = hidden // th
    return pl.pallas_call(
        _ffn_kernel, grid=(m // tm, n_h),
        in_specs=[pl.BlockSpec((tm, d), lambda i, j: (i, 0)),
                  pl.BlockSpec((1, d), lambda i, j: (0, 0)),
                  pl.BlockSpec((None, d, th), lambda i, j: (layer, 0, j)),
                  pl.BlockSpec((None, d, th), lambda i, j: (layer, 0, n_h + j)),
                  pl.BlockSpec((None, th, d), lambda i, j: (layer, j, 0))],
        out_specs=pl.BlockSpec((tm, d), lambda i, j: (i, 0)),
        out_shape=jax.ShapeDtypeStruct((m, d), F32),
        scratch_shapes=[pltpu.VMEM((tm, d), BF16), pltpu.VMEM((tm, d), F32)],
        compiler_params=_params("parallel", "arbitrary"), name="ffn",
    )(x, gain, w_in, w_in, w_out)


def _split3_bf16(x):
    hi = x.astype(BF16)
    r1 = x - hi.astype(F32)
    mid = r1.astype(BF16)
    lo = (r1 - mid.astype(F32)).astype(BF16)
    return hi, mid, lo


def _hgrn_kernel(q_ref, z_ref, i_ref, g_ref, lb_ref, og_ref, o_ref, st_ref, *,
                 layer, n_chunks):
    C = A_CHUNK
    SUB = 16

    @pl.when(pl.program_id(2) == 0)
    def _():
        st_ref[...] = jnp.zeros_like(st_ref)

    a = lb_ref[...]
    e = jnp.exp(a - jnp.max(a, axis=0, keepdims=True))
    sm = e / jnp.sum(e, axis=0, keepdims=True)
    lb = jnp.sum(sm[:layer + 1], axis=0, keepdims=True) - sm[0:1]
    lb = jnp.clip(lb, 0.0, 1.0 - 1e-6)
    log_lb = jnp.log(lb)
    one_m_lb = 1.0 - lb
    log_one_m_lb = jnp.log(one_m_lb)
    o_gain = og_ref[...]

    row = lax.broadcasted_iota(jnp.int32, (C, LANES), 0)
    tri = (lax.broadcasted_iota(jnp.int32, (C, C), 0)
           >= lax.broadcasted_iota(jnp.int32, (C, C), 1)).astype(BF16)
    ti = lax.broadcasted_iota(jnp.int32, (C, C), 0)
    si = lax.broadcasted_iota(jnp.int32, (C, C), 1)
    same_half = (ti // 32) == (si // 32)
    upper_half = row >= 32
    odd_quarter = (row // SUB) % 2 == 1
    t_sub = lax.broadcasted_iota(jnp.int32, (SUB, 1), 0)

    def chunk(c, carry):
        r0 = pl.multiple_of(c * C, C)
        rows = pl.ds(r0, C)
        q, z, v, g = q_ref[rows, :], z_ref[rows, :], i_ref[rows, :], g_ref[rows, :]

        ez = jnp.exp(-jnp.abs(z))
        log_sig = jnp.minimum(z, 0.0) - jnp.log(1.0 + ez)
        k = one_m_lb * (jnp.where(z >= 0.0, ez, 1.0) / (1.0 + ez))
        y = log_one_m_lb + log_sig
        hi_ = jnp.maximum(log_lb, y)
        lo_ = jnp.minimum(log_lb, y)
        log_f = hi_ + jnp.log(1.0 + jnp.exp(lo_ - hi_))

        b = sum(jnp.dot(tri, part, preferred_element_type=F32)
                for part in _split3_bf16(log_f))
        b_last = b[C - 1:C, :]

        st = st_ref[...]
        q_dec = (q * jnp.exp(b)).astype(BF16)
        o = lax.dot_general(q_dec, st.astype(BF16), NT_DIMS, preferred_element_type=F32)

        d1 = b - b[31:32, :]
        q1 = jnp.where(upper_half, q * jnp.exp(jnp.minimum(d1, 0.0)), 0.0)
        k1 = jnp.where(upper_half, 0.0, k * jnp.exp(jnp.minimum(-d1, 0.0)))
        a1 = lax.dot_general(q1.astype(BF16), k1.astype(BF16), NT_DIMS,
                             preferred_element_type=F32)
        d2 = b - jnp.where(upper_half, b[47:48, :], b[15:16, :])
        q2 = jnp.where(odd_quarter, q * jnp.exp(jnp.minimum(d2, 0.0)), 0.0)
        k2 = jnp.where(odd_quarter, 0.0, k * jnp.exp(jnp.minimum(-d2, 0.0)))
        a2 = lax.dot_general(q2.astype(BF16), k2.astype(BF16), NT_DIMS,
                             preferred_element_type=F32)
        a_off = a1 + jnp.where(same_half, a2, 0.0)
        v_bf = v.astype(BF16)
        o = o + jnp.dot(a_off.astype(BF16), v_bf, preferred_element_type=F32)

        diag = []
        for blk in range(C // SUB):
            sl = slice(blk * SUB, (blk + 1) * SUB)
            qb, kb, vb, bb = q[sl], k[sl], v[sl], b[sl]
            od = jnp.zeros((SUB, LANES), F32)
            for s in range(SUB):
                w = jnp.exp(jnp.minimum(bb - bb[s:s + 1], 0.0))
                a_s = jnp.sum(qb * w * kb[s:s + 1], axis=-1, keepdims=True)
                od = od + jnp.where(t_sub >= s, a_s, 0.0) * vb[s:s + 1]
            diag.append(od)
        o = o + jnp.concatenate(diag, axis=0)

        k_dec = (k * jnp.exp(b_last - b)).astype(BF16)
        st_ref[...] = st * jnp.exp(b_last) + lax.dot_general(
            v_bf, k_dec, TN_DIMS, preferred_element_type=F32)

        out = _rms_rows(o, o_gain) * (g / (1.0 + jnp.exp(-g)))
        o_ref[rows, :] = out.astype(o_ref.dtype)
        return carry

    lax.fori_loop(0, n_chunks, chunk, 0)


def _hgrn(proj, a_lower_bounds, o_gain, layer, batch, seq, *, rows=512):
    n_r = seq // rows
    h_ = A_HEADS

    def sec_spec(sec):
        return pl.BlockSpec((rows, A_DK), lambda b, h, r: (b * n_r + r, sec * h_ + h))

    kern = functools.partial(_hgrn_kernel, layer=layer, n_chunks=rows // A_CHUNK)
    return pl.pallas_call(
        kern, grid=(batch, h_, n_r),
        in_specs=[sec_spec(0), sec_spec(1), sec_spec(2), sec_spec(3),
                  pl.BlockSpec((N_A_LAYERS, A_DK), lambda b, h, r: (0, h)),
                  pl.BlockSpec((1, A_DK), lambda b, h, r: (0, 0))],
        out_specs=pl.BlockSpec((rows, A_DK), lambda b, h, r: (b * n_r + r, h)),
        out_shape=jax.ShapeDtypeStruct((batch * seq, D_MODEL), BF16),
        scratch_shapes=[pltpu.VMEM((A_DK, A_DK), F32)],
        compiler_params=_params("parallel", "parallel", "arbitrary"), name="hgrn",
    )(proj, proj, proj, proj, a_lower_bounds, o_gain)


def _attn_kernel(*refs, tq, has_prev, write_lse):
    q_ref, kc_ref, vc_ref, kp_ref, vp_ref = refs[:5]
    refs = refs[5:]
    if has_prev:
        op_ref, lp_ref = refs[:2]
        refs = refs[2:]
    o_ref = refs[0]
    l_ref = refs[1] if write_lse else None

    QB = B_QBLOCK
    first_key = jnp.where(pl.program_id(2) == 0, QB, 0)
    qi = lax.broadcasted_iota(jnp.int32, (QB, 2 * QB), 0)
    kj = lax.broadcasted_iota(jnp.int32, (QB, 2 * QB), 1)
    band = (kj >= qi) & (kj <= qi + QB)
    band_first = band & (kj >= first_key)
    lane = lax.broadcasted_iota(jnp.int32, (QB, LANES), 1)

    for sb in range(tq // QB):
        rows = slice(sb * QB, (sb + 1) * QB)
        lse_tile = jnp.zeros((QB, LANES), F32)
        lse_prev = lp_ref[0, rows, :] if has_prev else None
        for h in range(B_HEADS):
            cols = slice(h * B_HD, (h + 1) * B_HD)
            q = q_ref[0, rows, cols]
            if sb == 0:
                keys = jnp.concatenate([kp_ref[0, :, cols], kc_ref[0, rows, cols]], axis=0)
                vals = jnp.concatenate([vp_ref[0, :, cols], vc_ref[0, rows, cols]], axis=0)
                valid = band_first
            else:
                both = slice((sb - 1) * QB, (sb + 1) * QB)
                keys, vals = kc_ref[0, both, cols], vc_ref[0, both, cols]
                valid = band
            s = lax.dot_general(q, keys, NT_DIMS, preferred_element_type=F32)
            s = jnp.where(valid, s, MASK_VALUE)
            m = jnp.max(s, axis=-1, keepdims=True)
            p = jnp.exp(s - m)
            l = jnp.sum(p, axis=-1, keepdims=True)
            o = jnp.dot(p.astype(BF16), vals, preferred_element_type=F32) / l
            lse = m + jnp.log(l)
            if has_prev:
                lp = jnp.sum(jnp.where(lane == h, lse_prev, 0.0), axis=-1, keepdims=True)
                hi_ = jnp.maximum(lp, lse)
                new = hi_ + jnp.log(1.0 + jnp.exp(jnp.minimum(lp, lse) - hi_))
                o = (op_ref[0, rows, cols].astype(F32) * jnp.exp(lp - new)
                     + o * jnp.exp(lse - new))
                lse = new
            o_ref[0, rows, cols] = o.astype(o_ref.dtype)
            if write_lse:
                lse_tile = jnp.where(lane == h, lse, lse_tile)
        if write_lse:
            l_ref[0, rows, :] = lse_tile


def _attn_group(q, kv, group, prev, batch, seq, *, write_lse, tq=256):
    _, dil = B_CONFIGS[group]
    L = seq // dil
    tq = min(tq, L)
    n_blk = L // tq
    sub = tq // B_QBLOCK
    G, D = N_GROUPS, D_MODEL
    qv = q.reshape(batch, L, dil * G * D)
    kvv = kv.reshape(batch, L, dil * 2 * G * D)

    def prev_row(n):
        return jnp.maximum(n * sub - 1, 0)

    in_specs = [
        pl.BlockSpec((1, tq, D), lambda b, r, n: (b, n, r * G + group)),
        pl.BlockSpec((1, tq, D), lambda b, r, n: (b, n, r * 2 * G + group)),
        pl.BlockSpec((1, tq, D), lambda b, r, n: (b, n, r * 2 * G + G + group)),
        pl.BlockSpec((1, B_QBLOCK, D), lambda b, r, n: (b, prev_row(n), r * 2 * G + group)),
        pl.BlockSpec((1, B_QBLOCK, D), lambda b, r, n: (b, prev_row(n), r * 2 * G + G + group)),
    ]
    args = [qv, kvv, kvv, kvv, kvv]
    o_spec = pl.BlockSpec((1, tq, D), lambda b, r, n: (b, n, r))
    l_spec = pl.BlockSpec((1, tq, LANES), lambda b, r, n: (b, n, r))
    if prev is not None:
        in_specs += [o_spec, l_spec]
        args += [prev[0].reshape(batch, L, dil * D), prev[1].reshape(batch, L, dil * LANES)]
    out_specs = [o_spec]
    out_shape = [jax.ShapeDtypeStruct((batch, L, dil * D), BF16)]
    if write_lse:
        out_specs.append(l_spec)
        out_shape.append(jax.ShapeDtypeStruct((batch, L, dil * LANES), F32))
    kern = functools.partial(_attn_kernel, tq=tq, has_prev=prev is not None,
                             write_lse=write_lse)
    outs = pl.pallas_call(
        kern, grid=(batch, dil, n_blk), in_specs=in_specs, out_specs=out_specs,
        out_shape=out_shape,
        compiler_params=_params("parallel", "parallel", "arbitrary"),
        name=f"attn_g{group}",
    )(*args)
    o = outs[0].reshape(batch, seq, D)
    lse = outs[1].reshape(batch, seq, LANES) if write_lse else None
    return o, lse


def _rope_tables(seq):
    inv_freq = ROPE_THETA ** (-jnp.arange(ROPE_HALF, dtype=F32) * 2.0 / ROPE_DIM)
    ang = jnp.arange(seq).astype(F32)[:, None] * inv_freq[None, :]
    cos, sin = jnp.cos(ang), jnp.sin(ang)
    rest = LANES - ROPE_DIM
    cos_t = jnp.concatenate([cos, cos, jnp.ones((seq, rest), F32)], axis=1)
    sa_t = jnp.concatenate([-sin, jnp.zeros((seq, LANES - ROPE_HALF), F32)], axis=1)
    sb_t = jnp.concatenate([jnp.zeros((seq, ROPE_HALF), F32), sin,
                            jnp.zeros((seq, rest), F32)], axis=1)
    return cos_t, sa_t, sb_t


def _head_gain_cols(gain, n_cols, scale=1.0):
    cols = jnp.broadcast_to(gain[:, None, :] * scale, (N_GROUPS, B_HEADS, B_HD)).reshape(1, -1)
    pad = n_cols - cols.shape[1]
    return jnp.concatenate([cols, jnp.ones((1, pad), F32)], axis=1) if pad else cols


def kernel(x, a_norm, a_w_in, a_o_gain, a_w_out, a_lower_bounds, kv_norm, w_kv, k_gain,
           b_norm, b_w_q, b_q_gain, b_w_out, ffn_norm, ffn_w_in, ffn_w_out):
    batch, seq, d = x.shape
    m = batch * seq
    xf = x.reshape(m, d)

    a_w_in_b, a_w_out_b = a_w_in.astype(BF16), a_w_out.astype(BF16)
    w_kv_b = w_kv.astype(BF16)[None]
    b_w_q_b, b_w_out_b = b_w_q.astype(BF16), b_w_out.astype(BF16)
    ffn_w_in_b, ffn_w_out_b = ffn_w_in.astype(BF16), ffn_w_out.astype(BF16)

    rope_tabs = _rope_tables(seq)
    gd = N_GROUPS * D_MODEL
    o = None
    kv = None
    for layer in range(DEPTH):
        if layer < N_A_LAYERS:
            proj = _norm_matmul(xf, a_norm[layer][None], a_w_in_b, layer, F32)
            og = _hgrn(proj, a_lower_bounds, a_o_gain[layer][None], layer, batch, seq)
            xf = _matmul_residual(og, a_w_out_b, layer, xf)
        else:
            if layer == N_A_LAYERS:
                kv = _norm_matmul(
                    xf, kv_norm[None], w_kv_b, 0, BF16,
                    rope=(_head_gain_cols(k_gain, 2 * gd),) + rope_tabs + (gd,))
                kv = kv.reshape(batch, seq, 2 * gd)
            jb = layer - N_A_LAYERS
            q = _norm_matmul(
                xf, b_norm[jb][None], b_w_q_b, jb, BF16,
                rope=(_head_gain_cols(b_q_gain[jb], gd, B_HD ** -0.5),) + rope_tabs + (gd,))
            q = q.reshape(batch, seq, gd)
            prev = None
            for group in range(N_GROUPS):
                last = group == N_GROUPS - 1
                o, lse = _attn_group(q, kv, group, prev, batch, seq, write_lse=not last)
                prev = (o, lse)
            xf = _matmul_residual(o.reshape(m, d), b_w_out_b, jb, xf)
        xf = _ffn(xf, ffn_norm[layer][None], ffn_w_in_b, ffn_w_out_b, layer)
    return xf.reshape(batch, seq, d)
```

```python
import functools

import jax
import jax.numpy as jnp
from jax import lax
from jax.experimental import pallas as pl
from jax.experimental.pallas import tpu as pltpu

F32 = jnp.float32
BF16 = jnp.bfloat16

D_MODEL = 2048
DEPTH = 4
N_A_LAYERS = DEPTH // 2
A_HEADS = 16
A_DK = D_MODEL // A_HEADS
A_CHUNK = 64
B_HEADS = 16
B_HD = D_MODEL // B_HEADS
B_CONFIGS = ((128, 1), (512, 4), (2048, 16))
N_GROUPS = len(B_CONFIGS)
B_QBLOCK = 128
ROPE_THETA = 500000.0
ROPE_DIM = B_HD // 4
ROPE_HALF = ROPE_DIM // 2
FFN_HIDDEN = ((8 * D_MODEL // 3 + 255) // 256) * 256
EPS = 1e-6

LANES = 128
VMEM_LIMIT_BYTES = 56 * 1024 * 1024
MASK_VALUE = -1e30

NT_DIMS = (((1,), (1,)), ((), ()))
TN_DIMS = (((0,), (0,)), ((), ()))


def _params(*semantics):
    return pltpu.CompilerParams(dimension_semantics=semantics,
                                vmem_limit_bytes=VMEM_LIMIT_BYTES)


def _rms_rows(x, gain):
    ms = jnp.mean(x * x, axis=-1, keepdims=True)
    return x * lax.rsqrt(ms + EPS) * gain


def _norm_matmul_kernel(x_ref, g_ref, w_ref, o_ref, xn_ref):
    @pl.when(pl.program_id(1) == 0)
    def _():
        xn_ref[...] = _rms_rows(x_ref[...], g_ref[...]).astype(BF16)

    o_ref[...] = jnp.dot(xn_ref[...], w_ref[...],
                         preferred_element_type=F32).astype(o_ref.dtype)


def _norm_matmul_rope_kernel(x_ref, g_ref, w_ref, hg_ref, cos_ref, sa_ref, sb_ref,
                             o_ref, xn_ref, *, n_rope_blocks):
    j = pl.program_id(1)

    @pl.when(j == 0)
    def _():
        xn_ref[...] = _rms_rows(x_ref[...], g_ref[...]).astype(BF16)

    acc = jnp.dot(xn_ref[...], w_ref[...], preferred_element_type=F32)
    heads_per_block = acc.shape[1] // LANES

    @pl.when(j < n_rope_blocks)
    def _():
        cos, sa, sb = cos_ref[...], sa_ref[...], sb_ref[...]
        for h in range(heads_per_block):
            cols = slice(h * LANES, (h + 1) * LANES)
            y = _rms_rows(acc[:, cols], hg_ref[:, cols])
            y = (y * cos + pltpu.roll(y, LANES - ROPE_HALF, 1) * sa
                 + pltpu.roll(y, ROPE_HALF, 1) * sb)
            o_ref[:, cols] = y.astype(o_ref.dtype)

    @pl.when(j >= n_rope_blocks)
    def _():
        o_ref[...] = acc.astype(o_ref.dtype)


def _norm_matmul(x, gain, w, layer, out_dtype, *, tm=1024, tn=1024, rope=None):
    m, d = x.shape
    n = w.shape[-1]
    grid = (m // tm, n // tn)
    x_spec = pl.BlockSpec((tm, d), lambda i, j: (i, 0))
    g_spec = pl.BlockSpec((1, d), lambda i, j: (0, 0))
    w_spec = pl.BlockSpec((None, d, tn), lambda i, j: (layer, 0, j))
    o_spec = pl.BlockSpec((tm, tn), lambda i, j: (i, j))
    scratch = [pltpu.VMEM((tm, d), BF16)]
    out_shape = jax.ShapeDtypeStruct((m, n), out_dtype)
    if rope is None:
        return pl.pallas_call(
            _norm_matmul_kernel, grid=grid, in_specs=[x_spec, g_spec, w_spec],
            out_specs=o_spec, out_shape=out_shape, scratch_shapes=scratch,
            compiler_params=_params("parallel", "arbitrary"), name="norm_matmul",
        )(x, gain, w)
    head_gain, cos_t, sa_t, sb_t, n_rope_cols = rope
    seq_blocks = cos_t.shape[0] // tm
    tab_spec = pl.BlockSpec((tm, LANES), lambda i, j: (i % seq_blocks, 0))
    hg_spec = pl.BlockSpec((1, tn), lambda i, j: (0, j))
    kern = functools.partial(_norm_matmul_rope_kernel, n_rope_blocks=n_rope_cols // tn)
    return pl.pallas_call(
        kern, grid=grid,
        in_specs=[x_spec, g_spec, w_spec, hg_spec, tab_spec, tab_spec, tab_spec],
        out_specs=o_spec, out_shape=out_shape, scratch_shapes=scratch,
        compiler_params=_params("parallel", "arbitrary"), name="norm_matmul_rope",
    )(x, gain, w, head_gain, cos_t, sa_t, sb_t)


def _matmul_residual_kernel(a_ref, w_ref, x_ref, o_ref):
    o_ref[...] = x_ref[...] + jnp.dot(a_ref[...], w_ref[...], preferred_element_type=F32)


def _matmul_residual(a, w, layer, x, *, tm=512):
    m, k = a.shape
    n = w.shape[-1]
    return pl.pallas_call(
        _matmul_residual_kernel, grid=(m // tm,),
        in_specs=[pl.BlockSpec((tm, k), lambda i: (i, 0)),
                  pl.BlockSpec((None, k, n), lambda i: (layer, 0, 0)),
                  pl.BlockSpec((tm, n), lambda i: (i, 0))],
        out_specs=pl.BlockSpec((tm, n), lambda i: (i, 0)),
        out_shape=jax.ShapeDtypeStruct((m, n), F32),
        compiler_params=_params("parallel"), name="matmul_residual",
    )(a, w, x)


def _ffn_kernel(x_ref, g_ref, wg_ref, wu_ref, wo_ref, o_ref, xn_ref, acc_ref):
    j = pl.program_id(1)

    @pl.when(j == 0)
    def _():
        xn_ref[...] = _rms_rows(x_ref[...], g_ref[...]).astype(BF16)
        acc_ref[...] = jnp.zeros_like(acc_ref)

    xn = xn_ref[...]
    gate = jnp.dot(xn, wg_ref[...], preferred_element_type=F32)
    up = jnp.dot(xn, wu_ref[...], preferred_element_type=F32)
    act = (gate / (1.0 + jnp.exp(-gate)) * up).astype(BF16)
    acc_ref[...] += jnp.dot(act, wo_ref[...], preferred_element_type=F32)

    @pl.when(j == pl.num_programs(1) - 1)
    def _():
        o_ref[...] = x_ref[...] + acc_ref[...]


def _ffn(x, gain, w_in, w_out, layer, *, tm=512, th=512):
    m, d = x.shape
    hidden = w_out.shape[1]
    n_h = hidden // th
    return pl.pallas_call(
        _ffn_kernel, grid=(m // tm, n_h),
        in_specs=[pl.BlockSpec((tm, d), lambda i, j: (i, 0)),
                  pl.BlockSpec((1, d), lambda i, j: (0, 0)),
                  pl.BlockSpec((None, d, th), lambda i, j: (layer, 0, j)),
                  pl.BlockSpec((None, d, th), lambda i, j: (layer, 0, n_h + j)),
                  pl.BlockSpec((None, th, d), lambda i, j: (layer, j, 0))],
        out_specs=pl.BlockSpec((tm, d), lambda i, j: (i, 0)),
        out_shape=jax.ShapeDtypeStruct((m, d), F32),
        scratch_shapes=[pltpu.VMEM((tm, d), BF16), pltpu.VMEM((tm, d), F32)],
        compiler_params=_params("parallel", "arbitrary"), name="ffn",
    )(x, gain, w_in, w_in, w_out)


def _split3_bf16(x):
    hi = x.astype(BF16)
    r1 = x - hi.astype(F32)
    mid = r1.astype(BF16)
    lo = (r1 - mid.astype(F32)).astype(BF16)
    return hi, mid, lo


def _hgrn_kernel(q_ref, z_ref, i_ref, g_ref, lb_ref, og_ref, o_ref, st_ref, k_scr, b_scr, *,
                 layer, n_chunks):
    C = A_CHUNK
    SUB = 8
    NB = C // SUB

    @pl.when(pl.program_id(2) == 0)
    def _():
        st_ref[...] = jnp.zeros_like(st_ref)

    a = lb_ref[...]
    e = jnp.exp(a - jnp.max(a, axis=0, keepdims=True))
    sm = e / jnp.sum(e, axis=0, keepdims=True)
    lb = jnp.sum(sm[:layer + 1], axis=0, keepdims=True) - sm[0:1]
    lb = jnp.clip(lb, 0.0, 1.0 - 1e-6)
    log_lb = jnp.log(lb)
    one_m_lb = 1.0 - lb
    log_one_m_lb = jnp.log(one_m_lb)
    o_gain = og_ref[...]

    tri = (lax.broadcasted_iota(jnp.int32, (C, C), 0)
           >= lax.broadcasted_iota(jnp.int32, (C, C), 1)).astype(BF16)
    ti = lax.broadcasted_iota(jnp.int32, (C, C), 0)
    si = lax.broadcasted_iota(jnp.int32, (C, C), 1)
    same_32 = (ti // 32) == (si // 32)
    same_16 = (ti // 16) == (si // 16)
    t_sub = lax.broadcasted_iota(jnp.int32, (SUB, 1), 0)
    zero_blk = jnp.zeros((SUB, LANES), F32)

    def blocks(x):
        return [x[j * SUB:(j + 1) * SUB] for j in range(NB)]

    def padded(parts):
        return jnp.concatenate([zero_blk if p is None else p for p in parts],
                               axis=0).astype(BF16)

    st = st_ref[...]
    for c in range(n_chunks):
        r0 = c * C
        rows = slice(r0, r0 + C)
        q, z, v, g = q_ref[rows, :], z_ref[rows, :], i_ref[rows, :], g_ref[rows, :]

        ez = jnp.exp(-jnp.abs(z))
        log_sig = jnp.minimum(z, 0.0) - jnp.log(1.0 + ez)
        k = one_m_lb * (jnp.where(z >= 0.0, ez, 1.0) / (1.0 + ez))
        y = log_one_m_lb + log_sig
        hi_ = jnp.maximum(log_lb, y)
        lo_ = jnp.minimum(log_lb, y)
        log_f = hi_ + jnp.log(1.0 + jnp.exp(lo_ - hi_))

        b = sum(jnp.dot(tri, part, preferred_element_type=F32)
                for part in _split3_bf16(log_f))
        k_scr[rows, :] = k
        b_scr[rows, :] = b

        def bcast_row(ref, r):
            return jnp.broadcast_to(ref[r0 + r:r0 + r + 1, :], (SUB, LANES))

        qs, ks, bs = blocks(q), blocks(k), blocks(b)
        ends = [bcast_row(b_scr, j * SUB + SUB - 1) for j in range(NB)]

        q_dec = jnp.concatenate([qs[j] * jnp.exp(bs[j]) for j in range(NB)], axis=0)
        o = lax.dot_general(q_dec.astype(BF16), st.astype(BF16), NT_DIMS,
                            preferred_element_type=F32)

        a_levels = []
        for size in (4, 2, 1):
            qp, kp = [None] * NB, [None] * NB
            for pair in range(NB // (2 * size)):
                lo = pair * 2 * size
                r = ends[lo + size - 1]
                for j in range(lo, lo + size):
                    kp[j] = ks[j] * jnp.exp(r - bs[j])
                for j in range(lo + size, lo + 2 * size):
                    qp[j] = qs[j] * jnp.exp(bs[j] - r)
            a_levels.append(lax.dot_general(padded(qp), padded(kp), NT_DIMS,
                                            preferred_element_type=F32))
        a_off = (a_levels[0] + jnp.where(same_32, a_levels[1], 0.0)
                 + jnp.where(same_16, a_levels[2], 0.0))
        v_bf = v.astype(BF16)
        o = o + jnp.dot(a_off.astype(BF16), v_bf, preferred_element_type=F32)

        diag = []
        for j in range(NB):
            od = zero_blk
            for s in range(SUB):
                r = j * SUB + s
                w = jnp.exp(bs[j] - bcast_row(b_scr, r))
                a_s = jnp.sum(qs[j] * w * bcast_row(k_scr, r), axis=-1, keepdims=True)
                od = od + jnp.where(t_sub >= s, a_s, 0.0) * bcast_row(i_ref, r)
            diag.append(od)
        o = o + jnp.concatenate(diag, axis=0)

        b_last = ends[NB - 1]
        k_dec = jnp.concatenate([ks[j] * jnp.exp(b_last - bs[j]) for j in range(NB)], axis=0)
        st = st * jnp.exp(b[C - 1:C, :]) + lax.dot_general(
            v_bf, k_dec.astype(BF16), TN_DIMS, preferred_element_type=F32)

        out = _rms_rows(o, o_gain) * (g / (1.0 + jnp.exp(-g)))
        o_ref[rows, :] = out.astype(o_ref.dtype)

    st_ref[...] = st


def _hgrn(proj, a_lower_bounds, o_gain, layer, batch, seq, *, rows=512):
    n_r = seq // rows
    h_ = A_HEADS

    def sec_spec(sec):
        return pl.BlockSpec((rows, A_DK), lambda b, h, r: (b * n_r + r, sec * h_ + h))

    kern = functools.partial(_hgrn_kernel, layer=layer, n_chunks=rows // A_CHUNK)
    return pl.pallas_call(
        kern, grid=(batch, h_, n_r),
        in_specs=[sec_spec(0), sec_spec(1), sec_spec(2), sec_spec(3),
                  pl.BlockSpec((N_A_LAYERS, A_DK), lambda b, h, r: (0, h)),
                  pl.BlockSpec((1, A_DK), lambda b, h, r: (0, 0))],
        out_specs=pl.BlockSpec((rows, A_DK), lambda b, h, r: (b * n_r + r, h)),
        out_shape=jax.ShapeDtypeStruct((batch * seq, D_MODEL), BF16),
        scratch_shapes=[pltpu.VMEM((A_DK, A_DK), F32), pltpu.VMEM((rows, A_DK), F32),
                        pltpu.VMEM((rows, A_DK), F32)],
        compiler_params=_params("parallel", "parallel", "arbitrary"), name="hgrn",
    )(proj, proj, proj, proj, a_lower_bounds, o_gain)


def _attn_kernel(*refs, tq, has_prev, write_lse):
    q_ref, kc_ref, vc_ref, kp_ref, vp_ref = refs[:5]
    refs = refs[5:]
    if has_prev:
        op_ref, lp_ref = refs[:2]
        refs = refs[2:]
    o_ref = refs[0]
    l_ref = refs[1] if write_lse else None

    QB = B_QBLOCK
    first_key = jnp.where(pl.program_id(2) == 0, QB, 0)
    qi = lax.broadcasted_iota(jnp.int32, (QB, 2 * QB), 0)
    kj = lax.broadcasted_iota(jnp.int32, (QB, 2 * QB), 1)
    band = (kj >= qi) & (kj <= qi + QB)
    band_first = band & (kj >= first_key)
    lane = lax.broadcasted_iota(jnp.int32, (QB, LANES), 1)

    for sb in range(tq // QB):
        rows = slice(sb * QB, (sb + 1) * QB)
        lse_tile = jnp.zeros((QB, LANES), F32)
        lse_prev = lp_ref[0, rows, :] if has_prev else None
        for h in range(B_HEADS):
            cols = slice(h * B_HD, (h + 1) * B_HD)
            q = q_ref[0, rows, cols]
            if sb == 0:
                keys = jnp.concatenate([kp_ref[0, :, cols], kc_ref[0, rows, cols]], axis=0)
                vals = jnp.concatenate([vp_ref[0, :, cols], vc_ref[0, rows, cols]], axis=0)
                valid = band_first
            else:
                both = slice((sb - 1) * QB, (sb + 1) * QB)
                keys, vals = kc_ref[0, both, cols], vc_ref[0, both, cols]
                valid = band
            s = lax.dot_general(q, keys, NT_DIMS, preferred_element_type=F32)
            s = jnp.where(valid, s, MASK_VALUE)
            m = jnp.max(s, axis=-1, keepdims=True)
            p = jnp.exp(s - m)
            l = jnp.sum(p, axis=-1, keepdims=True)
            o = jnp.dot(p.astype(BF16), vals, preferred_element_type=F32) / l
            lse = m + jnp.log(l)
            if has_prev:
                lp = jnp.sum(jnp.where(lane == h, lse_prev, 0.0), axis=-1, keepdims=True)
                hi_ = jnp.maximum(lp, lse)
                new = hi_ + jnp.log(1.0 + jnp.exp(jnp.minimum(lp, lse) - hi_))
                o = (op_ref[0, rows, cols].astype(F32) * jnp.exp(lp - new)
                     + o * jnp.exp(lse - new))
                lse = new
            o_ref[0, rows, cols] = o.astype(o_ref.dtype)
            if write_lse:
                lse_tile = jnp.where(lane == h, lse, lse_tile)
        if write_lse:
            l_ref[0, rows, :] = lse_tile


def _attn_group(q, kv, group, prev, batch, seq, *, write_lse, tq=256):
    _, dil = B_CONFIGS[group]
    L = seq // dil
    tq = min(tq, L)
    n_blk = L // tq
    sub = tq // B_QBLOCK
    G, D = N_GROUPS, D_MODEL
    qv = q.reshape(batch, L, dil * G * D)
    kvv = kv.reshape(batch, L, dil * 2 * G * D)

    def prev_row(n):
        return jnp.maximum(n * sub - 1, 0)

    in_specs = [
        pl.BlockSpec((1, tq, D), lambda b, r, n: (b, n, r * G + group)),
        pl.BlockSpec((1, tq, D), lambda b, r, n: (b, n, r * 2 * G + group)),
        pl.BlockSpec((1, tq, D), lambda b, r, n: (b, n, r * 2 * G + G + group)),
        pl.BlockSpec((1, B_QBLOCK, D), lambda b, r, n: (b, prev_row(n), r * 2 * G + group)),
        pl.BlockSpec((1, B_QBLOCK, D), lambda b, r, n: (b, prev_row(n), r * 2 * G + G + group)),
    ]
    args = [qv, kvv, kvv, kvv, kvv]
    o_spec = pl.BlockSpec((1, tq, D), lambda b, r, n: (b, n, r))
    l_spec = pl.BlockSpec((1, tq, LANES), lambda b, r, n: (b, n, r))
    if prev is not None:
        in_specs += [o_spec, l_spec]
        args += [prev[0].reshape(batch, L, dil * D), prev[1].reshape(batch, L, dil * LANES)]
    out_specs = [o_spec]
    out_shape = [jax.ShapeDtypeStruct((batch, L, dil * D), BF16)]
    if write_lse:
        out_specs.append(l_spec)
        out_shape.append(jax.ShapeDtypeStruct((batch, L, dil * LANES), F32))
    kern = functools.partial(_attn_kernel, tq=tq, has_prev=prev is not None,
                             write_lse=write_lse)
    outs = pl.pallas_call(
        kern, grid=(batch, dil, n_blk), in_specs=in_specs, out_specs=out_specs,
        out_shape=out_shape,
        compiler_params=_params("parallel", "parallel", "arbitrary"),
        name=f"attn_g{group}",
    )(*args)
    o = outs[0].reshape(batch, seq, D)
    lse = outs[1].reshape(batch, seq, LANES) if write_lse else None
    return o, lse


def _rope_tables(seq):
    inv_freq = ROPE_THETA ** (-jnp.arange(ROPE_HALF, dtype=F32) * 2.0 / ROPE_DIM)
    ang = jnp.arange(seq).astype(F32)[:, None] * inv_freq[None, :]
    cos, sin = jnp.cos(ang), jnp.sin(ang)
    rest = LANES - ROPE_DIM
    cos_t = jnp.concatenate([cos, cos, jnp.ones((seq, rest), F32)], axis=1)
    sa_t = jnp.concatenate([-sin, jnp.zeros((seq, LANES - ROPE_HALF), F32)], axis=1)
    sb_t = jnp.concatenate([jnp.zeros((seq, ROPE_HALF), F32), sin,
                            jnp.zeros((seq, rest), F32)], axis=1)
    return cos_t, sa_t, sb_t


def _head_gain_cols(gain, n_cols, scale=1.0):
    cols = jnp.broadcast_to(gain[:, None, :] * scale, (N_GROUPS, B_HEADS, B_HD)).reshape(1, -1)
    pad = n_cols - cols.shape[1]
    return jnp.concatenate([cols, jnp.ones((1, pad), F32)], axis=1) if pad else cols


def kernel(x, a_norm, a_w_in, a_o_gain, a_w_out, a_lower_bounds, kv_norm, w_kv, k_gain,
           b_norm, b_w_q, b_q_gain, b_w_out, ffn_norm, ffn_w_in, ffn_w_out):
    batch, seq, d = x.shape
    m = batch * seq
    xf = x.reshape(m, d)

    a_w_in_b, a_w_out_b = a_w_in.astype(BF16), a_w_out.astype(BF16)
    w_kv_b = w_kv.astype(BF16)[None]
    b_w_q_b, b_w_out_b = b_w_q.astype(BF16), b_w_out.astype(BF16)
    ffn_w_in_b, ffn_w_out_b = ffn_w_in.astype(BF16), ffn_w_out.astype(BF16)

    rope_tabs = _rope_tables(seq)
    gd = N_GROUPS * D_MODEL
    o = None
    kv = None
    for layer in range(DEPTH):
        if layer < N_A_LAYERS:
            proj = _norm_matmul(xf, a_norm[layer][None], a_w_in_b, layer, F32)
            og = _hgrn(proj, a_lower_bounds, a_o_gain[layer][None], layer, batch, seq)
            xf = _matmul_residual(og, a_w_out_b, layer, xf)
        else:
            if layer == N_A_LAYERS:
                kv = _norm_matmul(
                    xf, kv_norm[None], w_kv_b, 0, BF16,
                    rope=(_head_gain_cols(k_gain, 2 * gd),) + rope_tabs + (gd,))
                kv = kv.reshape(batch, seq, 2 * gd)
            jb = layer - N_A_LAYERS
            q = _norm_matmul(
                xf, b_norm[jb][None], b_w_q_b, jb, BF16,
                rope=(_head_gain_cols(b_q_gain[jb], gd, B_HD ** -0.5),) + rope_tabs + (gd,))
            q = q.reshape(batch, seq, gd)
            prev = None
            for group in range(N_GROUPS):
                last = group == N_GROUPS - 1
                o, lse = _attn_group(q, kv, group, prev, batch, seq, write_lse=not last)
                prev = (o, lse)
            xf = _matmul_residual(o.reshape(m, d), b_w_out_b, jb, xf)
        xf = _ffn(xf, ffn_norm[layer][None], ffn_w_in_b, ffn_w_out_b, layer)
    return xf.reshape(batch, seq, d)
```

```python
import functools

import jax
import jax.numpy as jnp
from jax import lax
from jax.experimental import pallas as pl
from jax.experimental.pallas import tpu as pltpu

F32 = jnp.float32
BF16 = jnp.bfloat16

D_MODEL = 2048
DEPTH = 4
N_A_LAYERS = DEPTH // 2
A_HEADS = 16
A_DK = D_MODEL // A_HEADS
A_CHUNK = 64
B_HEADS = 16
B_HD = D_MODEL // B_HEADS
B_CONFIGS = ((128, 1), (512, 4), (2048, 16))
N_GROUPS = len(B_CONFIGS)
B_QBLOCK = 128
ROPE_THETA = 500000.0
ROPE_DIM = B_HD // 4
ROPE_HALF = ROPE_DIM // 2
FFN_HIDDEN = ((8 * D_MODEL // 3 + 255) // 256) * 256
EPS = 1e-6

LANES = 128
MXU_COLS = 256
SEQ_PLANES = 8
VMEM_LIMIT_BYTES = 56 * 1024 * 1024
MASK_VALUE = -1e30

NT_DIMS = (((1,), (1,)), ((), ()))
TN_DIMS = (((0,), (0,)), ((), ()))


def _params(*semantics):
    return pltpu.CompilerParams(dimension_semantics=semantics,
                                vmem_limit_bytes=VMEM_LIMIT_BYTES)


def _rms_rows(x, gain):
    ms = jnp.mean(x * x, axis=-1, keepdims=True)
    return x * lax.rsqrt(ms + EPS) * gain


def _norm_matmul_kernel(x_ref, g_ref, w_ref, o_ref, xn_ref):
    @pl.when(pl.program_id(1) == 0)
    def _():
        xn_ref[...] = _rms_rows(x_ref[...], g_ref[...]).astype(BF16)

    o_ref[...] = jnp.dot(xn_ref[...], w_ref[...],
                         preferred_element_type=F32).astype(o_ref.dtype)


def _norm_matmul_rope_kernel(x_ref, g_ref, w_ref, hg_ref, cos_ref, sin_ref, o_ref, xn_ref,
                             acc_ref, scale_ref, rot_ref):
    @pl.when(pl.program_id(1) == 0)
    def _():
        xn_ref[...] = _rms_rows(x_ref[...], g_ref[...]).astype(BF16)

    xn = xn_ref[...]
    tm = xn.shape[0]
    for c in range(o_ref.shape[1] // MXU_COLS):
        acc_ref[...] = jnp.dot(xn, w_ref[:, c * MXU_COLS:(c + 1) * MXU_COLS],
                               preferred_element_type=F32)
        for h in range(MXU_COLS // LANES):
            hs = slice(h * LANES, (h + 1) * LANES)
            cols = slice(c * MXU_COLS + h * LANES, c * MXU_COLS + (h + 1) * LANES)
            y = acc_ref[:, hs]
            ms = jnp.mean(y * y, axis=-1, keepdims=True)
            scale_ref[:, hs] = jnp.broadcast_to(lax.rsqrt(ms + EPS), (tm, LANES)) * hg_ref[:, cols]
        for h in range(MXU_COLS // LANES):
            hs = slice(h * LANES, (h + 1) * LANES)
            y = acc_ref[:, hs] * scale_ref[:, hs]
            rot_ref[:, hs] = pltpu.roll(y * sin_ref[...], LANES // 2, 1)
        for h in range(MXU_COLS // LANES):
            hs = slice(h * LANES, (h + 1) * LANES)
            cols = slice(c * MXU_COLS + h * LANES, c * MXU_COLS + (h + 1) * LANES)
            y = acc_ref[:, hs] * scale_ref[:, hs]
            o_ref[:, cols] = (y * cos_ref[...] + rot_ref[:, hs]).astype(o_ref.dtype)


def _norm_matmul(x, gain, w, layer, out_dtype, *, tm=1024, tn=1024, rope=None):
    m, d = x.shape
    n = w.shape[-1]
    grid = (m // tm, n // tn)
    x_spec = pl.BlockSpec((tm, d), lambda i, j: (i, 0))
    g_spec = pl.BlockSpec((1, d), lambda i, j: (0, 0))
    w_spec = pl.BlockSpec((None, d, tn), lambda i, j: (layer, 0, j))
    o_spec = pl.BlockSpec((tm, tn), lambda i, j: (i, j))
    scratch = [pltpu.VMEM((tm, d), BF16)]
    out_shape = jax.ShapeDtypeStruct((m, n), out_dtype)
    if rope is None:
        return pl.pallas_call(
            _norm_matmul_kernel, grid=grid, in_specs=[x_spec, g_spec, w_spec],
            out_specs=o_spec, out_shape=out_shape, scratch_shapes=scratch,
            compiler_params=_params("parallel", "arbitrary"), name="norm_matmul",
        )(x, gain, w)
    head_gain, cos_t, sin_t = rope
    seq_blocks = cos_t.shape[0] // tm
    tab_spec = pl.BlockSpec((tm, LANES), lambda i, j: (i % seq_blocks, 0))
    hg_spec = pl.BlockSpec((1, tn), lambda i, j: (0, j))
    return pl.pallas_call(
        _norm_matmul_rope_kernel, grid=grid,
        in_specs=[x_spec, g_spec, w_spec, hg_spec, tab_spec, tab_spec],
        out_specs=o_spec, out_shape=out_shape,
        scratch_shapes=scratch + [pltpu.VMEM((tm, MXU_COLS), F32)] * 3,
        compiler_params=_params("parallel", "arbitrary"), name="norm_matmul_rope",
    )(x, gain, w, head_gain, cos_t, sin_t)


def _matmul_residual_kernel(a_ref, w_ref, x_ref, o_ref):
    o_ref[...] = x_ref[...] + jnp.dot(a_ref[...], w_ref[...], preferred_element_type=F32)


def _matmul_residual(a, w, layer, x, *, tm=512):
    m, k = a.shape
    n = w.shape[-1]
    return pl.pallas_call(
        _matmul_residual_kernel, grid=(m // tm,),
        in_specs=[pl.BlockSpec((tm, k), lambda i: (i, 0)),
                  pl.BlockSpec((None, k, n), lambda i: (layer, 0, 0)),
                  pl.BlockSpec((tm, n), lambda i: (i, 0))],
        out_specs=pl.BlockSpec((tm, n), lambda i: (i, 0)),
        out_shape=jax.ShapeDtypeStruct((m, n), F32),
        compiler_params=_params("parallel"), name="matmul_residual",
    )(a, w, x)


def _ffn_kernel(x_ref, g_ref, wg_ref, wu_ref, wo_ref, o_ref, xn_ref, acc_ref):
    j = pl.program_id(1)

    @pl.when(j == 0)
    def _():
        xn_ref[...] = _rms_rows(x_ref[...], g_ref[...]).astype(BF16)
        acc_ref[...] = jnp.zeros_like(acc_ref)

    xn = xn_ref[...]
    gate = jnp.dot(xn, wg_ref[...], preferred_element_type=F32)
    up = jnp.dot(xn, wu_ref[...], preferred_element_type=F32)
    act = (gate / (1.0 + jnp.exp(-gate)) * up).astype(BF16)
    acc_ref[...] += jnp.dot(act, wo_ref[...], preferred_element_type=F32)

    @pl.when(j == pl.num_programs(1) - 1)
    def _():
        o_ref[...] = x_ref[...] + acc_ref[...]


def _ffn(x, gain, w_in, w_out, layer, *, tm=512, th=512):
    m, d = x.shape
    hidden = w_out.shape[1]
    n_h = hidden // th
    return pl.pallas_call(
        _ffn_kernel, grid=(m // tm, n_h),
        in_specs=[pl.BlockSpec((tm, d), lambda i, j: (i, 0)),
                  pl.BlockSpec((1, d), lambda i, j: (0, 0)),
                  pl.BlockSpec((None, d, th), lambda i, j: (layer, 0, j)),
                  pl.BlockSpec((None, d, th), lambda i, j: (layer, 0, n_h + j)),
                  pl.BlockSpec((None, th, d), lambda i, j: (layer, j, 0))],
        out_specs=pl.BlockSpec((tm, d), lambda i, j: (i, 0)),
        out_shape=jax.ShapeDtypeStruct((m, d), F32),
        scratch_shapes=[pltpu.VMEM((tm, d), BF16), pltpu.VMEM((tm, d), F32)],
        compiler_params=_params("parallel", "arbitrary"), name="ffn",
    )(x, gain, w_in, w_in, w_out)


def _split3_bf16(x):
    hi = x.astype(BF16)
    r1 = x - hi.astype(F32)
    mid = r1.astype(BF16)
    lo = (r1 - mid.astype(F32)).astype(BF16)
    return hi, mid, lo


def _hgrn_kernel(q_ref, z_ref, i_ref, g_ref, lb_ref, og_ref, o_ref, st_ref, k_scr, b_scr, *,
                 layer, n_chunks):
    C = A_CHUNK
    SUB = 8
    NB = C // SUB

    @pl.when(pl.program_id(2) == 0)
    def _():
        st_ref[...] = jnp.zeros_like(st_ref)

    a = lb_ref[...]
    e = jnp.exp(a - jnp.max(a, axis=0, keepdims=True))
    sm = e / jnp.sum(e, axis=0, keepdims=True)
    lb = jnp.sum(sm[:layer + 1], axis=0, keepdims=True) - sm[0:1]
    lb = jnp.clip(lb, 0.0, 1.0 - 1e-6)
    log_lb = jnp.log(lb)
    one_m_lb = 1.0 - lb
    log_one_m_lb = jnp.log(one_m_lb)
    o_gain = og_ref[...]

    tri = (lax.broadcasted_iota(jnp.int32, (C, C), 0)
           >= lax.broadcasted_iota(jnp.int32, (C, C), 1)).astype(BF16)
    ti = lax.broadcasted_iota(jnp.int32, (C, C), 0)
    si = lax.broadcasted_iota(jnp.int32, (C, C), 1)
    same_32 = (ti // 32) == (si // 32)
    same_16 = (ti // 16) == (si // 16)
    t_sub = lax.broadcasted_iota(jnp.int32, (SUB, 1), 0)
    zero_blk = jnp.zeros((SUB, LANES), F32)

    def blocks(x):
        return [x[j * SUB:(j + 1) * SUB] for j in range(NB)]

    def padded(parts):
        return jnp.concatenate([zero_blk if p is None else p for p in parts],
                               axis=0).astype(BF16)

    st = st_ref[...]
    for c in range(n_chunks):
        r0 = c * C
        rows = slice(r0, r0 + C)
        q, z, v, g = q_ref[rows, :], z_ref[rows, :], i_ref[rows, :], g_ref[rows, :]

        ez = jnp.exp(-jnp.abs(z))
        log_sig = jnp.minimum(z, 0.0) - jnp.log(1.0 + ez)
        k = one_m_lb * (jnp.where(z >= 0.0, ez, 1.0) / (1.0 + ez))
        y = log_one_m_lb + log_sig
        hi_ = jnp.maximum(log_lb, y)
        lo_ = jnp.minimum(log_lb, y)
        log_f = hi_ + jnp.log(1.0 + jnp.exp(lo_ - hi_))

        b = sum(jnp.dot(tri, part, preferred_element_type=F32)
                for part in _split3_bf16(log_f))
        k_scr[rows, :] = k
        b_scr[rows, :] = b

        def bcast_row(ref, r):
            return jnp.broadcast_to(ref[r0 + r:r0 + r + 1, :], (SUB, LANES))

        qs, ks, bs = blocks(q), blocks(k), blocks(b)
        ends = [bcast_row(b_scr, j * SUB + SUB - 1) for j in range(NB)]

        q_dec = jnp.concatenate([qs[j] * jnp.exp(bs[j]) for j in range(NB)], axis=0)
        o = lax.dot_general(q_dec.astype(BF16), st.astype(BF16), NT_DIMS,
                            preferred_element_type=F32)

        a_levels = []
        for size in (4, 2, 1):
            qp, kp = [None] * NB, [None] * NB
            for pair in range(NB // (2 * size)):
                lo = pair * 2 * size
                r = ends[lo + size - 1]
                for j in range(lo, lo + size):
                    kp[j] = ks[j] * jnp.exp(r - bs[j])
                for j in range(lo + size, lo + 2 * size):
                    qp[j] = qs[j] * jnp.exp(bs[j] - r)
            a_levels.append(lax.dot_general(padded(qp), padded(kp), NT_DIMS,
                                            preferred_element_type=F32))
        a_off = (a_levels[0] + jnp.where(same_32, a_levels[1], 0.0)
                 + jnp.where(same_16, a_levels[2], 0.0))
        v_bf = v.astype(BF16)
        o = o + jnp.dot(a_off.astype(BF16), v_bf, preferred_element_type=F32)

        diag = []
        for j in range(NB):
            od = zero_blk
            for s in range(SUB):
                r = j * SUB + s
                w = jnp.exp(bs[j] - bcast_row(b_scr, r))
                a_s = jnp.sum(qs[j] * w * bcast_row(k_scr, r), axis=-1, keepdims=True)
                od = od + jnp.where(t_sub >= s, a_s, 0.0) * bcast_row(i_ref, r)
            diag.append(od)
        o = o + jnp.concatenate(diag, axis=0)

        b_last = ends[NB - 1]
        k_dec = jnp.concatenate([ks[j] * jnp.exp(b_last - bs[j]) for j in range(NB)], axis=0)
        st = st * jnp.exp(b[C - 1:C, :]) + lax.dot_general(
            v_bf, k_dec.astype(BF16), TN_DIMS, preferred_element_type=F32)

        out = _rms_rows(o, o_gain) * (g / (1.0 + jnp.exp(-g)))
        o_ref[rows, :] = out.astype(o_ref.dtype)

    st_ref[...] = st


def _hgrn(proj, a_lower_bounds, o_gain, layer, batch, seq, *, rows=512):
    n_r = seq // rows
    h_ = A_HEADS

    def sec_spec(sec):
        return pl.BlockSpec((rows, A_DK), lambda b, h, r: (b * n_r + r, sec * h_ + h))

    kern = functools.partial(_hgrn_kernel, layer=layer, n_chunks=rows // A_CHUNK)
    return pl.pallas_call(
        kern, grid=(batch, h_, n_r),
        in_specs=[sec_spec(0), sec_spec(1), sec_spec(2), sec_spec(3),
                  pl.BlockSpec((N_A_LAYERS, A_DK), lambda b, h, r: (0, h)),
                  pl.BlockSpec((1, A_DK), lambda b, h, r: (0, 0))],
        out_specs=pl.BlockSpec((rows, A_DK), lambda b, h, r: (b * n_r + r, h)),
        out_shape=jax.ShapeDtypeStruct((batch * seq, D_MODEL), BF16),
        scratch_shapes=[pltpu.VMEM((A_DK, A_DK), F32), pltpu.VMEM((rows, A_DK), F32),
                        pltpu.VMEM((rows, A_DK), F32)],
        compiler_params=_params("parallel", "parallel", "arbitrary"), name="hgrn",
    )(proj, proj, proj, proj, a_lower_bounds, o_gain)


def _attn_kernel(*refs, planes, tile_rows, key_rows, den, block_rows, has_prev, write_lse):
    q_ref, kc_ref, vc_ref, kp_ref, vp_ref = refs[:5]
    refs = refs[5:]
    if has_prev:
        op_ref, lp_ref = refs[:2]
        refs = refs[2:]
    o_ref = refs[0]
    l_ref = refs[1] if write_lse else None

    QB, P, R, KW = B_QBLOCK, planes, tile_rows, key_rows
    n_keys = P * (KW + R)
    qa = lax.broadcasted_iota(jnp.int32, (QB, n_keys), 0)
    kc = lax.broadcasted_iota(jnp.int32, (QB, n_keys), 1)
    key_row = kc % (KW + R) - KW
    num = P * (qa % R - key_row) + (qa // R - kc // (KW + R))
    band = (num >= 0) & (num <= QB * den) & (num % den == 0)
    lane = lax.broadcasted_iota(jnp.int32, (QB, LANES), 1)
    block_row0 = pl.program_id(2) * block_rows

    def gather(ref, rows, cols):
        return jnp.concatenate([ref[0, m, rows, cols] for m in range(P)], axis=0)

    for t in range(block_rows // R):
        rows = slice(t * R, (t + 1) * R)
        first = t * R - KW
        if first < 0:
            valid = band & (key_row >= -(block_row0 + t * R))
        else:
            valid = band

        def window(cur_ref, prev_ref, cols):
            if first >= 0:
                return gather(cur_ref, slice(first, (t + 1) * R), cols)
            parts = []
            for m in range(P):
                parts += [prev_ref[0, m, KW + first:KW, cols], cur_ref[0, m, 0:(t + 1) * R, cols]]
            return jnp.concatenate(parts, axis=0)

        lse_tile = jnp.zeros((QB, LANES), F32)
        lse_prev = gather(lp_ref, rows, slice(None)) if has_prev else None
        for h in range(B_HEADS):
            cols = slice(h * B_HD, (h + 1) * B_HD)
            q = gather(q_ref, rows, cols)
            keys = window(kc_ref, kp_ref, cols)
            vals = window(vc_ref, vp_ref, cols)
            s = lax.dot_general(q, keys, NT_DIMS, preferred_element_type=F32)
            s = jnp.where(valid, s, MASK_VALUE)
            m = jnp.max(s, axis=-1, keepdims=True)
            p = jnp.exp(s - m)
            l = jnp.sum(p, axis=-1, keepdims=True)
            o = jnp.dot(p.astype(BF16), vals, preferred_element_type=F32) / l
            lse = m + jnp.log(l)
            if has_prev:
                lp = jnp.sum(jnp.where(lane == h, lse_prev, 0.0), axis=-1, keepdims=True)
                hi_ = jnp.maximum(lp, lse)
                new = hi_ + jnp.log(1.0 + jnp.exp(jnp.minimum(lp, lse) - hi_))
                o = (gather(op_ref, rows, cols).astype(F32) * jnp.exp(lp - new)
                     + o * jnp.exp(lse - new))
                lse = new
            o = o.astype(o_ref.dtype)
            for m in range(P):
                o_ref[0, m, rows, cols] = o[m * R:(m + 1) * R]
            if write_lse:
                lse_tile = jnp.where(lane == h, lse, lse_tile)
        if write_lse:
            for m in range(P):
                l_ref[0, m, rows, :] = lse_tile[m * R:(m + 1) * R]


def _attn_group(q, k, v, group, prev, *, write_lse):
    batch, n_planes, plane_rows, _ = q.shape
    window, dil = B_CONFIGS[group]
    assert window // dil == B_QBLOCK
    planes = max(n_planes // dil, 1)
    den = max(dil // n_planes, 1)
    n_sets = n_planes // planes
    tile_rows = B_QBLOCK // planes
    key_rows = tile_rows * den
    block_rows = plane_rows // planes
    D = D_MODEL

    def view(a):
        return a.reshape(batch, planes, n_sets, plane_rows, a.shape[-1])

    def prev_block(n):
        return jnp.maximum(n * (block_rows // key_rows) - 1, 0)

    cur = pl.BlockSpec((1, planes, None, block_rows, D), lambda b, s, n: (b, 0, s, n, group))
    before = pl.BlockSpec((1, planes, None, key_rows, D),
                          lambda b, s, n: (b, 0, s, prev_block(n), group))
    o_spec = pl.BlockSpec((1, planes, None, block_rows, D), lambda b, s, n: (b, 0, s, n, 0))
    l_spec = pl.BlockSpec((1, planes, None, block_rows, LANES), lambda b, s, n: (b, 0, s, n, 0))
    in_specs = [cur, cur, cur, before, before]
    args = [view(q), view(k), view(v), view(k), view(v)]
    if prev is not None:
        in_specs += [o_spec, l_spec]
        args += [view(prev[0]), view(prev[1])]
    out_specs = [o_spec]
    out_shape = [jax.ShapeDtypeStruct((batch, planes, n_sets, plane_rows, D), BF16)]
    if write_lse:
        out_specs.append(l_spec)
        out_shape.append(jax.ShapeDtypeStruct((batch, planes, n_sets, plane_rows, LANES), F32))
    kern = functools.partial(
        _attn_kernel, planes=planes, tile_rows=tile_rows, key_rows=key_rows, den=den,
        block_rows=block_rows, has_prev=prev is not None, write_lse=write_lse)
    outs = pl.pallas_call(
        kern, grid=(batch, n_sets, plane_rows // block_rows), in_specs=in_specs,
        out_specs=out_specs, out_shape=out_shape,
        compiler_params=_params("parallel", "parallel", "arbitrary"),
        name=f"attn_g{group}",
    )(*args)
    o = outs[0].reshape(batch, n_planes, plane_rows, D)
    lse = outs[1].reshape(batch, n_planes, plane_rows, LANES) if write_lse else None
    return o, lse


def _to_planes(a, batch):
    rows, c = a.shape
    return (a.reshape(batch, rows // batch // SEQ_PLANES, SEQ_PLANES, c)
            .transpose(0, 2, 1, 3).reshape(rows, c))


def _from_planes(a, batch):
    rows, c = a.shape
    return (a.reshape(batch, SEQ_PLANES, rows // batch // SEQ_PLANES, c)
            .transpose(0, 2, 1, 3).reshape(rows, c))


def _rope_dim_order(a):
    h, half = ROPE_HALF, LANES // 2
    return jnp.concatenate([a[..., :h], a[..., half:half + h], a[..., 2 * h:half],
                            a[..., h:2 * h], a[..., half + h:]], axis=-1)


def _rope_tables(seq):
    inv_freq = ROPE_THETA ** (-jnp.arange(ROPE_HALF, dtype=F32) * 2.0 / ROPE_DIM)
    ang = jnp.arange(seq).astype(F32)[:, None] * inv_freq[None, :]
    cos, sin = jnp.cos(ang), jnp.sin(ang)
    gap = LANES // 2 - ROPE_HALF
    cos_t = jnp.concatenate([cos, jnp.ones((seq, gap), F32), cos, jnp.ones((seq, gap), F32)], axis=1)
    sin_t = jnp.concatenate([sin, jnp.zeros((seq, gap), F32), -sin, jnp.zeros((seq, gap), F32)],
                            axis=1)
    return _to_planes(cos_t, 1), _to_planes(sin_t, 1)


def _head_weight_cols(w):
    heads = w.reshape(w.shape[:-1] + (N_GROUPS * B_HEADS, B_HD))
    return _rope_dim_order(heads).reshape(w.shape).astype(BF16)


def _head_gain_cols(gain, scale=1.0):
    g = _rope_dim_order(gain * scale)
    return jnp.broadcast_to(g[:, None, :], (N_GROUPS, B_HEADS, B_HD)).reshape(1, -1)


def kernel(x, a_norm, a_w_in, a_o_gain, a_w_out, a_lower_bounds, kv_norm, w_kv, k_gain,
           b_norm, b_w_q, b_q_gain, b_w_out, ffn_norm, ffn_w_in, ffn_w_out):
    batch, seq, d = x.shape
    m = batch * seq
    xf = x.reshape(m, d)

    a_w_in_b, a_w_out_b = a_w_in.astype(BF16), a_w_out.astype(BF16)
    gd = N_GROUPS * D_MODEL
    w_k_b = _head_weight_cols(w_kv[:, :gd])[None]
    w_v_b = w_kv[:, gd:].astype(BF16)[None]
    b_w_q_b, b_w_out_b = _head_weight_cols(b_w_q), b_w_out.astype(BF16)
    ffn_w_in_b, ffn_w_out_b = ffn_w_in.astype(BF16), ffn_w_out.astype(BF16)

    rope_tabs = _rope_tables(seq)
    plane_shape = (batch, SEQ_PLANES, seq // SEQ_PLANES, gd)
    k = v = None
    for layer in range(DEPTH):
        if layer < N_A_LAYERS:
            proj = _norm_matmul(xf, a_norm[layer][None], a_w_in_b, layer, F32)
            og = _hgrn(proj, a_lower_bounds, a_o_gain[layer][None], layer, batch, seq)
            xf = _matmul_residual(og, a_w_out_b, layer, xf)
        else:
            if layer == N_A_LAYERS:
                xf = _to_planes(xf, batch)
                k = _norm_matmul(xf, kv_norm[None], w_k_b, 0, BF16,
                                 rope=(_head_gain_cols(k_gain),) + rope_tabs)
                v = _norm_matmul(xf, kv_norm[None], w_v_b, 0, BF16)
                k, v = k.reshape(plane_shape), v.reshape(plane_shape)
            jb = layer - N_A_LAYERS
            q = _norm_matmul(xf, b_norm[jb][None], b_w_q_b, jb, BF16,
                             rope=(_head_gain_cols(b_q_gain[jb], B_HD ** -0.5),) + rope_tabs)
            q = q.reshape(plane_shape)
            prev = None
            for group in range(N_GROUPS):
                last = group == N_GROUPS - 1
                prev = _attn_group(q, k, v, group, prev, write_lse=not last)
            xf = _matmul_residual(prev[0].reshape(m, d), b_w_out_b, jb, xf)
        xf = _ffn(xf, ffn_norm[layer][None], ffn_w_in_b, ffn_w_out_b, layer)
    return _from_planes(xf, batch).reshape(batch, seq, d)
```

```python
import functools

import jax
import jax.numpy as jnp
from jax import lax
from jax.experimental import pallas as pl
from jax.experimental.pallas import tpu as pltpu

F32 = jnp.float32
BF16 = jnp.bfloat16

D_MODEL = 2048
DEPTH = 4
N_A_LAYERS = DEPTH // 2
A_HEADS = 16
A_DK = D_MODEL // A_HEADS
A_CHUNK = 64
B_HEADS = 16
B_HD = D_MODEL // B_HEADS
B_CONFIGS = ((128, 1), (512, 4), (2048, 16))
N_GROUPS = len(B_CONFIGS)
B_QBLOCK = 128
ROPE_THETA = 500000.0
ROPE_DIM = B_HD // 4
ROPE_HALF = ROPE_DIM // 2
FFN_HIDDEN = ((8 * D_MODEL // 3 + 255) // 256) * 256
EPS = 1e-6

LANES = 128
MXU_COLS = 256
SEQ_PLANES = 8
VMEM_LIMIT_BYTES = 56 * 1024 * 1024
MASK_VALUE = -1e30

NT_DIMS = (((1,), (1,)), ((), ()))
TN_DIMS = (((0,), (0,)), ((), ()))


def _params(*semantics):
    return pltpu.CompilerParams(dimension_semantics=semantics,
                                vmem_limit_bytes=VMEM_LIMIT_BYTES)


def _rms_rows(x, gain):
    ms = jnp.mean(x * x, axis=-1, keepdims=True)
    return x * lax.rsqrt(ms + EPS) * gain


def _norm_matmul_kernel(x_ref, g_ref, w_ref, o_ref, xn_ref):
    @pl.when(pl.program_id(1) == 0)
    def _():
        xn_ref[...] = _rms_rows(x_ref[...], g_ref[...]).astype(BF16)

    o_ref[...] = jnp.dot(xn_ref[...], w_ref[...],
                         preferred_element_type=F32).astype(o_ref.dtype)


def _norm_matmul_rope_kernel(x_ref, g_ref, w_ref, hg_ref, cos_ref, sin_ref, o_ref, xn_ref,
                             acc_ref, scale_ref, rot_ref):
    @pl.when(pl.program_id(1) == 0)
    def _():
        xn_ref[...] = _rms_rows(x_ref[...], g_ref[...]).astype(BF16)

    xn = xn_ref[...]
    tm = xn.shape[0]
    for c in range(o_ref.shape[1] // MXU_COLS):
        acc_ref[...] = jnp.dot(xn, w_ref[:, c * MXU_COLS:(c + 1) * MXU_COLS],
                               preferred_element_type=F32)
        for h in range(MXU_COLS // LANES):
            hs = slice(h * LANES, (h + 1) * LANES)
            cols = slice(c * MXU_COLS + h * LANES, c * MXU_COLS + (h + 1) * LANES)
            y = acc_ref[:, hs]
            ms = jnp.mean(y * y, axis=-1, keepdims=True)
            scale_ref[:, hs] = jnp.broadcast_to(lax.rsqrt(ms + EPS), (tm, LANES)) * hg_ref[:, cols]
        for h in range(MXU_COLS // LANES):
            hs = slice(h * LANES, (h + 1) * LANES)
            y = acc_ref[:, hs] * scale_ref[:, hs]
            rot_ref[:, hs] = pltpu.roll(y * sin_ref[...], LANES // 2, 1)
        for h in range(MXU_COLS // LANES):
            hs = slice(h * LANES, (h + 1) * LANES)
            cols = slice(c * MXU_COLS + h * LANES, c * MXU_COLS + (h + 1) * LANES)
            y = acc_ref[:, hs] * scale_ref[:, hs]
            o_ref[:, cols] = (y * cos_ref[...] + rot_ref[:, hs]).astype(o_ref.dtype)


def _norm_matmul(x, gain, w, layer, out_dtype, *, tm=1024, tn=1024, rope=None):
    m, d = x.shape
    n = w.shape[-1]
    grid = (m // tm, n // tn)
    x_spec = pl.BlockSpec((tm, d), lambda i, j: (i, 0))
    g_spec = pl.BlockSpec((1, d), lambda i, j: (0, 0))
    w_spec = pl.BlockSpec((None, d, tn), lambda i, j: (layer, 0, j))
    o_spec = pl.BlockSpec((tm, tn), lambda i, j: (i, j))
    scratch = [pltpu.VMEM((tm, d), BF16)]
    out_shape = jax.ShapeDtypeStruct((m, n), out_dtype)
    if rope is None:
        return pl.pallas_call(
            _norm_matmul_kernel, grid=grid, in_specs=[x_spec, g_spec, w_spec],
            out_specs=o_spec, out_shape=out_shape, scratch_shapes=scratch,
            compiler_params=_params("parallel", "arbitrary"), name="norm_matmul",
        )(x, gain, w)
    head_gain, cos_t, sin_t = rope
    seq_blocks = cos_t.shape[0] // tm
    tab_spec = pl.BlockSpec((tm, LANES), lambda i, j: (i % seq_blocks, 0))
    hg_spec = pl.BlockSpec((1, tn), lambda i, j: (0, j))
    return pl.pallas_call(
        _norm_matmul_rope_kernel, grid=grid,
        in_specs=[x_spec, g_spec, w_spec, hg_spec, tab_spec, tab_spec],
        out_specs=o_spec, out_shape=out_shape,
        scratch_shapes=scratch + [pltpu.VMEM((tm, MXU_COLS), F32)] * 3,
        compiler_params=_params("parallel", "arbitrary"), name="norm_matmul_rope",
    )(x, gain, w, head_gain, cos_t, sin_t)


def _matmul_residual_kernel(a_ref, w_ref, x_ref, o_ref):
    o_ref[...] = x_ref[...] + jnp.dot(a_ref[...], w_ref[...], preferred_element_type=F32)


def _matmul_residual(a, w, layer, x, *, tm=512):
    m, k = a.shape
    n = w.shape[-1]
    return pl.pallas_call(
        _matmul_residual_kernel, grid=(m // tm,),
        in_specs=[pl.BlockSpec((tm, k), lambda i: (i, 0)),
                  pl.BlockSpec((None, k, n), lambda i: (layer, 0, 0)),
                  pl.BlockSpec((tm, n), lambda i: (i, 0))],
        out_specs=pl.BlockSpec((tm, n), lambda i: (i, 0)),
        out_shape=jax.ShapeDtypeStruct((m, n), F32),
        compiler_params=_params("parallel"), name="matmul_residual",
    )(a, w, x)


def _ffn_kernel(x_ref, g_ref, wg_ref, wu_ref, wo_ref, o_ref, xn_ref):
    j = pl.program_id(1)

    @pl.when(j == 0)
    def _():
        x = x_ref[...]
        xn_ref[...] = _rms_rows(x, g_ref[...]).astype(BF16)
        o_ref[...] = x

    xn = xn_ref[...]
    acts = []
    for c in range(wg_ref.shape[1] // MXU_COLS):
        cs = slice(c * MXU_COLS, (c + 1) * MXU_COLS)
        gate = jnp.dot(xn, wg_ref[:, cs], preferred_element_type=F32)
        up = jnp.dot(xn, wu_ref[:, cs], preferred_element_type=F32)
        acts.append((gate / (1.0 + jnp.exp(-gate)) * up).astype(BF16))
    act = jnp.concatenate(acts, axis=1)
    o_ref[...] += jnp.dot(act, wo_ref[...], preferred_element_type=F32)


def _ffn(x, gain, w_in, w_out, layer, *, tm=512, th=512):
    m, d = x.shape
    hidden = w_out.shape[1]
    n_h = hidden // th
    return pl.pallas_call(
        _ffn_kernel, grid=(m // tm, n_h),
        in_specs=[pl.BlockSpec((tm, d), lambda i, j: (i, 0)),
                  pl.BlockSpec((1, d), lambda i, j: (0, 0)),
                  pl.BlockSpec((None, d, th), lambda i, j: (layer, 0, j)),
                  pl.BlockSpec((None, d, th), lambda i, j: (layer, 0, n_h + j)),
                  pl.BlockSpec((None, th, d), lambda i, j: (layer, j, 0))],
        out_specs=pl.BlockSpec((tm, d), lambda i, j: (i, 0)),
        out_shape=jax.ShapeDtypeStruct((m, d), F32),
        scratch_shapes=[pltpu.VMEM((tm, d), BF16)],
        compiler_params=_params("parallel", "arbitrary"), name="ffn",
    )(x, gain, w_in, w_in, w_out)


def _split3_bf16(x):
    hi = x.astype(BF16)
    r1 = x - hi.astype(F32)
    mid = r1.astype(BF16)
    lo = (r1 - mid.astype(F32)).astype(BF16)
    return hi, mid, lo


def _hgrn_kernel(q_ref, z_ref, i_ref, g_ref, lb_ref, og_ref, o_ref, st_ref, k_scr, b_scr, *,
                 layer, n_chunks):
    C = A_CHUNK
    SUB = 8
    NB = C // SUB

    @pl.when(pl.program_id(2) == 0)
    def _():
        st_ref[...] = jnp.zeros_like(st_ref)

    a = lb_ref[...]
    e = jnp.exp(a - jnp.max(a, axis=0, keepdims=True))
    sm = e / jnp.sum(e, axis=0, keepdims=True)
    lb = jnp.sum(sm[:layer + 1], axis=0, keepdims=True) - sm[0:1]
    lb = jnp.clip(lb, 0.0, 1.0 - 1e-6)
    log_lb = jnp.log(lb)
    one_m_lb = 1.0 - lb
    log_one_m_lb = jnp.log(one_m_lb)
    o_gain = og_ref[...]

    tri = (lax.broadcasted_iota(jnp.int32, (C, C), 0)
           >= lax.broadcasted_iota(jnp.int32, (C, C), 1)).astype(BF16)
    ti = lax.broadcasted_iota(jnp.int32, (C, C), 0)
    si = lax.broadcasted_iota(jnp.int32, (C, C), 1)
    same_32 = (ti // 32) == (si // 32)
    same_16 = (ti // 16) == (si // 16)
    t_sub = lax.broadcasted_iota(jnp.int32, (SUB, 1), 0)
    zero_blk = jnp.zeros((SUB, LANES), F32)

    def blocks(x):
        return [x[j * SUB:(j + 1) * SUB] for j in range(NB)]

    def padded(parts):
        return jnp.concatenate([zero_blk if p is None else p for p in parts],
                               axis=0).astype(BF16)

    st = st_ref[...]
    for c in range(n_chunks):
        r0 = c * C
        rows = slice(r0, r0 + C)
        q, z, v, g = q_ref[rows, :], z_ref[rows, :], i_ref[rows, :], g_ref[rows, :]

        ez = jnp.exp(-jnp.abs(z))
        log_sig = jnp.minimum(z, 0.0) - jnp.log(1.0 + ez)
        k = one_m_lb * (jnp.where(z >= 0.0, ez, 1.0) / (1.0 + ez))
        y = log_one_m_lb + log_sig
        hi_ = jnp.maximum(log_lb, y)
        lo_ = jnp.minimum(log_lb, y)
        log_f = hi_ + jnp.log(1.0 + jnp.exp(lo_ - hi_))

        b = sum(jnp.dot(tri, part, preferred_element_type=F32)
                for part in _split3_bf16(log_f))
        k_scr[rows, :] = k
        b_scr[rows, :] = b

        def bcast_row(ref, r):
            return jnp.broadcast_to(ref[r0 + r:r0 + r + 1, :], (SUB, LANES))

        qs, ks, bs = blocks(q), blocks(k), blocks(b)
        ends = [bcast_row(b_scr, j * SUB + SUB - 1) for j in range(NB)]

        q_dec = jnp.concatenate([qs[j] * jnp.exp(bs[j]) for j in range(NB)], axis=0)
        o = lax.dot_general(q_dec.astype(BF16), st.astype(BF16), NT_DIMS,
                            preferred_element_type=F32)

        a_levels = []
        for size in (4, 2, 1):
            qp, kp = [None] * NB, [None] * NB
            for pair in range(NB // (2 * size)):
                lo = pair * 2 * size
                r = ends[lo + size - 1]
                for j in range(lo, lo + size):
                    kp[j] = ks[j] * jnp.exp(r - bs[j])
                for j in range(lo + size, lo + 2 * size):
                    qp[j] = qs[j] * jnp.exp(bs[j] - r)
            a_levels.append(lax.dot_general(padded(qp), padded(kp), NT_DIMS,
                                            preferred_element_type=F32))
        a_off = (a_levels[0] + jnp.where(same_32, a_levels[1], 0.0)
                 + jnp.where(same_16, a_levels[2], 0.0))
        v_bf = v.astype(BF16)
        o = o + jnp.dot(a_off.astype(BF16), v_bf, preferred_element_type=F32)

        diag = []
        for j in range(NB):
            od = zero_blk
            for s in range(SUB):
                r = j * SUB + s
                w = jnp.exp(bs[j] - bcast_row(b_scr, r))
                a_s = jnp.sum(qs[j] * w * bcast_row(k_scr, r), axis=-1, keepdims=True)
                od = od + jnp.where(t_sub >= s, a_s, 0.0) * bcast_row(i_ref, r)
            diag.append(od)
        o = o + jnp.concatenate(diag, axis=0)

        b_last = ends[NB - 1]
        k_dec = jnp.concatenate([ks[j] * jnp.exp(b_last - bs[j]) for j in range(NB)], axis=0)
        st = st * jnp.exp(b[C - 1:C, :]) + lax.dot_general(
            v_bf, k_dec.astype(BF16), TN_DIMS, preferred_element_type=F32)

        out = _rms_rows(o, o_gain) * (g / (1.0 + jnp.exp(-g)))
        o_ref[rows, :] = out.astype(o_ref.dtype)

    st_ref[...] = st


def _hgrn(proj, a_lower_bounds, o_gain, layer, batch, seq, *, rows=512):
    n_r = seq // rows
    h_ = A_HEADS

    def sec_spec(sec):
        return pl.BlockSpec((rows, A_DK), lambda b, h, r: (b * n_r + r, sec * h_ + h))

    kern = functools.partial(_hgrn_kernel, layer=layer, n_chunks=rows // A_CHUNK)
    return pl.pallas_call(
        kern, grid=(batch, h_, n_r),
        in_specs=[sec_spec(0), sec_spec(1), sec_spec(2), sec_spec(3),
                  pl.BlockSpec((N_A_LAYERS, A_DK), lambda b, h, r: (0, h)),
                  pl.BlockSpec((1, A_DK), lambda b, h, r: (0, 0))],
        out_specs=pl.BlockSpec((rows, A_DK), lambda b, h, r: (b * n_r + r, h)),
        out_shape=jax.ShapeDtypeStruct((batch * seq, D_MODEL), BF16),
        scratch_shapes=[pltpu.VMEM((A_DK, A_DK), F32), pltpu.VMEM((rows, A_DK), F32),
                        pltpu.VMEM((rows, A_DK), F32)],
        compiler_params=_params("parallel", "parallel", "arbitrary"), name="hgrn",
    )(proj, proj, proj, proj, a_lower_bounds, o_gain)


def _attn_kernel(q_ref, kc_ref, vc_ref, kp_ref, vp_ref, o_ref, l_ref, bias_ref, *,
                 planes, tile_rows, key_rows, den, block_rows):
    QB, P, R, KW = B_QBLOCK, planes, tile_rows, key_rows
    n_keys = P * (KW + R)
    n_edge = KW // R
    kc = lax.broadcasted_iota(jnp.int32, (n_keys, QB), 0)
    qa = lax.broadcasted_iota(jnp.int32, (n_keys, QB), 1)
    key_row = kc % (KW + R) - KW
    num = P * (qa % R - key_row) + (qa // R - kc // (KW + R))
    band = (num >= 0) & (num <= QB * den) & (num % den == 0)
    block_row0 = pl.program_id(2) * block_rows
    bias_ref[n_edge] = jnp.where(band, 0.0, MASK_VALUE)
    for t in range(n_edge):
        ok = band & (key_row >= -(block_row0 + t * R))
        bias_ref[t] = jnp.where(ok, 0.0, MASK_VALUE)

    def gather(ref, rows, cols):
        return jnp.concatenate([ref[0, m, rows, cols] for m in range(P)], axis=0)

    for t in range(block_rows // R):
        rows = slice(t * R, (t + 1) * R)
        first = t * R - KW

        def window(cur_ref, prev_ref, cols):
            if first >= 0:
                return gather(cur_ref, slice(first, (t + 1) * R), cols)
            parts = []
            for m in range(P):
                parts += [prev_ref[0, m, KW + first:KW, cols], cur_ref[0, m, 0:(t + 1) * R, cols]]
            return jnp.concatenate(parts, axis=0)

        lse_rows = []
        for h in range(B_HEADS):
            cols = slice(h * B_HD, (h + 1) * B_HD)
            q = gather(q_ref, rows, cols)
            keys = window(kc_ref, kp_ref, cols)
            vals = window(vc_ref, vp_ref, cols)
            s = lax.dot_general(keys, q, NT_DIMS, preferred_element_type=F32)
            s = s + bias_ref[min(t, n_edge)]
            m = jnp.max(s, axis=0, keepdims=True)
            p = jnp.exp(s - m)
            l = jnp.sum(p, axis=0, keepdims=True)
            o = lax.dot_general((p * (1.0 / l)).astype(BF16), vals, TN_DIMS,
                                preferred_element_type=F32).astype(o_ref.dtype)
            for m_ in range(P):
                o_ref[0, m_, rows, cols] = o[m_ * R:(m_ + 1) * R]
            lse_rows.append(m + jnp.log(l))
        pad = jnp.zeros((LANES - B_HEADS, QB), F32)
        lse_tile = jnp.concatenate(lse_rows + [pad], axis=0).T
        for m_ in range(P):
            l_ref[0, m_, rows, :] = lse_tile[m_ * R:(m_ + 1) * R]


def _attn_group(q, k, v, group):
    batch, n_planes, plane_rows, _ = q.shape
    window, dil = B_CONFIGS[group]
    assert window // dil == B_QBLOCK
    planes = max(n_planes // dil, 1)
    den = max(dil // n_planes, 1)
    n_sets = n_planes // planes
    tile_rows = B_QBLOCK // planes
    key_rows = tile_rows * den
    block_rows = plane_rows // planes
    D = D_MODEL

    def view(a):
        return a.reshape(batch, planes, n_sets, plane_rows, a.shape[-1])

    def prev_block(n):
        return jnp.maximum(n * (block_rows // key_rows) - 1, 0)

    cur = pl.BlockSpec((1, planes, None, block_rows, D), lambda b, s, n: (b, 0, s, n, group))
    before = pl.BlockSpec((1, planes, None, key_rows, D),
                          lambda b, s, n: (b, 0, s, prev_block(n), group))
    o_spec = pl.BlockSpec((1, planes, None, block_rows, D), lambda b, s, n: (b, 0, s, n, 0))
    l_spec = pl.BlockSpec((1, planes, None, block_rows, LANES), lambda b, s, n: (b, 0, s, n, 0))
    kern = functools.partial(
        _attn_kernel, planes=planes, tile_rows=tile_rows, key_rows=key_rows, den=den,
        block_rows=block_rows)
    o, lse = pl.pallas_call(
        kern, grid=(batch, n_sets, plane_rows // block_rows),
        in_specs=[cur, cur, cur, before, before], out_specs=[o_spec, l_spec],
        out_shape=[jax.ShapeDtypeStruct((batch, planes, n_sets, plane_rows, D), BF16),
                   jax.ShapeDtypeStruct((batch, planes, n_sets, plane_rows, LANES), F32)],
        scratch_shapes=[pltpu.VMEM((key_rows // tile_rows + 1,
                                    planes * (key_rows + tile_rows), B_QBLOCK), F32)],
        compiler_params=_params("parallel", "parallel", "arbitrary"),
        name=f"attn_g{group}",
    )(view(q), view(k), view(v), view(k), view(v))
    rows = batch * n_planes * plane_rows
    return o.reshape(rows, D), lse.reshape(rows, LANES)


def _mix_matmul_residual_kernel(*refs):
    o_refs, l_refs = refs[:N_GROUPS], refs[N_GROUPS:2 * N_GROUPS]
    w_ref, x_ref, out_ref = refs[2 * N_GROUPS:]
    lses = [l[...] for l in l_refs]
    top = functools.reduce(jnp.maximum, lses)
    es = [jnp.exp(l - top) for l in lses]
    inv = 1.0 / functools.reduce(jnp.add, es)
    spread = (lax.broadcasted_iota(jnp.int32, (LANES, D_MODEL), 1) // B_HD
              == lax.broadcasted_iota(jnp.int32, (LANES, D_MODEL), 0)).astype(BF16)
    mixed = None
    for o_ref, e in zip(o_refs, es):
        wts = jnp.dot((e * inv).astype(BF16), spread, preferred_element_type=F32)
        term = wts * o_ref[...].astype(F32)
        mixed = term if mixed is None else mixed + term
    out_ref[...] = x_ref[...] + jnp.dot(mixed.astype(BF16), w_ref[...],
                                        preferred_element_type=F32)


def _mix_matmul_residual(outs, lses, w, layer, x, *, tm=256):
    m, d = x.shape
    row_spec = pl.BlockSpec((tm, d), lambda i: (i, 0))
    return pl.pallas_call(
        _mix_matmul_residual_kernel, grid=(m // tm,),
        in_specs=[row_spec] * N_GROUPS + [pl.BlockSpec((tm, LANES), lambda i: (i, 0))] * N_GROUPS
        + [pl.BlockSpec((None, d, d), lambda i: (layer, 0, 0)), row_spec],
        out_specs=row_spec, out_shape=jax.ShapeDtypeStruct((m, d), F32),
        compiler_params=_params("parallel"), name="mix_matmul_residual",
    )(*outs, *lses, w, x)


def _to_planes(a, batch):
    rows, c = a.shape
    return (a.reshape(batch, rows // batch // SEQ_PLANES, SEQ_PLANES, c)
            .transpose(0, 2, 1, 3).reshape(rows, c))


def _from_planes(a, batch):
    rows, c = a.shape
    return (a.reshape(batch, SEQ_PLANES, rows // batch // SEQ_PLANES, c)
            .transpose(0, 2, 1, 3).reshape(rows, c))


def _rope_dim_order(a):
    h, half = ROPE_HALF, LANES // 2
    return jnp.concatenate([a[..., :h], a[..., half:half + h], a[..., 2 * h:half],
                            a[..., h:2 * h], a[..., half + h:]], axis=-1)


def _rope_tables(seq):
    inv_freq = ROPE_THETA ** (-jnp.arange(ROPE_HALF, dtype=F32) * 2.0 / ROPE_DIM)
    ang = jnp.arange(seq).astype(F32)[:, None] * inv_freq[None, :]
    cos, sin = jnp.cos(ang), jnp.sin(ang)
    gap = LANES // 2 - ROPE_HALF
    cos_t = jnp.concatenate([cos, jnp.ones((seq, gap), F32), cos, jnp.ones((seq, gap), F32)], axis=1)
    sin_t = jnp.concatenate([sin, jnp.zeros((seq, gap), F32), -sin, jnp.zeros((seq, gap), F32)],
                            axis=1)
    return _to_planes(cos_t, 1), _to_planes(sin_t, 1)


def _head_weight_kernel(w_ref, o_ref):
    h, half = ROPE_HALF, LANES // 2
    lane = lax.broadcasted_iota(jnp.int32, (w_ref.shape[0], LANES), 1)
    take_up = (lane >= h) & (lane < 2 * h)
    take_down = (lane >= half) & (lane < half + h)
    for c in range(w_ref.shape[1] // LANES):
        cols = slice(c * LANES, (c + 1) * LANES)
        y = w_ref[:, cols]
        up = pltpu.roll(y, LANES - (half - h), 1)
        down = pltpu.roll(y, half - h, 1)
        o_ref[:, cols] = jnp.where(take_up, up, jnp.where(take_down, down, y)).astype(BF16)


def _head_weight_cols(w, n_cols, *, tr=512, tn=1024):
    layers, d, _ = w.shape
    return pl.pallas_call(
        _head_weight_kernel, grid=(layers, d // tr, n_cols // tn),
        in_specs=[pl.BlockSpec((None, tr, tn), lambda l, i, j: (l, i, j))],
        out_specs=pl.BlockSpec((None, tr, tn), lambda l, i, j: (l, i, j)),
        out_shape=jax.ShapeDtypeStruct((layers, d, n_cols), BF16),
        compiler_params=_params("parallel", "parallel", "parallel"), name="head_weight_cols",
    )(w)


def _head_gain_cols(gain, scale=1.0):
    g = _rope_dim_order(gain * scale)
    return jnp.broadcast_to(g[:, None, :], (N_GROUPS, B_HEADS, B_HD)).reshape(1, -1)


def kernel(x, a_norm, a_w_in, a_o_gain, a_w_out, a_lower_bounds, kv_norm, w_kv, k_gain,
           b_norm, b_w_q, b_q_gain, b_w_out, ffn_norm, ffn_w_in, ffn_w_out):
    batch, seq, d = x.shape
    m = batch * seq
    xf = x.reshape(m, d)

    a_w_in_b, a_w_out_b = a_w_in.astype(BF16), a_w_out.astype(BF16)
    gd = N_GROUPS * D_MODEL
    w_k_b = _head_weight_cols(w_kv[None], gd)
    w_v_b = w_kv[:, gd:].astype(BF16)[None]
    b_w_q_b, b_w_out_b = _head_weight_cols(b_w_q, gd), b_w_out.astype(BF16)
    ffn_w_in_b, ffn_w_out_b = ffn_w_in.astype(BF16), ffn_w_out.astype(BF16)

    rope_tabs = _rope_tables(seq)
    plane_shape = (batch, SEQ_PLANES, seq // SEQ_PLANES, gd)
    k = v = None
    for layer in range(DEPTH):
        if layer < N_A_LAYERS:
            proj = _norm_matmul(xf, a_norm[layer][None], a_w_in_b, layer, F32)
            og = _hgrn(proj, a_lower_bounds, a_o_gain[layer][None], layer, batch, seq)
            xf = _matmul_residual(og, a_w_out_b, layer, xf)
        else:
            if layer == N_A_LAYERS:
                xf = _to_planes(xf, batch)
                k = _norm_matmul(xf, kv_norm[None], w_k_b, 0, BF16,
                                 rope=(_head_gain_cols(k_gain),) + rope_tabs)
                v = _norm_matmul(xf, kv_norm[None], w_v_b, 0, BF16)
                k, v = k.reshape(plane_shape), v.reshape(plane_shape)
            jb = layer - N_A_LAYERS
            q = _norm_matmul(xf, b_norm[jb][None], b_w_q_b, jb, BF16,
                             rope=(_head_gain_cols(b_q_gain[jb], B_HD ** -0.5),) + rope_tabs)
            q = q.reshape(plane_shape)
            outs, lses = zip(*[_attn_group(q, k, v, group) for group in range(N_GROUPS)])
            xf = _mix_matmul_residual(outs, lses, b_w_out_b, jb, xf)
        xf = _ffn(xf, ffn_norm[layer][None], ffn_w_in_b, ffn_w_out_b, layer)
    return _from_planes(xf, batch).reshape(batch, seq, d)
```

```python
import functools

import jax
import jax.numpy as jnp
from jax import lax
from jax.experimental import pallas as pl
from jax.experimental.pallas import tpu as pltpu

F32 = jnp.float32
BF16 = jnp.bfloat16

D_MODEL = 2048
DEPTH = 4
N_A_LAYERS = DEPTH // 2
A_HEADS = 16
A_DK = D_MODEL // A_HEADS
A_CHUNK = 64
B_HEADS = 16
B_HD = D_MODEL // B_HEADS
B_CONFIGS = ((128, 1), (512, 4), (2048, 16))
N_GROUPS = len(B_CONFIGS)
B_QBLOCK = 128
ROPE_THETA = 500000.0
ROPE_DIM = B_HD // 4
ROPE_HALF = ROPE_DIM // 2
FFN_HIDDEN = ((8 * D_MODEL // 3 + 255) // 256) * 256
EPS = 1e-6

LANES = 128
MXU_COLS = 256
SEQ_PLANES = 8
VMEM_LIMIT_BYTES = 56 * 1024 * 1024
MASK_VALUE = -1e30
LOG2_E = 1.4426950408889634

NT_DIMS = (((1,), (1,)), ((), ()))
TN_DIMS = (((0,), (0,)), ((), ()))


def _params(*semantics):
    return pltpu.CompilerParams(dimension_semantics=semantics,
                                vmem_limit_bytes=VMEM_LIMIT_BYTES)


def _rms_rows(x, gain):
    ms = jnp.mean(x * x, axis=-1, keepdims=True)
    return x * lax.rsqrt(ms + EPS) * gain


def _norm_matmul_kernel(x_ref, g_ref, w_ref, o_ref, xn_ref):
    @pl.when(pl.program_id(1) == 0)
    def _():
        xn_ref[...] = _rms_rows(x_ref[...], g_ref[...]).astype(BF16)

    o_ref[...] = jnp.dot(xn_ref[...], w_ref[...],
                         preferred_element_type=F32).astype(o_ref.dtype)


def _norm_matmul_rope_kernel(x_ref, g_ref, w_ref, hg_ref, cos_ref, sin_ref, o_ref, xn_ref,
                             acc_ref, scale_ref, rot_ref):
    @pl.when(pl.program_id(1) == 0)
    def _():
        xn_ref[...] = _rms_rows(x_ref[...], g_ref[...]).astype(BF16)

    xn = xn_ref[...]
    tm = xn.shape[0]
    for c in range(o_ref.shape[1] // MXU_COLS):
        acc_ref[...] = jnp.dot(xn, w_ref[:, c * MXU_COLS:(c + 1) * MXU_COLS],
                               preferred_element_type=F32)
        for h in range(MXU_COLS // LANES):
            hs = slice(h * LANES, (h + 1) * LANES)
            cols = slice(c * MXU_COLS + h * LANES, c * MXU_COLS + (h + 1) * LANES)
            y = acc_ref[:, hs]
            ms = jnp.mean(y * y, axis=-1, keepdims=True)
            scale_ref[:, hs] = jnp.broadcast_to(lax.rsqrt(ms + EPS), (tm, LANES)) * hg_ref[:, cols]
        for h in range(MXU_COLS // LANES):
            hs = slice(h * LANES, (h + 1) * LANES)
            y = acc_ref[:, hs] * scale_ref[:, hs]
            rot_ref[:, hs] = pltpu.roll(y * sin_ref[...], LANES // 2, 1)
        for h in range(MXU_COLS // LANES):
            hs = slice(h * LANES, (h + 1) * LANES)
            cols = slice(c * MXU_COLS + h * LANES, c * MXU_COLS + (h + 1) * LANES)
            y = acc_ref[:, hs] * scale_ref[:, hs]
            o_ref[:, cols] = (y * cos_ref[...] + rot_ref[:, hs]).astype(o_ref.dtype)


def _norm_matmul(x, gain, w, layer, out_dtype, *, tm=1024, tn=1024, rope=None):
    m, d = x.shape
    n = w.shape[-1]
    grid = (m // tm, n // tn)
    x_spec = pl.BlockSpec((tm, d), lambda i, j: (i, 0))
    g_spec = pl.BlockSpec((1, d), lambda i, j: (0, 0))
    w_spec = pl.BlockSpec((None, d, tn), lambda i, j: (layer, 0, j))
    o_spec = pl.BlockSpec((tm, tn), lambda i, j: (i, j))
    scratch = [pltpu.VMEM((tm, d), BF16)]
    out_shape = jax.ShapeDtypeStruct((m, n), out_dtype)
    if rope is None:
        return pl.pallas_call(
            _norm_matmul_kernel, grid=grid, in_specs=[x_spec, g_spec, w_spec],
            out_specs=o_spec, out_shape=out_shape, scratch_shapes=scratch,
            compiler_params=_params("parallel", "arbitrary"), name="norm_matmul",
        )(x, gain, w)
    head_gain, cos_t, sin_t = rope
    seq_blocks = cos_t.shape[0] // tm
    tab_spec = pl.BlockSpec((tm, LANES), lambda i, j: (i % seq_blocks, 0))
    hg_spec = pl.BlockSpec((1, tn), lambda i, j: (0, j))
    return pl.pallas_call(
        _norm_matmul_rope_kernel, grid=grid,
        in_specs=[x_spec, g_spec, w_spec, hg_spec, tab_spec, tab_spec],
        out_specs=o_spec, out_shape=out_shape,
        scratch_shapes=scratch + [pltpu.VMEM((tm, MXU_COLS), F32)] * 3,
        compiler_params=_params("parallel", "arbitrary"), name="norm_matmul_rope",
    )(x, gain, w, head_gain, cos_t, sin_t)


def _matmul_residual_kernel(a_ref, w_ref, x_ref, o_ref):
    o_ref[...] = x_ref[...] + jnp.dot(a_ref[...], w_ref[...], preferred_element_type=F32)


def _matmul_residual(a, w, layer, x, *, tm=512):
    m, k = a.shape
    n = w.shape[-1]
    return pl.pallas_call(
        _matmul_residual_kernel, grid=(m // tm,),
        in_specs=[pl.BlockSpec((tm, k), lambda i: (i, 0)),
                  pl.BlockSpec((None, k, n), lambda i: (layer, 0, 0)),
                  pl.BlockSpec((tm, n), lambda i: (i, 0))],
        out_specs=pl.BlockSpec((tm, n), lambda i: (i, 0)),
        out_shape=jax.ShapeDtypeStruct((m, n), F32),
        compiler_params=_params("parallel"), name="matmul_residual",
    )(a, w, x)


def _ffn_kernel(x_ref, g_ref, wg_ref, wu_ref, wo_ref, o_ref, xn_ref):
    j = pl.program_id(1)

    @pl.when(j == 0)
    def _():
        x = x_ref[...]
        xn_ref[...] = _rms_rows(x, g_ref[...]).astype(BF16)
        o_ref[...] = x

    xn = xn_ref[...]
    acts = []
    for c in range(wg_ref.shape[1] // MXU_COLS):
        cs = slice(c * MXU_COLS, (c + 1) * MXU_COLS)
        gate = jnp.dot(xn, wg_ref[:, cs], preferred_element_type=F32)
        up = jnp.dot(xn, wu_ref[:, cs], preferred_element_type=F32)
        acts.append((gate / (1.0 + jnp.exp(-gate)) * up).astype(BF16))
    act = jnp.concatenate(acts, axis=1)
    o_ref[...] += jnp.dot(act, wo_ref[...], preferred_element_type=F32)


def _ffn(x, gain, w_in, w_out, layer, *, tm=512, th=512):
    m, d = x.shape
    hidden = w_out.shape[1]
    n_h = hidden // th
    return pl.pallas_call(
        _ffn_kernel, grid=(m // tm, n_h),
        in_specs=[pl.BlockSpec((tm, d), lambda i, j: (i, 0)),
                  pl.BlockSpec((1, d), lambda i, j: (0, 0)),
                  pl.BlockSpec((None, d, th), lambda i, j: (layer, 0, j)),
                  pl.BlockSpec((None, d, th), lambda i, j: (layer, 0, n_h + j)),
                  pl.BlockSpec((None, th, d), lambda i, j: (layer, j, 0))],
        out_specs=pl.BlockSpec((tm, d), lambda i, j: (i, 0)),
        out_shape=jax.ShapeDtypeStruct((m, d), F32),
        scratch_shapes=[pltpu.VMEM((tm, d), BF16)],
        compiler_params=_params("parallel", "arbitrary"), name="ffn",
    )(x, gain, w_in, w_in, w_out)


def _split3_bf16(x):
    hi = x.astype(BF16)
    r1 = x - hi.astype(F32)
    mid = r1.astype(BF16)
    lo = (r1 - mid.astype(F32)).astype(BF16)
    return hi, mid, lo


def _hgrn_kernel(q_ref, z_ref, i_ref, g_ref, lb_ref, og_ref, o_ref, st_ref, c_scr, b_scr, *,
                 layer, n_chunks):
    C = A_CHUNK
    SUB = 8
    NB = C // SUB

    @pl.when(pl.program_id(2) == 0)
    def _():
        st_ref[...] = jnp.zeros_like(st_ref)

    a = lb_ref[...]
    e = jnp.exp(a - jnp.max(a, axis=0, keepdims=True))
    sm = e / jnp.sum(e, axis=0, keepdims=True)
    lb = jnp.sum(sm[:layer + 1], axis=0, keepdims=True) - sm[0:1]
    lb = jnp.clip(lb, 0.0, 1.0 - 1e-6)
    log_lb = jnp.log(lb)
    one_m_lb = 1.0 - lb
    log_one_m_lb = jnp.log(one_m_lb)
    o_gain = og_ref[...]

    tri = (lax.broadcasted_iota(jnp.int32, (C, C), 0)
           >= lax.broadcasted_iota(jnp.int32, (C, C), 1)).astype(BF16)
    ti = lax.broadcasted_iota(jnp.int32, (C, C), 0)
    si = lax.broadcasted_iota(jnp.int32, (C, C), 1)
    same_32 = (ti // 32) == (si // 32)
    same_16 = (ti // 16) == (si // 16)
    s_sub = lax.broadcasted_iota(jnp.int32, (SUB, C), 1)
    zero_blk = jnp.zeros((SUB, LANES), F32)

    def blocks(x):
        return [x[j * SUB:(j + 1) * SUB] for j in range(NB)]

    def padded(parts):
        return jnp.concatenate([zero_blk if p is None else p for p in parts],
                               axis=0).astype(BF16)

    def bcast_row(ref, r):
        return jnp.broadcast_to(ref[r:r + 1, :], (SUB, LANES))

    chunks = [slice(c * C, (c + 1) * C) for c in range(n_chunks)]

    ks_all, bs_all = [], []
    for rows in chunks:
        z = z_ref[rows, :]
        ez = jnp.exp(-jnp.abs(z))
        log_sig = jnp.minimum(z, 0.0) - jnp.log(1.0 + ez)
        k = one_m_lb * (jnp.where(z >= 0.0, ez, 1.0) / (1.0 + ez))
        y = log_one_m_lb + log_sig
        hi_ = jnp.maximum(log_lb, y)
        lo_ = jnp.minimum(log_lb, y)
        log_f = hi_ + jnp.log(1.0 + jnp.exp(lo_ - hi_))
        b = sum(jnp.dot(tri, part, preferred_element_type=F32)
                for part in _split3_bf16(log_f * LOG2_E))
        b_scr[rows, :] = b
        c_scr[rows, :] = b - (log_one_m_lb + log_sig - z) * LOG2_E
        ks_all.append(blocks(k))
        bs_all.append(blocks(b))

    levels_all, upd_all, qdec_all = [], [], []
    for rows, ks, bs in zip(chunks, ks_all, bs_all):
        qs = blocks(q_ref[rows, :])
        ends = [bcast_row(b_scr, rows.start + j * SUB + SUB - 1) for j in range(NB)]
        a_levels = []
        for size in (4, 2, 1):
            qp, kp = [None] * NB, [None] * NB
            for pair in range(NB // (2 * size)):
                lo = pair * 2 * size
                r = ends[lo + size - 1]
                for j in range(lo, lo + size):
                    kp[j] = ks[j] * jnp.exp2(r - bs[j])
                for j in range(lo + size, lo + 2 * size):
                    qp[j] = qs[j] * jnp.exp2(bs[j] - r)
            a_levels.append(lax.dot_general(padded(qp), padded(kp), NT_DIMS,
                                            preferred_element_type=F32))
        levels_all.append(a_levels)
        k_dec = jnp.concatenate([ks[j] * jnp.exp2(ends[NB - 1] - bs[j]) for j in range(NB)],
                                axis=0)
        upd_all.append(lax.dot_general(i_ref[rows, :].astype(BF16), k_dec.astype(BF16), TN_DIMS,
                                       preferred_element_type=F32))
        qdec_all.append(jnp.concatenate([qs[j] * jnp.exp2(bs[j]) for j in range(NB)],
                                        axis=0).astype(BF16))

    a_all = []
    for rows, bs, a_levels in zip(chunks, bs_all, levels_all):
        qs = blocks(q_ref[rows, :])
        diag = [jnp.zeros((SUB, C), F32)] * NB
        for s in range(SUB):
            for j in range(NB):
                r = j * SUB + s
                w = jnp.exp2(bs[j] - bcast_row(c_scr, rows.start + r))
                a_s = jnp.sum(qs[j] * w, axis=-1, keepdims=True)
                diag[j] = jnp.where(s_sub == r, a_s, diag[j])
        a = (a_levels[0] + jnp.where(same_32, a_levels[1], 0.0)
             + jnp.where(same_16, a_levels[2], 0.0)
             + jnp.where(ti >= si, jnp.concatenate(diag, axis=0), 0.0))
        a_all.append(a.astype(BF16))

    st = st_ref[...]
    for rows, a, upd, q_dec in zip(chunks, a_all, upd_all, qdec_all):
        o = (lax.dot_general(q_dec, st.astype(BF16), NT_DIMS, preferred_element_type=F32)
             + jnp.dot(a, i_ref[rows, :].astype(BF16), preferred_element_type=F32))
        st = st * jnp.exp2(b_scr[rows.stop - 1:rows.stop, :]) + upd
        g = g_ref[rows, :]
        out = _rms_rows(o, o_gain) * (g / (1.0 + jnp.exp(-g)))
        o_ref[rows, :] = out.astype(o_ref.dtype)

    st_ref[...] = st


def _hgrn(proj, a_lower_bounds, o_gain, layer, batch, seq, *, rows=512):
    n_r = seq // rows
    h_ = A_HEADS

    def sec_spec(sec):
        return pl.BlockSpec((rows, A_DK), lambda b, h, r: (b * n_r + r, sec * h_ + h))

    kern = functools.partial(_hgrn_kernel, layer=layer, n_chunks=rows // A_CHUNK)
    return pl.pallas_call(
        kern, grid=(batch, h_, n_r),
        in_specs=[sec_spec(0), sec_spec(1), sec_spec(2), sec_spec(3),
                  pl.BlockSpec((N_A_LAYERS, A_DK), lambda b, h, r: (0, h)),
                  pl.BlockSpec((1, A_DK), lambda b, h, r: (0, 0))],
        out_specs=pl.BlockSpec((rows, A_DK), lambda b, h, r: (b * n_r + r, h)),
        out_shape=jax.ShapeDtypeStruct((batch * seq, D_MODEL), BF16),
        scratch_shapes=[pltpu.VMEM((A_DK, A_DK), F32), pltpu.VMEM((rows, A_DK), F32),
                        pltpu.VMEM((rows, A_DK), F32)],
        compiler_params=_params("parallel", "parallel", "arbitrary"), name="hgrn",
    )(proj, proj, proj, proj, a_lower_bounds, o_gain)


def _attn_kernel(q_ref, kc_ref, vc_ref, kp_ref, vp_ref, o_ref, l_ref, bias_ref, *,
                 planes, tile_rows, key_rows, den, block_rows):
    QB, P, R, KW = B_QBLOCK, planes, tile_rows, key_rows
    n_keys = P * (KW + R)
    n_edge = KW // R
    kc = lax.broadcasted_iota(jnp.int32, (n_keys, QB), 0)
    qa = lax.broadcasted_iota(jnp.int32, (n_keys, QB), 1)
    key_row = kc % (KW + R) - KW
    num = P * (qa % R - key_row) + (qa // R - kc // (KW + R))
    band = (num >= 0) & (num <= QB * den) & (num % den == 0)
    block_row0 = pl.program_id(2) * block_rows
    bias_ref[n_edge] = jnp.where(band, 0.0, MASK_VALUE)
    for t in range(n_edge):
        ok = band & (key_row >= -(block_row0 + t * R))
        bias_ref[t] = jnp.where(ok, 0.0, MASK_VALUE)

    def gather(ref, rows, cols):
        return jnp.concatenate([ref[0, m, rows, cols] for m in range(P)], axis=0)

    for t in range(block_rows // R):
        rows = slice(t * R, (t + 1) * R)
        first = t * R - KW

        def window(cur_ref, prev_ref, cols):
            if first >= 0:
                return gather(cur_ref, slice(first, (t + 1) * R), cols)
            parts = []
            for m in range(P):
                parts += [prev_ref[0, m, KW + first:KW, cols], cur_ref[0, m, 0:(t + 1) * R, cols]]
            return jnp.concatenate(parts, axis=0)

        def scores(h):
            cols = slice(h * B_HD, (h + 1) * B_HD)
            return lax.dot_general(window(kc_ref, kp_ref, cols), gather(q_ref, rows, cols),
                                   NT_DIMS, preferred_element_type=F32)

        lse_rows = []
        s_next = scores(0)
        for h in range(B_HEADS):
            cols = slice(h * B_HD, (h + 1) * B_HD)
            vals = window(vc_ref, vp_ref, cols)
            s = s_next + bias_ref[min(t, n_edge)]
            if h + 1 < B_HEADS:
                s_next = scores(h + 1)
            m = jnp.max(s, axis=0, keepdims=True)
            p = jnp.exp(s - m)
            l = jnp.sum(p, axis=0, keepdims=True)
            o = lax.dot_general((p * (1.0 / l)).astype(BF16), vals, TN_DIMS,
                                preferred_element_type=F32).astype(o_ref.dtype)
            for m_ in range(P):
                o_ref[0, m_, rows, cols] = o[m_ * R:(m_ + 1) * R]
            lse_rows.append(m + jnp.log(l))
        pad = jnp.zeros((LANES - B_HEADS, QB), F32)
        lse_tile = jnp.concatenate(lse_rows + [pad], axis=0).T
        for m_ in range(P):
            l_ref[0, m_, rows, :] = lse_tile[m_ * R:(m_ + 1) * R]


def _attn_group(q, k, v, group):
    batch, n_planes, plane_rows, _ = q.shape
    window, dil = B_CONFIGS[group]
    assert window // dil == B_QBLOCK
    planes = max(n_planes // dil, 1)
    den = max(dil // n_planes, 1)
    n_sets = n_planes // planes
    tile_rows = B_QBLOCK // planes
    key_rows = tile_rows * den
    block_rows = plane_rows // planes
    D = D_MODEL

    def view(a):
        return a.reshape(batch, planes, n_sets, plane_rows, a.shape[-1])

    def prev_block(n):
        return jnp.maximum(n * (block_rows // key_rows) - 1, 0)

    cur = pl.BlockSpec((1, planes, None, block_rows, D), lambda b, s, n: (b, 0, s, n, group))
    before = pl.BlockSpec((1, planes, None, key_rows, D),
                          lambda b, s, n: (b, 0, s, prev_block(n), group))
    o_spec = pl.BlockSpec((1, planes, None, block_rows, D), lambda b, s, n: (b, 0, s, n, 0))
    l_spec = pl.BlockSpec((1, planes, None, block_rows, LANES), lambda b, s, n: (b, 0, s, n, 0))
    kern = functools.partial(
        _attn_kernel, planes=planes, tile_rows=tile_rows, key_rows=key_rows, den=den,
        block_rows=block_rows)
    o, lse = pl.pallas_call(
        kern, grid=(batch, n_sets, plane_rows // block_rows),
        in_specs=[cur, cur, cur, before, before], out_specs=[o_spec, l_spec],
        out_shape=[jax.ShapeDtypeStruct((batch, planes, n_sets, plane_rows, D), BF16),
                   jax.ShapeDtypeStruct((batch, planes, n_sets, plane_rows, LANES), F32)],
        scratch_shapes=[pltpu.VMEM((key_rows // tile_rows + 1,
                                    planes * (key_rows + tile_rows), B_QBLOCK), F32)],
        compiler_params=_params("parallel", "parallel", "arbitrary"),
        name=f"attn_g{group}",
    )(view(q), view(k), view(v), view(k), view(v))
    rows = batch * n_planes * plane_rows
    return o.reshape(rows, D), lse.reshape(rows, LANES)


def _mix_matmul_residual_kernel(*refs):
    o_refs, l_refs = refs[:N_GROUPS], refs[N_GROUPS:2 * N_GROUPS]
    w_ref, x_ref, out_ref = refs[2 * N_GROUPS:]
    lses = [l[...] for l in l_refs]
    top = functools.reduce(jnp.maximum, lses)
    es = [jnp.exp(l - top) for l in lses]
    inv = 1.0 / functools.reduce(jnp.add, es)
    spread = (lax.broadcasted_iota(jnp.int32, (LANES, D_MODEL), 1) // B_HD
              == lax.broadcasted_iota(jnp.int32, (LANES, D_MODEL), 0)).astype(BF16)
    mixed = None
    for o_ref, e in zip(o_refs, es):
        wts = jnp.dot((e * inv).astype(BF16), spread, preferred_element_type=F32)
        term = wts * o_ref[...].astype(F32)
        mixed = term if mixed is None else mixed + term
    out_ref[...] = x_ref[...] + jnp.dot(mixed.astype(BF16), w_ref[...],
                                        preferred_element_type=F32)


def _mix_matmul_residual(outs, lses, w, layer, x, *, tm=256):
    m, d = x.shape
    row_spec = pl.BlockSpec((tm, d), lambda i: (i, 0))
    return pl.pallas_call(
        _mix_matmul_residual_kernel, grid=(m // tm,),
        in_specs=[row_spec] * N_GROUPS + [pl.BlockSpec((tm, LANES), lambda i: (i, 0))] * N_GROUPS
        + [pl.BlockSpec((None, d, d), lambda i: (layer, 0, 0)), row_spec],
        out_specs=row_spec, out_shape=jax.ShapeDtypeStruct((m, d), F32),
        compiler_params=_params("parallel"), name="mix_matmul_residual",
    )(*outs, *lses, w, x)


def _to_planes(a, batch):
    rows, c = a.shape
    return (a.reshape(batch, rows // batch // SEQ_PLANES, SEQ_PLANES, c)
            .transpose(0, 2, 1, 3).reshape(rows, c))


def _from_planes(a, batch):
    rows, c = a.shape
    return (a.reshape(batch, SEQ_PLANES, rows // batch // SEQ_PLANES, c)
            .transpose(0, 2, 1, 3).reshape(rows, c))


def _rope_dim_order(a):
    h, half = ROPE_HALF, LANES // 2
    return jnp.concatenate([a[..., :h], a[..., half:half + h], a[..., 2 * h:half],
                            a[..., h:2 * h], a[..., half + h:]], axis=-1)


def _rope_tables(seq):
    inv_freq = ROPE_THETA ** (-jnp.arange(ROPE_HALF, dtype=F32) * 2.0 / ROPE_DIM)
    ang = jnp.arange(seq).astype(F32)[:, None] * inv_freq[None, :]
    cos, sin = jnp.cos(ang), jnp.sin(ang)
    gap = LANES // 2 - ROPE_HALF
    cos_t = jnp.concatenate([cos, jnp.ones((seq, gap), F32), cos, jnp.ones((seq, gap), F32)], axis=1)
    sin_t = jnp.concatenate([sin, jnp.zeros((seq, gap), F32), -sin, jnp.zeros((seq, gap), F32)],
                            axis=1)
    return _to_planes(cos_t, 1), _to_planes(sin_t, 1)


def _head_weight_kernel(w_ref, o_ref):
    h, half = ROPE_HALF, LANES // 2
    lane = lax.broadcasted_iota(jnp.int32, (w_ref.shape[0], LANES), 1)
    take_up = (lane >= h) & (lane < 2 * h)
    take_down = (lane >= half) & (lane < half + h)
    for c in range(w_ref.shape[1] // LANES):
        cols = slice(c * LANES, (c + 1) * LANES)
        y = w_ref[:, cols]
        up = pltpu.roll(y, LANES - (half - h), 1)
        down = pltpu.roll(y, half - h, 1)
        o_ref[:, cols] = jnp.where(take_up, up, jnp.where(take_down, down, y)).astype(BF16)


def _head_weight_cols(w, n_cols, *, tr=512, tn=1024):
    layers, d, _ = w.shape
    return pl.pallas_call(
        _head_weight_kernel, grid=(layers, d // tr, n_cols // tn),
        in_specs=[pl.BlockSpec((None, tr, tn), lambda l, i, j: (l, i, j))],
        out_specs=pl.BlockSpec((None, tr, tn), lambda l, i, j: (l, i, j)),
        out_shape=jax.ShapeDtypeStruct((layers, d, n_cols), BF16),
        compiler_params=_params("parallel", "parallel", "parallel"), name="head_weight_cols",
    )(w)


def _head_gain_cols(gain, scale=1.0):
    g = _rope_dim_order(gain * scale)
    return jnp.broadcast_to(g[:, None, :], (N_GROUPS, B_HEADS, B_HD)).reshape(1, -1)


def kernel(x, a_norm, a_w_in, a_o_gain, a_w_out, a_lower_bounds, kv_norm, w_kv, k_gain,
           b_norm, b_w_q, b_q_gain, b_w_out, ffn_norm, ffn_w_in, ffn_w_out):
    batch, seq, d = x.shape
    m = batch * seq
    xf = x.reshape(m, d)

    a_w_in_b, a_w_out_b = a_w_in.astype(BF16), a_w_out.astype(BF16)
    gd = N_GROUPS * D_MODEL
    w_k_b = _head_weight_cols(w_kv[None], gd)
    w_v_b = w_kv[:, gd:].astype(BF16)[None]
    b_w_q_b, b_w_out_b = _head_weight_cols(b_w_q, gd), b_w_out.astype(BF16)
    ffn_w_in_b, ffn_w_out_b = ffn_w_in.astype(BF16), ffn_w_out.astype(BF16)

    rope_tabs = _rope_tables(seq)
    plane_shape = (batch, SEQ_PLANES, seq // SEQ_PLANES, gd)
    k = v = None
    for layer in range(DEPTH):
        if layer < N_A_LAYERS:
            proj = _norm_matmul(xf, a_norm[layer][None], a_w_in_b, layer, F32)
            og = _hgrn(proj, a_lower_bounds, a_o_gain[layer][None], layer, batch, seq)
            xf = _matmul_residual(og, a_w_out_b, layer, xf)
        else:
            if layer == N_A_LAYERS:
                xf = _to_planes(xf, batch)
                k = _norm_matmul(xf, kv_norm[None], w_k_b, 0, BF16,
                                 rope=(_head_gain_cols(k_gain),) + rope_tabs)
                v = _norm_matmul(xf, kv_norm[None], w_v_b, 0, BF16)
                k, v = k.reshape(plane_shape), v.reshape(plane_shape)
            jb = layer - N_A_LAYERS
            q = _norm_matmul(xf, b_norm[jb][None], b_w_q_b, jb, BF16,
                             rope=(_head_gain_cols(b_q_gain[jb], B_HD ** -0.5),) + rope_tabs)
            q = q.reshape(plane_shape)
            outs, lses = zip(*[_attn_group(q, k, v, group) for group in range(N_GROUPS)])
            xf = _mix_matmul_residual(outs, lses, b_w_out_b, jb, xf)
        xf = _ffn(xf, ffn_norm[layer][None], ffn_w_in_b, ffn_w_out_b, layer)
    return _from_planes(xf, batch).reshape(batch, seq, d)
```

```python
import functools

import jax
import jax.numpy as jnp
from jax import lax
from jax.experimental import pallas as pl
from jax.experimental.pallas import tpu as pltpu

F32 = jnp.float32
BF16 = jnp.bfloat16

D_MODEL = 2048
DEPTH = 4
N_A_LAYERS = DEPTH // 2
A_HEADS = 16
A_DK = D_MODEL // A_HEADS
A_CHUNK = 64
B_HEADS = 16
B_HD = D_MODEL // B_HEADS
B_CONFIGS = ((128, 1), (512, 4), (2048, 16))
N_GROUPS = len(B_CONFIGS)
B_QBLOCK = 128
ROPE_THETA = 500000.0
ROPE_DIM = B_HD // 4
ROPE_HALF = ROPE_DIM // 2
FFN_HIDDEN = ((8 * D_MODEL // 3 + 255) // 256) * 256
EPS = 1e-6

LANES = 128
MXU_COLS = 256
SEQ_PLANES = 8
VMEM_LIMIT_BYTES = 56 * 1024 * 1024
MASK_VALUE = -1e30
LOG2_E = 1.4426950408889634

NT_DIMS = (((1,), (1,)), ((), ()))
TN_DIMS = (((0,), (0,)), ((), ()))


def _params(*semantics):
    return pltpu.CompilerParams(dimension_semantics=semantics,
                                vmem_limit_bytes=VMEM_LIMIT_BYTES)


def _rms_rows(x, gain):
    ms = jnp.mean(x * x, axis=-1, keepdims=True)
    return x * lax.rsqrt(ms + EPS) * gain


def _norm_matmul_kernel(x_ref, g_ref, w_ref, o_ref, xn_ref):
    @pl.when(pl.program_id(1) == 0)
    def _():
        xn_ref[...] = _rms_rows(x_ref[...], g_ref[...]).astype(BF16)

    o_ref[...] = jnp.dot(xn_ref[...], w_ref[...],
                         preferred_element_type=F32).astype(o_ref.dtype)


def _norm_matmul_rope_kernel(x_ref, g_ref, w_ref, hg_ref, cos_ref, sin_ref, o_ref, xn_ref,
                             acc_ref, scale_ref, rot_ref):
    @pl.when(pl.program_id(1) == 0)
    def _():
        xn_ref[...] = _rms_rows(x_ref[...], g_ref[...]).astype(BF16)

    xn = xn_ref[...]
    tm = xn.shape[0]
    for c in range(o_ref.shape[1] // MXU_COLS):
        acc_ref[...] = jnp.dot(xn, w_ref[:, c * MXU_COLS:(c + 1) * MXU_COLS],
                               preferred_element_type=F32)
        for h in range(MXU_COLS // LANES):
            hs = slice(h * LANES, (h + 1) * LANES)
            cols = slice(c * MXU_COLS + h * LANES, c * MXU_COLS + (h + 1) * LANES)
            y = acc_ref[:, hs]
            ms = jnp.mean(y * y, axis=-1, keepdims=True)
            scale_ref[:, hs] = jnp.broadcast_to(lax.rsqrt(ms + EPS), (tm, LANES)) * hg_ref[:, cols]
        for h in range(MXU_COLS // LANES):
            hs = slice(h * LANES, (h + 1) * LANES)
            y = acc_ref[:, hs] * scale_ref[:, hs]
            rot_ref[:, hs] = pltpu.roll(y * sin_ref[...], LANES // 2, 1)
        for h in range(MXU_COLS // LANES):
            hs = slice(h * LANES, (h + 1) * LANES)
            cols = slice(c * MXU_COLS + h * LANES, c * MXU_COLS + (h + 1) * LANES)
            y = acc_ref[:, hs] * scale_ref[:, hs]
            o_ref[:, cols] = (y * cos_ref[...] + rot_ref[:, hs]).astype(o_ref.dtype)


def _norm_matmul(x, gain, w, layer, out_dtype, *, tm=1024, tn=1024, rope=None):
    m, d = x.shape
    n = w.shape[-1]
    if rope is not None:
        tm, tn = 512, 3072
    grid = (m // tm, n // tn)
    x_spec = pl.BlockSpec((tm, d), lambda i, j: (i, 0))
    g_spec = pl.BlockSpec((1, d), lambda i, j: (0, 0))
    w_spec = pl.BlockSpec((None, d, tn), lambda i, j: (layer, 0, j))
    o_spec = pl.BlockSpec((tm, tn), lambda i, j: (i, j))
    scratch = [pltpu.VMEM((tm, d), BF16)]
    out_shape = jax.ShapeDtypeStruct((m, n), out_dtype)
    if rope is None:
        return pl.pallas_call(
            _norm_matmul_kernel, grid=grid, in_specs=[x_spec, g_spec, w_spec],
            out_specs=o_spec, out_shape=out_shape, scratch_shapes=scratch,
            compiler_params=_params("parallel", "arbitrary"), name="norm_matmul",
        )(x, gain, w)
    head_gain, cos_t, sin_t = rope
    seq_blocks = cos_t.shape[0] // tm
    tab_spec = pl.BlockSpec((tm, LANES), lambda i, j: (i % seq_blocks, 0))
    hg_spec = pl.BlockSpec((1, tn), lambda i, j: (0, j))
    return pl.pallas_call(
        _norm_matmul_rope_kernel, grid=grid,
        in_specs=[x_spec, g_spec, w_spec, hg_spec, tab_spec, tab_spec],
        out_specs=o_spec, out_shape=out_shape,
        scratch_shapes=scratch + [pltpu.VMEM((tm, MXU_COLS), F32)] * 3,
        compiler_params=_params("parallel", "arbitrary"), name="norm_matmul_rope",
    )(x, gain, w, head_gain, cos_t, sin_t)


def _matmul_residual_kernel(a_ref, w_ref, x_ref, o_ref):
    o_ref[...] = x_ref[...] + jnp.dot(a_ref[...], w_ref[...], preferred_element_type=F32)


def _matmul_residual(a, w, layer, x, *, tm=512):
    m, k = a.shape
    n = w.shape[-1]
    return pl.pallas_call(
        _matmul_residual_kernel, grid=(m // tm,),
        in_specs=[pl.BlockSpec((tm, k), lambda i: (i, 0)),
                  pl.BlockSpec((None, k, n), lambda i: (layer, 0, 0)),
                  pl.BlockSpec((tm, n), lambda i: (i, 0))],
        out_specs=pl.BlockSpec((tm, n), lambda i: (i, 0)),
        out_shape=jax.ShapeDtypeStruct((m, n), F32),
        compiler_params=_params("parallel"), name="matmul_residual",
    )(a, w, x)


def _ffn_kernel(x_ref, g_ref, wg_ref, wu_ref, wo_ref, o_ref, xn_ref):
    j = pl.program_id(1)

    @pl.when(j == 0)
    def _():
        x = x_ref[...]
        xn_ref[...] = _rms_rows(x, g_ref[...]).astype(BF16)
        o_ref[...] = x

    xn = xn_ref[...]
    acts = []
    th = wg_ref.shape[1]
    for c in range(pl.cdiv(th, MXU_COLS)):
        cs = slice(c * MXU_COLS, min((c + 1) * MXU_COLS, th))
        gate = jnp.dot(xn, wg_ref[:, cs], preferred_element_type=F32)
        up = jnp.dot(xn, wu_ref[:, cs], preferred_element_type=F32)
        acts.append((gate / (1.0 + jnp.exp(-gate)) * up).astype(BF16))
    act = jnp.concatenate(acts, axis=1)
    o_ref[...] += jnp.dot(act, wo_ref[...], preferred_element_type=F32)


def _ffn(x, gain, w_in, w_out, layer, *, tm=1024, th=512):
    m, d = x.shape
    hidden = w_out.shape[1]
    n_h = hidden // th
    return pl.pallas_call(
        _ffn_kernel, grid=(m // tm, n_h),
        in_specs=[pl.BlockSpec((tm, d), lambda i, j: (i, 0)),
                  pl.BlockSpec((1, d), lambda i, j: (0, 0)),
                  pl.BlockSpec((None, d, th), lambda i, j: (layer, 0, j)),
                  pl.BlockSpec((None, d, th), lambda i, j: (layer, 0, n_h + j)),
                  pl.BlockSpec((None, th, d), lambda i, j: (layer, j, 0))],
        out_specs=pl.BlockSpec((tm, d), lambda i, j: (i, 0)),
        out_shape=jax.ShapeDtypeStruct((m, d), F32),
        scratch_shapes=[pltpu.VMEM((tm, d), BF16)],
        compiler_params=_params("parallel", "arbitrary"), name="ffn",
    )(x, gain, w_in, w_in, w_out)


def _split3_bf16(x):
    hi = x.astype(BF16)
    r1 = x - hi.astype(F32)
    mid = r1.astype(BF16)
    lo = (r1 - mid.astype(F32)).astype(BF16)
    return hi, mid, lo


def _hgrn_kernel(q_ref, z_ref, i_ref, g_ref, lb_ref, og_ref, o_ref, st_ref, c_scr, b_scr, *,
                 layer, n_chunks):
    C = A_CHUNK
    SUB = 8
    NB = C // SUB

    @pl.when(pl.program_id(2) == 0)
    def _():
        st_ref[...] = jnp.zeros_like(st_ref)

    a = lb_ref[...]
    e = jnp.exp(a - jnp.max(a, axis=0, keepdims=True))
    sm = e / jnp.sum(e, axis=0, keepdims=True)
    lb = jnp.sum(sm[:layer + 1], axis=0, keepdims=True) - sm[0:1]
    lb = jnp.clip(lb, 0.0, 1.0 - 1e-6)
    log_lb = jnp.log(lb)
    one_m_lb = 1.0 - lb
    log_one_m_lb = jnp.log(one_m_lb)
    o_gain = og_ref[...]

    tri = (lax.broadcasted_iota(jnp.int32, (C, C), 0)
           >= lax.broadcasted_iota(jnp.int32, (C, C), 1)).astype(BF16)
    ti = lax.broadcasted_iota(jnp.int32, (C, C), 0)
    si = lax.broadcasted_iota(jnp.int32, (C, C), 1)
    same_32 = (ti // 32) == (si // 32)
    same_16 = (ti // 16) == (si // 16)
    s_sub = lax.broadcasted_iota(jnp.int32, (SUB, C), 1)
    zero_blk = jnp.zeros((SUB, LANES), F32)

    def blocks(x):
        return [x[j * SUB:(j + 1) * SUB] for j in range(NB)]

    def padded(parts):
        return jnp.concatenate([zero_blk if p is None else p for p in parts],
                               axis=0).astype(BF16)

    def bcast_row(ref, r):
        return jnp.broadcast_to(ref[r:r + 1, :], (SUB, LANES))

    chunks = [slice(c * C, (c + 1) * C) for c in range(n_chunks)]

    ks_all, bs_all = [], []
    for rows in chunks:
        z = z_ref[rows, :]
        ez = jnp.exp(-jnp.abs(z))
        log_sig = jnp.minimum(z, 0.0) - jnp.log(1.0 + ez)
        k = one_m_lb * (jnp.where(z >= 0.0, ez, 1.0) / (1.0 + ez))
        y = log_one_m_lb + log_sig
        hi_ = jnp.maximum(log_lb, y)
        lo_ = jnp.minimum(log_lb, y)
        log_f = hi_ + jnp.log(1.0 + jnp.exp(lo_ - hi_))
        b = sum(jnp.dot(tri, part, preferred_element_type=F32)
                for part in _split3_bf16(log_f * LOG2_E))
        b_scr[rows, :] = b
        c_scr[rows, :] = b - (log_one_m_lb + log_sig - z) * LOG2_E
        ks_all.append(blocks(k))
        bs_all.append(blocks(b))

    levels_all, upd_all, qdec_all = [], [], []
    for rows, ks, bs in zip(chunks, ks_all, bs_all):
        qs = blocks(q_ref[rows, :])
        ends = [bcast_row(b_scr, rows.start + j * SUB + SUB - 1) for j in range(NB)]
        a_levels = []
        for size in (4, 2, 1):
            qp, kp = [None] * NB, [None] * NB
            for pair in range(NB // (2 * size)):
                lo = pair * 2 * size
                r = ends[lo + size - 1]
                for j in range(lo, lo + size):
                    kp[j] = ks[j] * jnp.exp2(r - bs[j])
                for j in range(lo + size, lo + 2 * size):
                    qp[j] = qs[j] * jnp.exp2(bs[j] - r)
            a_levels.append(lax.dot_general(padded(qp), padded(kp), NT_DIMS,
                                            preferred_element_type=F32))
        levels_all.append(a_levels)
        k_dec = jnp.concatenate([ks[j] * jnp.exp2(ends[NB - 1] - bs[j]) for j in range(NB)],
                                axis=0)
        upd_all.append(lax.dot_general(i_ref[rows, :].astype(BF16), k_dec.astype(BF16), TN_DIMS,
                                       preferred_element_type=F32))
        qdec_all.append(jnp.concatenate([qs[j] * jnp.exp2(bs[j]) for j in range(NB)],
                                        axis=0).astype(BF16))

    a_all = []
    for rows, bs, a_levels in zip(chunks, bs_all, levels_all):
        qs = blocks(q_ref[rows, :])
        diag = [jnp.zeros((SUB, C), F32)] * NB
        for s in range(SUB):
            for j in range(NB):
                r = j * SUB + s
                w = jnp.exp2(bs[j] - bcast_row(c_scr, rows.start + r))
                a_s = jnp.sum(qs[j] * w, axis=-1, keepdims=True)
                diag[j] = jnp.where(s_sub == r, a_s, diag[j])
        a = (a_levels[0] + jnp.where(same_32, a_levels[1], 0.0)
             + jnp.where(same_16, a_levels[2], 0.0)
             + jnp.where(ti >= si, jnp.concatenate(diag, axis=0), 0.0))
        a_all.append(a.astype(BF16))

    st = st_ref[...]
    for rows, a, upd, q_dec in zip(chunks, a_all, upd_all, qdec_all):
        o = (lax.dot_general(q_dec, st.astype(BF16), NT_DIMS, preferred_element_type=F32)
             + jnp.dot(a, i_ref[rows, :].astype(BF16), preferred_element_type=F32))
        st = st * jnp.exp2(b_scr[rows.stop - 1:rows.stop, :]) + upd
        g = g_ref[rows, :]
        out = _rms_rows(o, o_gain) * (g / (1.0 + jnp.exp(-g)))
        o_ref[rows, :] = out.astype(o_ref.dtype)

    st_ref[...] = st


def _hgrn(proj, a_lower_bounds, o_gain, layer, batch, seq, *, rows=512):
    n_r = seq // rows
    h_ = A_HEADS

    def sec_spec(sec):
        return pl.BlockSpec((rows, A_DK), lambda b, h, r: (b * n_r + r, sec * h_ + h))

    kern = functools.partial(_hgrn_kernel, layer=layer, n_chunks=rows // A_CHUNK)
    return pl.pallas_call(
        kern, grid=(batch, h_, n_r),
        in_specs=[sec_spec(0), sec_spec(1), sec_spec(2), sec_spec(3),
                  pl.BlockSpec((N_A_LAYERS, A_DK), lambda b, h, r: (0, h)),
                  pl.BlockSpec((1, A_DK), lambda b, h, r: (0, 0))],
        out_specs=pl.BlockSpec((rows, A_DK), lambda b, h, r: (b * n_r + r, h)),
        out_shape=jax.ShapeDtypeStruct((batch * seq, D_MODEL), BF16),
        scratch_shapes=[pltpu.VMEM((A_DK, A_DK), F32), pltpu.VMEM((rows, A_DK), F32),
                        pltpu.VMEM((rows, A_DK), F32)],
        compiler_params=_params("parallel", "parallel", "arbitrary"), name="hgrn",
    )(proj, proj, proj, proj, a_lower_bounds, o_gain)


def _attn_kernel(q_ref, kc_ref, vc_ref, kp_ref, vp_ref, o_ref, l_ref, bias_ref, *,
                 planes, tile_rows, key_rows, den, block_rows):
    QB, P, R, KW = B_QBLOCK, planes, tile_rows, key_rows
    n_keys = P * (KW + R)
    n_edge = KW // R
    kc = lax.broadcasted_iota(jnp.int32, (n_keys, QB), 0)
    qa = lax.broadcasted_iota(jnp.int32, (n_keys, QB), 1)
    key_row = kc % (KW + R) - KW
    num = P * (qa % R - key_row) + (qa // R - kc // (KW + R))
    band = (num >= 0) & (num <= QB * den) & (num % den == 0)
    block_row0 = pl.program_id(2) * block_rows
    bias_ref[n_edge] = jnp.where(band, 0.0, MASK_VALUE)
    for t in range(n_edge):
        ok = band & (key_row >= -(block_row0 + t * R))
        bias_ref[t] = jnp.where(ok, 0.0, MASK_VALUE)

    def gather(ref, rows, cols):
        return jnp.concatenate([ref[0, m, rows, cols] for m in range(P)], axis=0)

    for t in range(block_rows // R):
        rows = slice(t * R, (t + 1) * R)
        first = t * R - KW

        def window(cur_ref, prev_ref, cols):
            if first >= 0:
                return gather(cur_ref, slice(first, (t + 1) * R), cols)
            parts = []
            for m in range(P):
                parts += [prev_ref[0, m, KW + first:KW, cols], cur_ref[0, m, 0:(t + 1) * R, cols]]
            return jnp.concatenate(parts, axis=0)

        def scores(h):
            cols = slice(h * B_HD, (h + 1) * B_HD)
            return lax.dot_general(window(kc_ref, kp_ref, cols), gather(q_ref, rows, cols),
                                   NT_DIMS, preferred_element_type=F32)

        lse_rows = []
        s_next = scores(0)
        for h in range(B_HEADS):
            cols = slice(h * B_HD, (h + 1) * B_HD)
            vals = window(vc_ref, vp_ref, cols)
            s = s_next + bias_ref[min(t, n_edge)]
            if h + 1 < B_HEADS:
                s_next = scores(h + 1)
            m = jnp.max(s, axis=0, keepdims=True)
            p = jnp.exp(s - m)
            l = jnp.sum(p, axis=0, keepdims=True)
            o = lax.dot_general((p * (1.0 / l)).astype(BF16), vals, TN_DIMS,
                                preferred_element_type=F32).astype(o_ref.dtype)
            for m_ in range(P):
                o_ref[0, m_, rows, cols] = o[m_ * R:(m_ + 1) * R]
            lse_rows.append(m + jnp.log(l))
        pad = jnp.zeros((LANES - B_HEADS, QB), F32)
        lse_tile = jnp.concatenate(lse_rows + [pad], axis=0).T
        for m_ in range(P):
            l_ref[0, m_, rows, :] = lse_tile[m_ * R:(m_ + 1) * R]


def _attn_group(q, k, v, group):
    batch, n_planes, plane_rows, _ = q.shape
    window, dil = B_CONFIGS[group]
    assert window // dil == B_QBLOCK
    planes = max(n_planes // dil, 1)
    den = max(dil // n_planes, 1)
    n_sets = n_planes // planes
    tile_rows = B_QBLOCK // planes
    key_rows = tile_rows * den
    block_rows = plane_rows // planes
    D = D_MODEL

    def view(a):
        return a.reshape(batch, planes, n_sets, plane_rows, a.shape[-1])

    def prev_block(n):
        return jnp.maximum(n * (block_rows // key_rows) - 1, 0)

    cur = pl.BlockSpec((1, planes, None, block_rows, D), lambda b, s, n: (b, 0, s, n, group))
    before = pl.BlockSpec((1, planes, None, key_rows, D),
                          lambda b, s, n: (b, 0, s, prev_block(n), group))
    o_spec = pl.BlockSpec((1, planes, None, block_rows, D), lambda b, s, n: (b, 0, s, n, 0))
    l_spec = pl.BlockSpec((1, planes, None, block_rows, LANES), lambda b, s, n: (b, 0, s, n, 0))
    kern = functools.partial(
        _attn_kernel, planes=planes, tile_rows=tile_rows, key_rows=key_rows, den=den,
        block_rows=block_rows)
    o, lse = pl.pallas_call(
        kern, grid=(batch, n_sets, plane_rows // block_rows),
        in_specs=[cur, cur, cur, before, before], out_specs=[o_spec, l_spec],
        out_shape=[jax.ShapeDtypeStruct((batch, planes, n_sets, plane_rows, D), BF16),
                   jax.ShapeDtypeStruct((batch, planes, n_sets, plane_rows, LANES), F32)],
        scratch_shapes=[pltpu.VMEM((key_rows // tile_rows + 1,
                                    planes * (key_rows + tile_rows), B_QBLOCK), F32)],
        compiler_params=_params("parallel", "parallel", "arbitrary"),
        name=f"attn_g{group}",
    )(view(q), view(k), view(v), view(k), view(v))
    rows = batch * n_planes * plane_rows
    return o.reshape(rows, D), lse.reshape(rows, LANES)


def _mix_matmul_residual_kernel(*refs):
    o_refs, l_refs = refs[:N_GROUPS], refs[N_GROUPS:2 * N_GROUPS]
    w_ref, x_ref, out_ref = refs[2 * N_GROUPS:]
    lses = [l[...] for l in l_refs]
    top = functools.reduce(jnp.maximum, lses)
    es = [jnp.exp(l - top) for l in lses]
    inv = 1.0 / functools.reduce(jnp.add, es)
    spread = (lax.broadcasted_iota(jnp.int32, (LANES, D_MODEL), 1) // B_HD
              == lax.broadcasted_iota(jnp.int32, (LANES, D_MODEL), 0)).astype(BF16)
    mixed = None
    for o_ref, e in zip(o_refs, es):
        wts = jnp.dot((e * inv).astype(BF16), spread, preferred_element_type=F32)
        term = wts * o_ref[...].astype(F32)
        mixed = term if mixed is None else mixed + term
    out_ref[...] = x_ref[...] + jnp.dot(mixed.astype(BF16), w_ref[...],
                                        preferred_element_type=F32)


def _mix_matmul_residual(outs, lses, w, layer, x, *, tm=256):
    m, d = x.shape
    row_spec = pl.BlockSpec((tm, d), lambda i: (i, 0))
    return pl.pallas_call(
        _mix_matmul_residual_kernel, grid=(m // tm,),
        in_specs=[row_spec] * N_GROUPS + [pl.BlockSpec((tm, LANES), lambda i: (i, 0))] * N_GROUPS
        + [pl.BlockSpec((None, d, d), lambda i: (layer, 0, 0)), row_spec],
        out_specs=row_spec, out_shape=jax.ShapeDtypeStruct((m, d), F32),
        compiler_params=_params("parallel"), name="mix_matmul_residual",
    )(*outs, *lses, w, x)


def _to_planes(a, batch):
    rows, c = a.shape
    return (a.reshape(batch, rows // batch // SEQ_PLANES, SEQ_PLANES, c)
            .transpose(0, 2, 1, 3).reshape(rows, c))


def _from_planes(a, batch):
    rows, c = a.shape
    return (a.reshape(batch, SEQ_PLANES, rows // batch // SEQ_PLANES, c)
            .transpose(0, 2, 1, 3).reshape(rows, c))


def _rope_dim_order(a):
    h, half = ROPE_HALF, LANES // 2
    return jnp.concatenate([a[..., :h], a[..., half:half + h], a[..., 2 * h:half],
                            a[..., h:2 * h], a[..., half + h:]], axis=-1)


def _rope_tables(seq):
    inv_freq = ROPE_THETA ** (-jnp.arange(ROPE_HALF, dtype=F32) * 2.0 / ROPE_DIM)
    ang = jnp.arange(seq).astype(F32)[:, None] * inv_freq[None, :]
    cos, sin = jnp.cos(ang), jnp.sin(ang)
    gap = LANES // 2 - ROPE_HALF
    cos_t = jnp.concatenate([cos, jnp.ones((seq, gap), F32), cos, jnp.ones((seq, gap), F32)], axis=1)
    sin_t = jnp.concatenate([sin, jnp.zeros((seq, gap), F32), -sin, jnp.zeros((seq, gap), F32)],
                            axis=1)
    return _to_planes(cos_t, 1), _to_planes(sin_t, 1)


def _head_weight_kernel(w_ref, o_ref):
    h, half = ROPE_HALF, LANES // 2
    lane = lax.broadcasted_iota(jnp.int32, (w_ref.shape[0], LANES), 1)
    take_up = (lane >= h) & (lane < 2 * h)
    take_down = (lane >= half) & (lane < half + h)
    for c in range(w_ref.shape[1] // LANES):
        cols = slice(c * LANES, (c + 1) * LANES)
        y = w_ref[:, cols]
        up = pltpu.roll(y, LANES - (half - h), 1)
        down = pltpu.roll(y, half - h, 1)
        o_ref[:, cols] = jnp.where(take_up, up, jnp.where(take_down, down, y)).astype(BF16)


def _head_weight_cols(w, n_cols, *, tr=512, tn=1024):
    layers, d, _ = w.shape
    return pl.pallas_call(
        _head_weight_kernel, grid=(layers, d // tr, n_cols // tn),
        in_specs=[pl.BlockSpec((None, tr, tn), lambda l, i, j: (l, i, j))],
        out_specs=pl.BlockSpec((None, tr, tn), lambda l, i, j: (l, i, j)),
        out_shape=jax.ShapeDtypeStruct((layers, d, n_cols), BF16),
        compiler_params=_params("parallel", "parallel", "parallel"), name="head_weight_cols",
    )(w)


def _head_gain_cols(gain, scale=1.0):
    g = _rope_dim_order(gain * scale)
    return jnp.broadcast_to(g[:, None, :], (N_GROUPS, B_HEADS, B_HD)).reshape(1, -1)


def kernel(x, a_norm, a_w_in, a_o_gain, a_w_out, a_lower_bounds, kv_norm, w_kv, k_gain,
           b_norm, b_w_q, b_q_gain, b_w_out, ffn_norm, ffn_w_in, ffn_w_out):
    batch, seq, d = x.shape
    m = batch * seq
    xf = x.reshape(m, d)

    a_w_in_b, a_w_out_b = a_w_in.astype(BF16), a_w_out.astype(BF16)
    gd = N_GROUPS * D_MODEL
    w_k_b = _head_weight_cols(w_kv[None], gd)
    w_v_b = w_kv[:, gd:].astype(BF16)[None]
    b_w_q_b, b_w_out_b = _head_weight_cols(b_w_q, gd), b_w_out.astype(BF16)
    ffn_w_in_b, ffn_w_out_b = ffn_w_in.astype(BF16), ffn_w_out.astype(BF16)

    rope_tabs = _rope_tables(seq)
    plane_shape = (batch, SEQ_PLANES, seq // SEQ_PLANES, gd)
    k = v = None
    for layer in range(DEPTH):
        if layer < N_A_LAYERS:
            proj = _norm_matmul(xf, a_norm[layer][None], a_w_in_b, layer, F32)
            og = _hgrn(proj, a_lower_bounds, a_o_gain[layer][None], layer, batch, seq)
            xf = _matmul_residual(og, a_w_out_b, layer, xf)
        else:
            if layer == N_A_LAYERS:
                xf = _to_planes(xf, batch)
                k = _norm_matmul(xf, kv_norm[None], w_k_b, 0, BF16,
                                 rope=(_head_gain_cols(k_gain),) + rope_tabs)
                v = _norm_matmul(xf, kv_norm[None], w_v_b, 0, BF16)
                k, v = k.reshape(plane_shape), v.reshape(plane_shape)
            jb = layer - N_A_LAYERS
            q = _norm_matmul(xf, b_norm[jb][None], b_w_q_b, jb, BF16,
                             rope=(_head_gain_cols(b_q_gain[jb], B_HD ** -0.5),) + rope_tabs)
            q = q.reshape(plane_shape)
            outs, lses = zip(*[_attn_group(q, k, v, group) for group in range(N_GROUPS)])
            xf = _mix_matmul_residual(outs, lses, b_w_out_b, jb, xf)
        xf = _ffn(xf, ffn_norm[layer][None], ffn_w_in_b, ffn_w_out_b, layer)
    return _from_planes(xf, batch).reshape(batch, seq, d)
```

```python
import functools

import jax
import jax.numpy as jnp
from jax import lax
from jax.experimental import pallas as pl
from jax.experimental.pallas import tpu as pltpu

F32 = jnp.float32
BF16 = jnp.bfloat16

D_MODEL = 2048
DEPTH = 4
N_A_LAYERS = DEPTH // 2
A_HEADS = 16
A_DK = D_MODEL // A_HEADS
A_CHUNK = 64
B_HEADS = 16
B_HD = D_MODEL // B_HEADS
B_CONFIGS = ((128, 1), (512, 4), (2048, 16))
N_GROUPS = len(B_CONFIGS)
B_QBLOCK = 128
ROPE_THETA = 500000.0
ROPE_DIM = B_HD // 4
ROPE_HALF = ROPE_DIM // 2
FFN_HIDDEN = ((8 * D_MODEL // 3 + 255) // 256) * 256
EPS = 1e-6

LANES = 128
MXU_COLS = 256
SEQ_PLANES = 8
VMEM_LIMIT_BYTES = 56 * 1024 * 1024
MASK_VALUE = -1e30
LOG2_E = 1.4426950408889634

NT_DIMS = (((1,), (1,)), ((), ()))
TN_DIMS = (((0,), (0,)), ((), ()))


def _params(*semantics):
    return pltpu.CompilerParams(dimension_semantics=semantics,
                                vmem_limit_bytes=VMEM_LIMIT_BYTES)


def _rms_rows(x, gain):
    ms = jnp.mean(x * x, axis=-1, keepdims=True)
    return x * lax.rsqrt(ms + EPS) * gain


def _norm_matmul_kernel(x_ref, g_ref, w_ref, o_ref, xn_ref):
    @pl.when(pl.program_id(1) == 0)
    def _():
        xn_ref[...] = _rms_rows(x_ref[...], g_ref[...]).astype(BF16)

    o_ref[...] = jnp.dot(xn_ref[...], w_ref[...],
                         preferred_element_type=F32).astype(o_ref.dtype)


def _norm_matmul_rope_kernel(x_ref, g_ref, w_ref, hg_ref, cos_ref, sin_ref, o_ref, xn_ref,
                             acc_ref, scale_ref, rot_ref):
    @pl.when(pl.program_id(1) == 0)
    def _():
        xn_ref[...] = _rms_rows(x_ref[...], g_ref[...]).astype(BF16)

    xn = xn_ref[...]
    tm = xn.shape[0]
    for c in range(o_ref.shape[1] // MXU_COLS):
        acc_ref[...] = jnp.dot(xn, w_ref[:, c * MXU_COLS:(c + 1) * MXU_COLS],
                               preferred_element_type=F32)
        for h in range(MXU_COLS // LANES):
            hs = slice(h * LANES, (h + 1) * LANES)
            cols = slice(c * MXU_COLS + h * LANES, c * MXU_COLS + (h + 1) * LANES)
            y = acc_ref[:, hs]
            ms = jnp.mean(y * y, axis=-1, keepdims=True)
            scale_ref[:, hs] = jnp.broadcast_to(lax.rsqrt(ms + EPS), (tm, LANES)) * hg_ref[:, cols]
        for h in range(MXU_COLS // LANES):
            hs = slice(h * LANES, (h + 1) * LANES)
            y = acc_ref[:, hs] * scale_ref[:, hs]
            rot_ref[:, hs] = pltpu.roll(y * sin_ref[...], LANES // 2, 1)
        for h in range(MXU_COLS // LANES):
            hs = slice(h * LANES, (h + 1) * LANES)
            cols = slice(c * MXU_COLS + h * LANES, c * MXU_COLS + (h + 1) * LANES)
            y = acc_ref[:, hs] * scale_ref[:, hs]
            o_ref[:, cols] = (y * cos_ref[...] + rot_ref[:, hs]).astype(o_ref.dtype)


def _norm_matmul(x, gain, w, layer, out_dtype, *, tm=1024, tn=1024, rope=None):
    m, d = x.shape
    n = w.shape[-1]
    if rope is not None:
        tm, tn = 512, 3072
    grid = (m // tm, n // tn)
    x_spec = pl.BlockSpec((tm, d), lambda i, j: (i, 0))
    g_spec = pl.BlockSpec((1, d), lambda i, j: (0, 0))
    w_spec = pl.BlockSpec((None, d, tn), lambda i, j: (layer, 0, j))
    o_spec = pl.BlockSpec((tm, tn), lambda i, j: (i, j))
    scratch = [pltpu.VMEM((tm, d), BF16)]
    out_shape = jax.ShapeDtypeStruct((m, n), out_dtype)
    if rope is None:
        return pl.pallas_call(
            _norm_matmul_kernel, grid=grid, in_specs=[x_spec, g_spec, w_spec],
            out_specs=o_spec, out_shape=out_shape, scratch_shapes=scratch,
            compiler_params=_params("parallel", "arbitrary"), name="norm_matmul",
        )(x, gain, w)
    head_gain, cos_t, sin_t = rope
    seq_blocks = cos_t.shape[0] // tm
    tab_spec = pl.BlockSpec((tm, LANES), lambda i, j: (i % seq_blocks, 0))
    hg_spec = pl.BlockSpec((1, tn), lambda i, j: (0, j))
    return pl.pallas_call(
        _norm_matmul_rope_kernel, grid=grid,
        in_specs=[x_spec, g_spec, w_spec, hg_spec, tab_spec, tab_spec],
        out_specs=o_spec, out_shape=out_shape,
        scratch_shapes=scratch + [pltpu.VMEM((tm, MXU_COLS), F32)] * 3,
        compiler_params=_params("parallel", "arbitrary"), name="norm_matmul_rope",
    )(x, gain, w, head_gain, cos_t, sin_t)


def _matmul_residual_kernel(a_ref, w_ref, x_ref, o_ref):
    o_ref[...] = x_ref[...] + jnp.dot(a_ref[...], w_ref[...], preferred_element_type=F32)


def _matmul_residual(a, w, layer, x, *, tm=1024):
    m, k = a.shape
    n = w.shape[-1]
    return pl.pallas_call(
        _matmul_residual_kernel, grid=(m // tm,),
        in_specs=[pl.BlockSpec((tm, k), lambda i: (i, 0)),
                  pl.BlockSpec((None, k, n), lambda i: (layer, 0, 0),
                               pipeline_mode=pl.Buffered(1)),
                  pl.BlockSpec((tm, n), lambda i: (i, 0))],
        out_specs=pl.BlockSpec((tm, n), lambda i: (i, 0)),
        out_shape=jax.ShapeDtypeStruct((m, n), F32),
        compiler_params=_params("parallel"), name="matmul_residual",
    )(a, w, x)


def _ffn_kernel(x_ref, g_ref, wg_ref, wu_ref, wo_ref, o_ref, xn_ref):
    j = pl.program_id(1)

    @pl.when(j == 0)
    def _():
        x = x_ref[...]
        xn_ref[...] = _rms_rows(x, g_ref[...]).astype(BF16)
        o_ref[...] = x

    xn = xn_ref[...]
    acts = []
    th = wg_ref.shape[1]
    for c in range(pl.cdiv(th, MXU_COLS)):
        cs = slice(c * MXU_COLS, min((c + 1) * MXU_COLS, th))
        gate = jnp.dot(xn, wg_ref[:, cs], preferred_element_type=F32)
        up = jnp.dot(xn, wu_ref[:, cs], preferred_element_type=F32)
        acts.append((gate / (1.0 + jnp.exp(-gate)) * up).astype(BF16))
    act = jnp.concatenate(acts, axis=1)
    o_ref[...] += jnp.dot(act, wo_ref[...], preferred_element_type=F32)


def _ffn(x, gain, w_in, w_out, layer, *, tm=1024, th=512):
    m, d = x.shape
    hidden = w_out.shape[1]
    n_h = hidden // th
    return pl.pallas_call(
        _ffn_kernel, grid=(m // tm, n_h),
        in_specs=[pl.BlockSpec((tm, d), lambda i, j: (i, 0)),
                  pl.BlockSpec((1, d), lambda i, j: (0, 0)),
                  pl.BlockSpec((None, d, th), lambda i, j: (layer, 0, j)),
                  pl.BlockSpec((None, d, th), lambda i, j: (layer, 0, n_h + j)),
                  pl.BlockSpec((None, th, d), lambda i, j: (layer, j, 0))],
        out_specs=pl.BlockSpec((tm, d), lambda i, j: (i, 0)),
        out_shape=jax.ShapeDtypeStruct((m, d), F32),
        scratch_shapes=[pltpu.VMEM((tm, d), BF16)],
        compiler_params=_params("parallel", "arbitrary"), name="ffn",
    )(x, gain, w_in, w_in, w_out)


def _split3_bf16(x):
    hi = x.astype(BF16)
    r1 = x - hi.astype(F32)
    mid = r1.astype(BF16)
    lo = (r1 - mid.astype(F32)).astype(BF16)
    return hi, mid, lo


def _hgrn_kernel(q_ref, z_ref, i_ref, g_ref, lb_ref, og_ref, o_ref, st_ref, c_scr, b_scr, *,
                 layer, n_chunks):
    C = A_CHUNK
    SUB = 8
    NB = C // SUB

    @pl.when(pl.program_id(2) == 0)
    def _():
        st_ref[...] = jnp.zeros_like(st_ref)

    a = lb_ref[...]
    e = jnp.exp(a - jnp.max(a, axis=0, keepdims=True))
    sm = e / jnp.sum(e, axis=0, keepdims=True)
    lb = jnp.sum(sm[:layer + 1], axis=0, keepdims=True) - sm[0:1]
    lb = jnp.clip(lb, 0.0, 1.0 - 1e-6)
    log_lb = jnp.log(lb)
    one_m_lb = 1.0 - lb
    log_one_m_lb = jnp.log(one_m_lb)
    o_gain = og_ref[...]

    tri = (lax.broadcasted_iota(jnp.int32, (C, C), 0)
           >= lax.broadcasted_iota(jnp.int32, (C, C), 1)).astype(BF16)
    ti = lax.broadcasted_iota(jnp.int32, (C, C), 0)
    si = lax.broadcasted_iota(jnp.int32, (C, C), 1)
    same_32 = (ti // 32) == (si // 32)
    same_16 = (ti // 16) == (si // 16)
    s_sub = lax.broadcasted_iota(jnp.int32, (SUB, C), 1)
    zero_blk = jnp.zeros((SUB, LANES), F32)

    def blocks(x):
        return [x[j * SUB:(j + 1) * SUB] for j in range(NB)]

    def padded(parts):
        return jnp.concatenate([zero_blk if p is None else p for p in parts],
                               axis=0).astype(BF16)

    def bcast_row(ref, r):
        return jnp.broadcast_to(ref[r:r + 1, :], (SUB, LANES))

    chunks = [slice(c * C, (c + 1) * C) for c in range(n_chunks)]

    ks_all, bs_all = [], []
    for rows in chunks:
        z = z_ref[rows, :]
        ez = jnp.exp(-jnp.abs(z))
        log_sig = jnp.minimum(z, 0.0) - jnp.log(1.0 + ez)
        k = one_m_lb * (jnp.where(z >= 0.0, ez, 1.0) / (1.0 + ez))
        y = log_one_m_lb + log_sig
        hi_ = jnp.maximum(log_lb, y)
        lo_ = jnp.minimum(log_lb, y)
        log_f = hi_ + jnp.log(1.0 + jnp.exp(lo_ - hi_))
        b = sum(jnp.dot(tri, part, preferred_element_type=F32)
                for part in _split3_bf16(log_f * LOG2_E))
        b_scr[rows, :] = b
        c_scr[rows, :] = b - (log_one_m_lb + log_sig - z) * LOG2_E
        ks_all.append(blocks(k))
        bs_all.append(blocks(b))

    levels_all, upd_all, qdec_all = [], [], []
    for rows, ks, bs in zip(chunks, ks_all, bs_all):
        qs = blocks(q_ref[rows, :])
        ends = [bcast_row(b_scr, rows.start + j * SUB + SUB - 1) for j in range(NB)]
        a_levels = []
        for size in (4, 2, 1):
            qp, kp = [None] * NB, [None] * NB
            for pair in range(NB // (2 * size)):
                lo = pair * 2 * size
                r = ends[lo + size - 1]
                for j in range(lo, lo + size):
                    kp[j] = ks[j] * jnp.exp2(r - bs[j])
                for j in range(lo + size, lo + 2 * size):
                    qp[j] = qs[j] * jnp.exp2(bs[j] - r)
            a_levels.append(lax.dot_general(padded(qp), padded(kp), NT_DIMS,
                                            preferred_element_type=F32))
        levels_all.append(a_levels)
        k_dec = jnp.concatenate([ks[j] * jnp.exp2(ends[NB - 1] - bs[j]) for j in range(NB)],
                                axis=0)
        upd_all.append(lax.dot_general(i_ref[rows, :].astype(BF16), k_dec.astype(BF16), TN_DIMS,
                                       preferred_element_type=F32))
        qdec_all.append(jnp.concatenate([qs[j] * jnp.exp2(bs[j]) for j in range(NB)],
                                        axis=0).astype(BF16))

    a_all = []
    for rows, bs, a_levels in zip(chunks, bs_all, levels_all):
        qs = blocks(q_ref[rows, :])
        diag = [jnp.zeros((SUB, C), F32)] * NB
        for s in range(SUB):
            for j in range(NB):
                r = j * SUB + s
                w = jnp.exp2(bs[j] - bcast_row(c_scr, rows.start + r))
                a_s = jnp.sum(qs[j] * w, axis=-1, keepdims=True)
                diag[j] = jnp.where(s_sub == r, a_s, diag[j])
        a = (a_levels[0] + jnp.where(same_32, a_levels[1], 0.0)
             + jnp.where(same_16, a_levels[2], 0.0)
             + jnp.where(ti >= si, jnp.concatenate(diag, axis=0), 0.0))
        a_all.append(a.astype(BF16))

    st = st_ref[...]
    for rows, a, upd, q_dec in zip(chunks, a_all, upd_all, qdec_all):
        o = (lax.dot_general(q_dec, st.astype(BF16), NT_DIMS, preferred_element_type=F32)
             + jnp.dot(a, i_ref[rows, :].astype(BF16), preferred_element_type=F32))
        st = st * jnp.exp2(b_scr[rows.stop - 1:rows.stop, :]) + upd
        g = g_ref[rows, :]
        out = _rms_rows(o, o_gain) * (g / (1.0 + jnp.exp(-g)))
        o_ref[rows, :] = out.astype(o_ref.dtype)

    st_ref[...] = st


def _hgrn(proj, a_lower_bounds, o_gain, layer, batch, seq, *, rows=512):
    n_r = seq // rows
    h_ = A_HEADS

    def sec_spec(sec):
        return pl.BlockSpec((rows, A_DK), lambda b, h, r: (b * n_r + r, sec * h_ + h))

    kern = functools.partial(_hgrn_kernel, layer=layer, n_chunks=rows // A_CHUNK)
    return pl.pallas_call(
        kern, grid=(batch, h_, n_r),
        in_specs=[sec_spec(0), sec_spec(1), sec_spec(2), sec_spec(3),
                  pl.BlockSpec((N_A_LAYERS, A_DK), lambda b, h, r: (0, h)),
                  pl.BlockSpec((1, A_DK), lambda b, h, r: (0, 0))],
        out_specs=pl.BlockSpec((rows, A_DK), lambda b, h, r: (b * n_r + r, h)),
        out_shape=jax.ShapeDtypeStruct((batch * seq, D_MODEL), BF16),
        scratch_shapes=[pltpu.VMEM((A_DK, A_DK), F32), pltpu.VMEM((rows, A_DK), F32),
                        pltpu.VMEM((rows, A_DK), F32)],
        compiler_params=_params("parallel", "parallel", "arbitrary"), name="hgrn",
    )(proj, proj, proj, proj, a_lower_bounds, o_gain)


def _attn_kernel(q_ref, kc_ref, vc_ref, kp_ref, vp_ref, o_ref, l_ref, bias_ref, *,
                 planes, tile_rows, key_rows, den, block_rows):
    QB, P, R, KW = B_QBLOCK, planes, tile_rows, key_rows
    n_keys = P * (KW + R)
    n_edge = KW // R
    kc = lax.broadcasted_iota(jnp.int32, (n_keys, QB), 0)
    qa = lax.broadcasted_iota(jnp.int32, (n_keys, QB), 1)
    key_row = kc % (KW + R) - KW
    num = P * (qa % R - key_row) + (qa // R - kc // (KW + R))
    band = (num >= 0) & (num <= QB * den) & (num % den == 0)
    block_row0 = pl.program_id(2) * block_rows
    bias_ref[n_edge] = jnp.where(band, 0.0, MASK_VALUE)
    for t in range(n_edge):
        ok = band & (key_row >= -(block_row0 + t * R))
        bias_ref[t] = jnp.where(ok, 0.0, MASK_VALUE)

    def gather(ref, rows, cols):
        return jnp.concatenate([ref[0, m, rows, cols] for m in range(P)], axis=0)

    for t in range(block_rows // R):
        rows = slice(t * R, (t + 1) * R)
        first = t * R - KW

        def window(cur_ref, prev_ref, cols):
            if first >= 0:
                return gather(cur_ref, slice(first, (t + 1) * R), cols)
            parts = []
            for m in range(P):
                parts += [prev_ref[0, m, KW + first:KW, cols], cur_ref[0, m, 0:(t + 1) * R, cols]]
            return jnp.concatenate(parts, axis=0)

        def scores(h):
            cols = slice(h * B_HD, (h + 1) * B_HD)
            return lax.dot_general(window(kc_ref, kp_ref, cols), gather(q_ref, rows, cols),
                                   NT_DIMS, preferred_element_type=F32)

        lse_rows = []
        s_next = scores(0)
        for h in range(B_HEADS):
            cols = slice(h * B_HD, (h + 1) * B_HD)
            vals = window(vc_ref, vp_ref, cols)
            s = s_next + bias_ref[min(t, n_edge)]
            if h + 1 < B_HEADS:
                s_next = scores(h + 1)
            m = jnp.max(s, axis=0, keepdims=True)
            p = jnp.exp2(s - m)
            l = jnp.sum(p, axis=0, keepdims=True)
            o = lax.dot_general((p * (1.0 / l)).astype(BF16), vals, TN_DIMS,
                                preferred_element_type=F32).astype(o_ref.dtype)
            for m_ in range(P):
                o_ref[0, m_, rows, cols] = o[m_ * R:(m_ + 1) * R]
            lse_rows.append((m + jnp.log2(l)) * (1.0 / LOG2_E))
        pad = jnp.zeros((LANES - B_HEADS, QB), F32)
        lse_tile = jnp.concatenate(lse_rows + [pad], axis=0).T
        for m_ in range(P):
            l_ref[0, m_, rows, :] = lse_tile[m_ * R:(m_ + 1) * R]


def _attn_group(q, k, v, group):
    batch, n_planes, plane_rows, _ = q.shape
    window, dil = B_CONFIGS[group]
    assert window // dil == B_QBLOCK
    planes = max(n_planes // dil, 1)
    den = max(dil // n_planes, 1)
    n_sets = n_planes // planes
    tile_rows = B_QBLOCK // planes
    key_rows = tile_rows * den
    block_rows = plane_rows // planes
    D = D_MODEL

    def view(a):
        return a.reshape(batch, planes, n_sets, plane_rows, a.shape[-1])

    def prev_block(n):
        return jnp.maximum(n * (block_rows // key_rows) - 1, 0)

    cur = pl.BlockSpec((1, planes, None, block_rows, D), lambda b, s, n: (b, 0, s, n, group))
    before = pl.BlockSpec((1, planes, None, key_rows, D),
                          lambda b, s, n: (b, 0, s, prev_block(n), group))
    o_spec = pl.BlockSpec((1, planes, None, block_rows, D), lambda b, s, n: (b, 0, s, n, 0))
    l_spec = pl.BlockSpec((1, planes, None, block_rows, LANES), lambda b, s, n: (b, 0, s, n, 0))
    kern = functools.partial(
        _attn_kernel, planes=planes, tile_rows=tile_rows, key_rows=key_rows, den=den,
        block_rows=block_rows)
    o, lse = pl.pallas_call(
        kern, grid=(batch, n_sets, plane_rows // block_rows),
        in_specs=[cur, cur, cur, before, before], out_specs=[o_spec, l_spec],
        out_shape=[jax.ShapeDtypeStruct((batch, planes, n_sets, plane_rows, D), BF16),
                   jax.ShapeDtypeStruct((batch, planes, n_sets, plane_rows, LANES), F32)],
        scratch_shapes=[pltpu.VMEM((key_rows // tile_rows + 1,
                                    planes * (key_rows + tile_rows), B_QBLOCK), F32)],
        compiler_params=_params("parallel", "parallel", "arbitrary"),
        name=f"attn_g{group}",
    )(view(q), view(k), view(v), view(k), view(v))
    rows = batch * n_planes * plane_rows
    return o.reshape(rows, D), lse.reshape(rows, LANES)


def _mix_matmul_residual_kernel(*refs):
    o_refs, l_refs = refs[:N_GROUPS], refs[N_GROUPS:2 * N_GROUPS]
    w_ref, x_ref, out_ref = refs[2 * N_GROUPS:]
    lses = [l[...] for l in l_refs]
    top = functools.reduce(jnp.maximum, lses)
    es = [jnp.exp(l - top) for l in lses]
    inv = 1.0 / functools.reduce(jnp.add, es)
    spread = (lax.broadcasted_iota(jnp.int32, (LANES, D_MODEL), 1) // B_HD
              == lax.broadcasted_iota(jnp.int32, (LANES, D_MODEL), 0)).astype(BF16)
    mixed = None
    for o_ref, e in zip(o_refs, es):
        wts = jnp.dot((e * inv).astype(BF16), spread, preferred_element_type=F32)
        term = wts.astype(BF16) * o_ref[...]
        mixed = term if mixed is None else mixed + term
    out_ref[...] = x_ref[...] + jnp.dot(mixed, w_ref[...], preferred_element_type=F32)


def _mix_matmul_residual(outs, lses, w, layer, x, *, tm=512):
    m, d = x.shape
    row_spec = pl.BlockSpec((tm, d), lambda i: (i, 0))
    return pl.pallas_call(
        _mix_matmul_residual_kernel, grid=(m // tm,),
        in_specs=[row_spec] * N_GROUPS + [pl.BlockSpec((tm, LANES), lambda i: (i, 0))] * N_GROUPS
        + [pl.BlockSpec((None, d, d), lambda i: (layer, 0, 0), pipeline_mode=pl.Buffered(1)),
           row_spec],
        out_specs=row_spec, out_shape=jax.ShapeDtypeStruct((m, d), F32),
        compiler_params=_params("parallel"), name="mix_matmul_residual",
    )(*outs, *lses, w, x)


def _to_planes(a, batch):
    rows, c = a.shape
    return (a.reshape(batch, rows // batch // SEQ_PLANES, SEQ_PLANES, c)
            .transpose(0, 2, 1, 3).reshape(rows, c))


def _from_planes(a, batch):
    rows, c = a.shape
    return (a.reshape(batch, SEQ_PLANES, rows // batch // SEQ_PLANES, c)
            .transpose(0, 2, 1, 3).reshape(rows, c))


def _rope_dim_order(a):
    h, half = ROPE_HALF, LANES // 2
    return jnp.concatenate([a[..., :h], a[..., half:half + h], a[..., 2 * h:half],
                            a[..., h:2 * h], a[..., half + h:]], axis=-1)


def _rope_tables(seq):
    inv_freq = ROPE_THETA ** (-jnp.arange(ROPE_HALF, dtype=F32) * 2.0 / ROPE_DIM)
    ang = jnp.arange(seq).astype(F32)[:, None] * inv_freq[None, :]
    cos, sin = jnp.cos(ang), jnp.sin(ang)
    gap = LANES // 2 - ROPE_HALF
    cos_t = jnp.concatenate([cos, jnp.ones((seq, gap), F32), cos, jnp.ones((seq, gap), F32)], axis=1)
    sin_t = jnp.concatenate([sin, jnp.zeros((seq, gap), F32), -sin, jnp.zeros((seq, gap), F32)],
                            axis=1)
    return _to_planes(cos_t, 1), _to_planes(sin_t, 1)


def _head_weight_kernel(w_ref, o_ref):
    h, half = ROPE_HALF, LANES // 2
    lane = lax.broadcasted_iota(jnp.int32, (w_ref.shape[0], LANES), 1)
    take_up = (lane >= h) & (lane < 2 * h)
    take_down = (lane >= half) & (lane < half + h)
    for c in range(w_ref.shape[1] // LANES):
        cols = slice(c * LANES, (c + 1) * LANES)
        y = w_ref[:, cols]
        up = pltpu.roll(y, LANES - (half - h), 1)
        down = pltpu.roll(y, half - h, 1)
        o_ref[:, cols] = jnp.where(take_up, up, jnp.where(take_down, down, y)).astype(BF16)


def _head_weight_cols(w, n_cols, *, tr=512, tn=1024):
    layers, d, _ = w.shape
    return pl.pallas_call(
        _head_weight_kernel, grid=(layers, d // tr, n_cols // tn),
        in_specs=[pl.BlockSpec((None, tr, tn), lambda l, i, j: (l, i, j))],
        out_specs=pl.BlockSpec((None, tr, tn), lambda l, i, j: (l, i, j)),
        out_shape=jax.ShapeDtypeStruct((layers, d, n_cols), BF16),
        compiler_params=_params("parallel", "parallel", "parallel"), name="head_weight_cols",
    )(w)


def _head_gain_cols(gain, scale=1.0):
    g = _rope_dim_order(gain * scale)
    return jnp.broadcast_to(g[:, None, :], (N_GROUPS, B_HEADS, B_HD)).reshape(1, -1)


def kernel(x, a_norm, a_w_in, a_o_gain, a_w_out, a_lower_bounds, kv_norm, w_kv, k_gain,
           b_norm, b_w_q, b_q_gain, b_w_out, ffn_norm, ffn_w_in, ffn_w_out):
    batch, seq, d = x.shape
    m = batch * seq
    xf = x.reshape(m, d)

    a_w_in_b, a_w_out_b = a_w_in.astype(BF16), a_w_out.astype(BF16)
    gd = N_GROUPS * D_MODEL
    w_k_b = _head_weight_cols(w_kv[None], gd)
    w_v_b = w_kv[:, gd:].astype(BF16)[None]
    b_w_q_b, b_w_out_b = _head_weight_cols(b_w_q, gd), b_w_out.astype(BF16)
    ffn_w_in_b, ffn_w_out_b = ffn_w_in.astype(BF16), ffn_w_out.astype(BF16)

    rope_tabs = _rope_tables(seq)
    plane_shape = (batch, SEQ_PLANES, seq // SEQ_PLANES, gd)
    k = v = None
    for layer in range(DEPTH):
        if layer < N_A_LAYERS:
            proj = _norm_matmul(xf, a_norm[layer][None], a_w_in_b, layer, F32, tn=2048)
            og = _hgrn(proj, a_lower_bounds, a_o_gain[layer][None], layer, batch, seq)
            xf = _matmul_residual(og, a_w_out_b, layer, xf)
        else:
            if layer == N_A_LAYERS:
                xf = _to_planes(xf, batch)
                k = _norm_matmul(xf, kv_norm[None], w_k_b, 0, BF16,
                                 rope=(_head_gain_cols(k_gain),) + rope_tabs)
                v = _norm_matmul(xf, kv_norm[None], w_v_b, 0, BF16, tn=2048)
                k, v = k.reshape(plane_shape), v.reshape(plane_shape)
            jb = layer - N_A_LAYERS
            q = _norm_matmul(xf, b_norm[jb][None], b_w_q_b, jb, BF16,
                             rope=(_head_gain_cols(b_q_gain[jb], B_HD ** -0.5 * LOG2_E),)
                             + rope_tabs)
            q = q.reshape(plane_shape)
            outs, lses = zip(*[_attn_group(q, k, v, group) for group in range(N_GROUPS)])
            xf = _mix_matmul_residual(outs, lses, b_w_out_b, jb, xf)
        xf = _ffn(xf, ffn_norm[layer][None], ffn_w_in_b, ffn_w_out_b, layer)
    return _from_planes(xf, batch).reshape(batch, seq, d)
```

```python
import functools

import jax
import jax.numpy as jnp
from jax import lax
from jax.experimental import pallas as pl
from jax.experimental.pallas import tpu as pltpu

F32 = jnp.float32
BF16 = jnp.bfloat16

D_MODEL = 2048
DEPTH = 4
N_A_LAYERS = DEPTH // 2
A_HEADS = 16
A_DK = D_MODEL // A_HEADS
A_CHUNK = 64
B_HEADS = 16
B_HD = D_MODEL // B_HEADS
B_CONFIGS = ((128, 1), (512, 4), (2048, 16))
N_GROUPS = len(B_CONFIGS)
B_QBLOCK = 128
ROPE_THETA = 500000.0
ROPE_DIM = B_HD // 4
ROPE_HALF = ROPE_DIM // 2
FFN_HIDDEN = ((8 * D_MODEL // 3 + 255) // 256) * 256
EPS = 1e-6

LANES = 128
MXU_COLS = 256
SEQ_PLANES = 8
VMEM_LIMIT_BYTES = 56 * 1024 * 1024
MASK_VALUE = -1e30
LOG2_E = 1.4426950408889634

NT_DIMS = (((1,), (1,)), ((), ()))
TN_DIMS = (((0,), (0,)), ((), ()))


def _params(*semantics):
    return pltpu.CompilerParams(dimension_semantics=semantics,
                                vmem_limit_bytes=VMEM_LIMIT_BYTES)


def _rms_rows(x, gain):
    ms = jnp.mean(x * x, axis=-1, keepdims=True)
    return x * lax.rsqrt(ms + EPS) * gain


def _norm_matmul_kernel(x_ref, g_ref, w_ref, o_ref, xn_ref):
    @pl.when(pl.program_id(1) == 0)
    def _():
        xn_ref[...] = _rms_rows(x_ref[...], g_ref[...]).astype(BF16)

    o_ref[...] = jnp.dot(xn_ref[...], w_ref[...],
                         preferred_element_type=F32).astype(o_ref.dtype)


def _norm_matmul_rope_kernel(x_ref, g_ref, w_ref, hg_ref, cos_ref, sin_ref, o_ref, xn_ref,
                             acc_ref, scale_ref, rot_ref):
    @pl.when(pl.program_id(1) == 0)
    def _():
        xn_ref[...] = _rms_rows(x_ref[...], g_ref[...]).astype(BF16)

    xn = xn_ref[...]
    tm = xn.shape[0]
    for c in range(o_ref.shape[1] // MXU_COLS):
        acc_ref[...] = jnp.dot(xn, w_ref[:, c * MXU_COLS:(c + 1) * MXU_COLS],
                               preferred_element_type=F32)
        for h in range(MXU_COLS // LANES):
            hs = slice(h * LANES, (h + 1) * LANES)
            cols = slice(c * MXU_COLS + h * LANES, c * MXU_COLS + (h + 1) * LANES)
            y = acc_ref[:, hs]
            ms = jnp.mean(y * y, axis=-1, keepdims=True)
            scale_ref[:, hs] = jnp.broadcast_to(lax.rsqrt(ms + EPS), (tm, LANES)) * hg_ref[:, cols]
        for h in range(MXU_COLS // LANES):
            hs = slice(h * LANES, (h + 1) * LANES)
            y = acc_ref[:, hs] * scale_ref[:, hs]
            rot_ref[:, hs] = pltpu.roll(y * sin_ref[...], LANES // 2, 1)
        for h in range(MXU_COLS // LANES):
            hs = slice(h * LANES, (h + 1) * LANES)
            cols = slice(c * MXU_COLS + h * LANES, c * MXU_COLS + (h + 1) * LANES)
            y = acc_ref[:, hs] * scale_ref[:, hs]
            o_ref[:, cols] = (y * cos_ref[...] + rot_ref[:, hs]).astype(o_ref.dtype)


def _norm_matmul(x, gain, w, layer, out_dtype, *, tm=1024, tn=1024, rope=None):
    m, d = x.shape
    n = w.shape[-1]
    if rope is not None:
        tm, tn = 512, n
    grid = (m // tm, n // tn)
    x_spec = pl.BlockSpec((tm, d), lambda i, j: (i, 0))
    g_spec = pl.BlockSpec((1, d), lambda i, j: (0, 0))
    w_mode = dict(pipeline_mode=pl.Buffered(1)) if tn == n else {}
    w_spec = pl.BlockSpec((None, d, tn), lambda i, j: (layer, 0, j), **w_mode)
    o_spec = pl.BlockSpec((tm, tn), lambda i, j: (i, j))
    scratch = [pltpu.VMEM((tm, d), BF16)]
    out_shape = jax.ShapeDtypeStruct((m, n), out_dtype)
    if rope is None:
        return pl.pallas_call(
            _norm_matmul_kernel, grid=grid, in_specs=[x_spec, g_spec, w_spec],
            out_specs=o_spec, out_shape=out_shape, scratch_shapes=scratch,
            compiler_params=_params("parallel", "arbitrary"), name="norm_matmul",
        )(x, gain, w)
    head_gain, cos_t, sin_t = rope
    seq_blocks = cos_t.shape[0] // tm
    tab_spec = pl.BlockSpec((tm, LANES), lambda i, j: (i % seq_blocks, 0))
    hg_spec = pl.BlockSpec((1, tn), lambda i, j: (0, j))
    return pl.pallas_call(
        _norm_matmul_rope_kernel, grid=grid,
        in_specs=[x_spec, g_spec, w_spec, hg_spec, tab_spec, tab_spec],
        out_specs=o_spec, out_shape=out_shape,
        scratch_shapes=scratch + [pltpu.VMEM((tm, MXU_COLS), F32)] * 3,
        compiler_params=_params("parallel", "arbitrary"), name="norm_matmul_rope",
    )(x, gain, w, head_gain, cos_t, sin_t)


def _matmul_residual_kernel(a_ref, w_ref, x_ref, o_ref):
    o_ref[...] = x_ref[...] + jnp.dot(a_ref[...], w_ref[...], preferred_element_type=F32)


def _matmul_residual(a, w, layer, x, *, tm=1024):
    m, k = a.shape
    n = w.shape[-1]
    return pl.pallas_call(
        _matmul_residual_kernel, grid=(m // tm,),
        in_specs=[pl.BlockSpec((tm, k), lambda i: (i, 0)),
                  pl.BlockSpec((None, k, n), lambda i: (layer, 0, 0),
                               pipeline_mode=pl.Buffered(1)),
                  pl.BlockSpec((tm, n), lambda i: (i, 0))],
        out_specs=pl.BlockSpec((tm, n), lambda i: (i, 0)),
        out_shape=jax.ShapeDtypeStruct((m, n), F32),
        compiler_params=_params("parallel"), name="matmul_residual",
    )(a, w, x)


def _ffn_kernel(x_ref, g_ref, wg_ref, wu_ref, wo_ref, o_ref, xn_ref):
    j = pl.program_id(1)

    @pl.when(j == 0)
    def _():
        x = x_ref[...]
        xn_ref[...] = _rms_rows(x, g_ref[...]).astype(BF16)
        o_ref[...] = x

    xn = xn_ref[...]
    acts = []
    th = wg_ref.shape[1]
    for c in range(pl.cdiv(th, MXU_COLS)):
        cs = slice(c * MXU_COLS, min((c + 1) * MXU_COLS, th))
        gate = jnp.dot(xn, wg_ref[:, cs], preferred_element_type=F32)
        up = jnp.dot(xn, wu_ref[:, cs], preferred_element_type=F32)
        acts.append((gate / (1.0 + jnp.exp(-gate)) * up).astype(BF16))
    act = jnp.concatenate(acts, axis=1)
    o_ref[...] += jnp.dot(act, wo_ref[...], preferred_element_type=F32)


def _ffn(x, gain, w_in, w_out, layer, *, tm=1024, th=512):
    m, d = x.shape
    hidden = w_out.shape[1]
    n_h = hidden // th
    return pl.pallas_call(
        _ffn_kernel, grid=(m // tm, n_h),
        in_specs=[pl.BlockSpec((tm, d), lambda i, j: (i, 0)),
                  pl.BlockSpec((1, d), lambda i, j: (0, 0)),
                  pl.BlockSpec((None, d, th), lambda i, j: (layer, 0, j)),
                  pl.BlockSpec((None, d, th), lambda i, j: (layer, 0, n_h + j)),
                  pl.BlockSpec((None, th, d), lambda i, j: (layer, j, 0))],
        out_specs=pl.BlockSpec((tm, d), lambda i, j: (i, 0)),
        out_shape=jax.ShapeDtypeStruct((m, d), F32),
        scratch_shapes=[pltpu.VMEM((tm, d), BF16)],
        compiler_params=_params("parallel", "arbitrary"), name="ffn",
    )(x, gain, w_in, w_in, w_out)


def _split3_bf16(x):
    hi = x.astype(BF16)
    r1 = x - hi.astype(F32)
    mid = r1.astype(BF16)
    lo = (r1 - mid.astype(F32)).astype(BF16)
    return hi, mid, lo


def _hgrn_kernel(q_ref, z_ref, i_ref, g_ref, lb_ref, og_ref, o_ref, st_ref, c_scr, b_scr, *,
                 layer, n_chunks):
    C = A_CHUNK
    SUB = 8
    NB = C // SUB

    @pl.when(pl.program_id(2) == 0)
    def _():
        st_ref[...] = jnp.zeros_like(st_ref)

    a = lb_ref[...]
    e = jnp.exp(a - jnp.max(a, axis=0, keepdims=True))
    sm = e / jnp.sum(e, axis=0, keepdims=True)
    lb = jnp.sum(sm[:layer + 1], axis=0, keepdims=True) - sm[0:1]
    lb = jnp.clip(lb, 0.0, 1.0 - 1e-6)
    log_lb = jnp.log(lb)
    one_m_lb = 1.0 - lb
    log_one_m_lb = jnp.log(one_m_lb)
    o_gain = og_ref[...]

    tri = (lax.broadcasted_iota(jnp.int32, (C, C), 0)
           >= lax.broadcasted_iota(jnp.int32, (C, C), 1)).astype(BF16)
    ti = lax.broadcasted_iota(jnp.int32, (C, C), 0)
    si = lax.broadcasted_iota(jnp.int32, (C, C), 1)
    same_32 = (ti // 32) == (si // 32)
    same_16 = (ti // 16) == (si // 16)
    s_sub = lax.broadcasted_iota(jnp.int32, (SUB, C), 1)
    zero_blk = jnp.zeros((SUB, LANES), F32)

    def blocks(x):
        return [x[j * SUB:(j + 1) * SUB] for j in range(NB)]

    def padded(parts):
        return jnp.concatenate([zero_blk if p is None else p for p in parts],
                               axis=0).astype(BF16)

    def bcast_row(ref, r):
        return jnp.broadcast_to(ref[r:r + 1, :], (SUB, LANES))

    chunks = [slice(c * C, (c + 1) * C) for c in range(n_chunks)]

    ks_all, bs_all = [], []
    for rows in chunks:
        z = z_ref[rows, :]
        ez = jnp.exp(-jnp.abs(z))
        log_sig = jnp.minimum(z, 0.0) - jnp.log(1.0 + ez)
        k = one_m_lb * (jnp.where(z >= 0.0, ez, 1.0) / (1.0 + ez))
        y = log_one_m_lb + log_sig
        hi_ = jnp.maximum(log_lb, y)
        lo_ = jnp.minimum(log_lb, y)
        log_f = hi_ + jnp.log(1.0 + jnp.exp(lo_ - hi_))
        b = sum(jnp.dot(tri, part, preferred_element_type=F32)
                for part in _split3_bf16(log_f * LOG2_E))
        b_scr[rows, :] = b
        c_scr[rows, :] = b - (log_one_m_lb + log_sig - z) * LOG2_E
        ks_all.append(blocks(k))
        bs_all.append(blocks(b))

    levels_all, upd_all, qdec_all = [], [], []
    for rows, ks, bs in zip(chunks, ks_all, bs_all):
        qs = blocks(q_ref[rows, :])
        ends = [bcast_row(b_scr, rows.start + j * SUB + SUB - 1) for j in range(NB)]
        a_levels = []
        for size in (4, 2, 1):
            qp, kp = [None] * NB, [None] * NB
            for pair in range(NB // (2 * size)):
                lo = pair * 2 * size
                r = ends[lo + size - 1]
                for j in range(lo, lo + size):
                    kp[j] = ks[j] * jnp.exp2(r - bs[j])
                for j in range(lo + size, lo + 2 * size):
                    qp[j] = qs[j] * jnp.exp2(bs[j] - r)
            a_levels.append(lax.dot_general(padded(qp), padded(kp), NT_DIMS,
                                            preferred_element_type=F32))
        levels_all.append(a_levels)
        k_dec = jnp.concatenate([ks[j] * jnp.exp2(ends[NB - 1] - bs[j]) for j in range(NB)],
                                axis=0)
        upd_all.append(lax.dot_general(i_ref[rows, :].astype(BF16), k_dec.astype(BF16), TN_DIMS,
                                       preferred_element_type=F32))
        qdec_all.append(jnp.concatenate([qs[j] * jnp.exp2(bs[j]) for j in range(NB)],
                                        axis=0).astype(BF16))

    a_all = []
    for rows, bs, a_levels in zip(chunks, bs_all, levels_all):
        qs = blocks(q_ref[rows, :])
        diag = [jnp.zeros((SUB, C), F32)] * NB
        for s in range(SUB):
            for j in range(NB):
                r = j * SUB + s
                w = jnp.exp2(bs[j] - bcast_row(c_scr, rows.start + r))
                a_s = jnp.sum(qs[j] * w, axis=-1, keepdims=True)
                diag[j] = jnp.where(s_sub == r, a_s, diag[j])
        a = (a_levels[0] + jnp.where(same_32, a_levels[1], 0.0)
             + jnp.where(same_16, a_levels[2], 0.0)
             + jnp.where(ti >= si, jnp.concatenate(diag, axis=0), 0.0))
        a_all.append(a.astype(BF16))

    st = st_ref[...]
    for rows, a, upd, q_dec in zip(chunks, a_all, upd_all, qdec_all):
        o = (lax.dot_general(q_dec, st.astype(BF16), NT_DIMS, preferred_element_type=F32)
             + jnp.dot(a, i_ref[rows, :].astype(BF16), preferred_element_type=F32))
        st = st * jnp.exp2(b_scr[rows.stop - 1:rows.stop, :]) + upd
        g = g_ref[rows, :]
        out = _rms_rows(o, o_gain) * (g / (1.0 + jnp.exp(-g)))
        o_ref[rows, :] = out.astype(o_ref.dtype)

    st_ref[...] = st


def _hgrn(proj, a_lower_bounds, o_gain, layer, batch, seq, *, rows=512):
    n_r = seq // rows
    h_ = A_HEADS

    def sec_spec(sec):
        return pl.BlockSpec((rows, A_DK), lambda b, h, r: (b * n_r + r, sec * h_ + h))

    kern = functools.partial(_hgrn_kernel, layer=layer, n_chunks=rows // A_CHUNK)
    return pl.pallas_call(
        kern, grid=(batch, h_, n_r),
        in_specs=[sec_spec(0), sec_spec(1), sec_spec(2), sec_spec(3),
                  pl.BlockSpec((N_A_LAYERS, A_DK), lambda b, h, r: (0, h)),
                  pl.BlockSpec((1, A_DK), lambda b, h, r: (0, 0))],
        out_specs=pl.BlockSpec((rows, A_DK), lambda b, h, r: (b * n_r + r, h)),
        out_shape=jax.ShapeDtypeStruct((batch * seq, D_MODEL), BF16),
        scratch_shapes=[pltpu.VMEM((A_DK, A_DK), F32), pltpu.VMEM((rows, A_DK), F32),
                        pltpu.VMEM((rows, A_DK), F32)],
        compiler_params=_params("parallel", "parallel", "arbitrary"), name="hgrn",
    )(proj, proj, proj, proj, a_lower_bounds, o_gain)


def _attn_kernel(q_ref, kc_ref, vc_ref, kp_ref, vp_ref, o_ref, l_ref, bias_ref, *,
                 planes, tile_rows, key_rows, den, block_rows):
    QB, P, R, KW = B_QBLOCK, planes, tile_rows, key_rows
    n_keys = P * (KW + R)
    n_edge = KW // R
    kc = lax.broadcasted_iota(jnp.int32, (n_keys, QB), 0)
    qa = lax.broadcasted_iota(jnp.int32, (n_keys, QB), 1)
    key_row = kc % (KW + R) - KW
    num = P * (qa % R - key_row) + (qa // R - kc // (KW + R))
    band = (num >= 0) & (num <= QB * den) & (num % den == 0)
    block_row0 = pl.program_id(2) * block_rows
    bias_ref[n_edge] = jnp.where(band, 0.0, MASK_VALUE)
    for t in range(n_edge):
        ok = band & (key_row >= -(block_row0 + t * R))
        bias_ref[t] = jnp.where(ok, 0.0, MASK_VALUE)

    def gather(ref, rows, cols):
        return jnp.concatenate([ref[0, m, rows, cols] for m in range(P)], axis=0)

    for t in range(block_rows // R):
        rows = slice(t * R, (t + 1) * R)
        first = t * R - KW

        def window(cur_ref, prev_ref, cols):
            if first >= 0:
                return gather(cur_ref, slice(first, (t + 1) * R), cols)
            parts = []
            for m in range(P):
                parts += [prev_ref[0, m, KW + first:KW, cols], cur_ref[0, m, 0:(t + 1) * R, cols]]
            return jnp.concatenate(parts, axis=0)

        def scores(h):
            cols = slice(h * B_HD, (h + 1) * B_HD)
            return lax.dot_general(window(kc_ref, kp_ref, cols), gather(q_ref, rows, cols),
                                   NT_DIMS, preferred_element_type=F32)

        lse_rows = []
        s_next = scores(0)
        for h in range(B_HEADS):
            cols = slice(h * B_HD, (h + 1) * B_HD)
            vals = window(vc_ref, vp_ref, cols)
            s = s_next + bias_ref[min(t, n_edge)]
            if h + 1 < B_HEADS:
                s_next = scores(h + 1)
            m = jnp.max(s, axis=0, keepdims=True)
            p = jnp.exp2(s - m)
            l = jnp.sum(p, axis=0, keepdims=True)
            o = lax.dot_general((p * (1.0 / l)).astype(BF16), vals, TN_DIMS,
                                preferred_element_type=F32).astype(o_ref.dtype)
            for m_ in range(P):
                o_ref[0, m_, rows, cols] = o[m_ * R:(m_ + 1) * R]
            lse_rows.append((m + jnp.log2(l)) * (1.0 / LOG2_E))
        pad = jnp.zeros((LANES - B_HEADS, QB), F32)
        lse_tile = jnp.concatenate(lse_rows + [pad], axis=0).T
        for m_ in range(P):
            l_ref[0, m_, rows, :] = lse_tile[m_ * R:(m_ + 1) * R]


def _attn_group(q, k, v, group):
    batch, n_planes, plane_rows, _ = q.shape
    window, dil = B_CONFIGS[group]
    assert window // dil == B_QBLOCK
    planes = max(n_planes // dil, 1)
    den = max(dil // n_planes, 1)
    n_sets = n_planes // planes
    tile_rows = B_QBLOCK // planes
    key_rows = tile_rows * den
    block_rows = plane_rows // planes
    D = D_MODEL

    def view(a):
        return a.reshape(batch, planes, n_sets, plane_rows, a.shape[-1])

    def prev_block(n):
        return jnp.maximum(n * (block_rows // key_rows) - 1, 0)

    cur = pl.BlockSpec((1, planes, None, block_rows, D), lambda b, s, n: (b, 0, s, n, group))
    before = pl.BlockSpec((1, planes, None, key_rows, D),
                          lambda b, s, n: (b, 0, s, prev_block(n), group))
    o_spec = pl.BlockSpec((1, planes, None, block_rows, D), lambda b, s, n: (b, 0, s, n, 0))
    l_spec = pl.BlockSpec((1, planes, None, block_rows, LANES), lambda b, s, n: (b, 0, s, n, 0))
    kern = functools.partial(
        _attn_kernel, planes=planes, tile_rows=tile_rows, key_rows=key_rows, den=den,
        block_rows=block_rows)
    o, lse = pl.pallas_call(
        kern, grid=(batch, n_sets, plane_rows // block_rows),
        in_specs=[cur, cur, cur, before, before], out_specs=[o_spec, l_spec],
        out_shape=[jax.ShapeDtypeStruct((batch, planes, n_sets, plane_rows, D), BF16),
                   jax.ShapeDtypeStruct((batch, planes, n_sets, plane_rows, LANES), F32)],
        scratch_shapes=[pltpu.VMEM((key_rows // tile_rows + 1,
                                    planes * (key_rows + tile_rows), B_QBLOCK), F32)],
        compiler_params=_params("parallel", "parallel", "arbitrary"),
        name=f"attn_g{group}",
    )(view(q), view(k), view(v), view(k), view(v))
    rows = batch * n_planes * plane_rows
    return o.reshape(rows, D), lse.reshape(rows, LANES)


def _mix_matmul_residual_kernel(*refs):
    o_refs, l_refs = refs[:N_GROUPS], refs[N_GROUPS:2 * N_GROUPS]
    w_ref, x_ref, out_ref = refs[2 * N_GROUPS:]
    lses = [l[...] for l in l_refs]
    top = functools.reduce(jnp.maximum, lses)
    es = [jnp.exp(l - top) for l in lses]
    inv = 1.0 / functools.reduce(jnp.add, es)
    spread = (lax.broadcasted_iota(jnp.int32, (LANES, D_MODEL), 1) // B_HD
              == lax.broadcasted_iota(jnp.int32, (LANES, D_MODEL), 0)).astype(BF16)
    mixed = None
    for o_ref, e in zip(o_refs, es):
        wts = jnp.dot((e * inv).astype(BF16), spread, preferred_element_type=F32)
        term = wts.astype(BF16) * o_ref[...]
        mixed = term if mixed is None else mixed + term
    out_ref[...] = x_ref[...] + jnp.dot(mixed, w_ref[...], preferred_element_type=F32)


def _mix_matmul_residual(outs, lses, w, layer, x, *, tm=512):
    m, d = x.shape
    row_spec = pl.BlockSpec((tm, d), lambda i: (i, 0))
    return pl.pallas_call(
        _mix_matmul_residual_kernel, grid=(m // tm,),
        in_specs=[row_spec] * N_GROUPS + [pl.BlockSpec((tm, LANES), lambda i: (i, 0))] * N_GROUPS
        + [pl.BlockSpec((None, d, d), lambda i: (layer, 0, 0), pipeline_mode=pl.Buffered(1)),
           row_spec],
        out_specs=row_spec, out_shape=jax.ShapeDtypeStruct((m, d), F32),
        compiler_params=_params("parallel"), name="mix_matmul_residual",
    )(*outs, *lses, w, x)


def _to_planes(a, batch):
    rows, c = a.shape
    return (a.reshape(batch, rows // batch // SEQ_PLANES, SEQ_PLANES, c)
            .transpose(0, 2, 1, 3).reshape(rows, c))


def _from_planes(a, batch):
    rows, c = a.shape
    return (a.reshape(batch, SEQ_PLANES, rows // batch // SEQ_PLANES, c)
            .transpose(0, 2, 1, 3).reshape(rows, c))


def _rope_dim_order(a):
    h, half = ROPE_HALF, LANES // 2
    return jnp.concatenate([a[..., :h], a[..., half:half + h], a[..., 2 * h:half],
                            a[..., h:2 * h], a[..., half + h:]], axis=-1)


def _rope_tables(seq):
    inv_freq = ROPE_THETA ** (-jnp.arange(ROPE_HALF, dtype=F32) * 2.0 / ROPE_DIM)
    ang = jnp.arange(seq).astype(F32)[:, None] * inv_freq[None, :]
    cos, sin = jnp.cos(ang), jnp.sin(ang)
    gap = LANES // 2 - ROPE_HALF
    cos_t = jnp.concatenate([cos, jnp.ones((seq, gap), F32), cos, jnp.ones((seq, gap), F32)], axis=1)
    sin_t = jnp.concatenate([sin, jnp.zeros((seq, gap), F32), -sin, jnp.zeros((seq, gap), F32)],
                            axis=1)
    return _to_planes(cos_t, 1), _to_planes(sin_t, 1)


def _head_weight_kernel(w_ref, o_ref):
    h, half = ROPE_HALF, LANES // 2
    src = lax.broadcasted_iota(jnp.int32, (MXU_COLS, MXU_COLS), 0)
    dst = lax.broadcasted_iota(jnp.int32, (MXU_COLS, MXU_COLS), 1)
    d = dst % LANES
    from_lane = jnp.where((d >= h) & (d < 2 * h), d + (half - h),
                          jnp.where((d >= half) & (d < half + h), d - (half - h), d))
    perm = (src == (dst - d) + from_lane).astype(BF16)
    for c in range(w_ref.shape[1] // MXU_COLS):
        cols = slice(c * MXU_COLS, (c + 1) * MXU_COLS)
        o_ref[:, cols] = jnp.dot(w_ref[:, cols].astype(BF16), perm,
                                 preferred_element_type=F32).astype(BF16)


def _head_weight_cols(w, n_cols, *, tr=512, tn=1024):
    layers, d, _ = w.shape
    return pl.pallas_call(
        _head_weight_kernel, grid=(layers, d // tr, n_cols // tn),
        in_specs=[pl.BlockSpec((None, tr, tn), lambda l, i, j: (l, i, j))],
        out_specs=pl.BlockSpec((None, tr, tn), lambda l, i, j: (l, i, j)),
        out_shape=jax.ShapeDtypeStruct((layers, d, n_cols), BF16),
        compiler_params=_params("parallel", "parallel", "parallel"), name="head_weight_cols",
    )(w)


def _head_gain_cols(gain, scale=1.0):
    g = _rope_dim_order(gain * scale)
    return jnp.broadcast_to(g[:, None, :], (N_GROUPS, B_HEADS, B_HD)).reshape(1, -1)


def kernel(x, a_norm, a_w_in, a_o_gain, a_w_out, a_lower_bounds, kv_norm, w_kv, k_gain,
           b_norm, b_w_q, b_q_gain, b_w_out, ffn_norm, ffn_w_in, ffn_w_out):
    batch, seq, d = x.shape
    m = batch * seq
    xf = x.reshape(m, d)

    a_w_in_b, a_w_out_b = a_w_in.astype(BF16), a_w_out.astype(BF16)
    gd = N_GROUPS * D_MODEL
    w_k_b = _head_weight_cols(w_kv[None], gd)
    w_v_b = w_kv[:, gd:].astype(BF16)[None]
    b_w_q_b, b_w_out_b = _head_weight_cols(b_w_q, gd), b_w_out.astype(BF16)
    ffn_w_in_b, ffn_w_out_b = ffn_w_in.astype(BF16), ffn_w_out.astype(BF16)

    rope_tabs = _rope_tables(seq)
    plane_shape = (batch, SEQ_PLANES, seq // SEQ_PLANES, gd)
    k = v = None
    for layer in range(DEPTH):
        if layer < N_A_LAYERS:
            proj = _norm_matmul(xf, a_norm[layer][None], a_w_in_b, layer, F32, tn=2048)
            og = _hgrn(proj, a_lower_bounds, a_o_gain[layer][None], layer, batch, seq)
            xf = _matmul_residual(og, a_w_out_b, layer, xf)
        else:
            if layer == N_A_LAYERS:
                xf = _to_planes(xf, batch)
                k = _norm_matmul(xf, kv_norm[None], w_k_b, 0, BF16,
                                 rope=(_head_gain_cols(k_gain),) + rope_tabs)
                v = _norm_matmul(xf, kv_norm[None], w_v_b, 0, BF16, tn=2048)
                k, v = k.reshape(plane_shape), v.reshape(plane_shape)
            jb = layer - N_A_LAYERS
            q = _norm_matmul(xf, b_norm[jb][None], b_w_q_b, jb, BF16,
                             rope=(_head_gain_cols(b_q_gain[jb], B_HD ** -0.5 * LOG2_E),)
                             + rope_tabs)
            q = q.reshape(plane_shape)
            outs, lses = zip(*[_attn_group(q, k, v, group) for group in range(N_GROUPS)])
            xf = _mix_matmul_residual(outs, lses, b_w_out_b, jb, xf)
        xf = _ffn(xf, ffn_norm[layer][None], ffn_w_in_b, ffn_w_out_b, layer)
    return _from_planes(xf, batch).reshape(batch, seq, d)
```

```python
import functools

import jax
import jax.numpy as jnp
from jax import lax
from jax.experimental import pallas as pl
from jax.experimental.pallas import tpu as pltpu

F32 = jnp.float32
BF16 = jnp.bfloat16

D_MODEL = 2048
DEPTH = 4
N_A_LAYERS = DEPTH // 2
A_HEADS = 16
A_DK = D_MODEL // A_HEADS
A_CHUNK = 64
B_HEADS = 16
B_HD = D_MODEL // B_HEADS
B_CONFIGS = ((128, 1), (512, 4), (2048, 16))
N_GROUPS = len(B_CONFIGS)
B_QBLOCK = 128
ROPE_THETA = 500000.0
ROPE_DIM = B_HD // 4
ROPE_HALF = ROPE_DIM // 2
FFN_HIDDEN = ((8 * D_MODEL // 3 + 255) // 256) * 256
EPS = 1e-6

LANES = 128
MXU_COLS = 256
SEQ_PLANES = 8
VMEM_LIMIT_BYTES = 56 * 1024 * 1024
MASK_VALUE = -1e30
LOG2_E = 1.4426950408889634

NT_DIMS = (((1,), (1,)), ((), ()))
TN_DIMS = (((0,), (0,)), ((), ()))


def _params(*semantics):
    return pltpu.CompilerParams(dimension_semantics=semantics,
                                vmem_limit_bytes=VMEM_LIMIT_BYTES)


def _rms_rows(x, gain):
    ms = jnp.mean(x * x, axis=-1, keepdims=True)
    return x * lax.rsqrt(ms + EPS) * gain


def _norm_matmul_kernel(x_ref, g_ref, w_ref, o_ref, xn_ref):
    @pl.when(pl.program_id(1) == 0)
    def _():
        xn_ref[...] = _rms_rows(x_ref[...], g_ref[...]).astype(BF16)

    o_ref[...] = jnp.dot(xn_ref[...], w_ref[...],
                         preferred_element_type=F32).astype(o_ref.dtype)


def _norm_matmul_rope_kernel(x_ref, g_ref, w_ref, hg_ref, cos_ref, sin_ref, o_ref, xn_ref,
                             acc_ref, scale_ref, rot_ref):
    @pl.when(pl.program_id(1) == 0)
    def _():
        xn_ref[...] = _rms_rows(x_ref[...], g_ref[...]).astype(BF16)

    xn = xn_ref[...]
    tm = xn.shape[0]
    for c in range(o_ref.shape[1] // MXU_COLS):
        acc_ref[...] = jnp.dot(xn, w_ref[:, c * MXU_COLS:(c + 1) * MXU_COLS],
                               preferred_element_type=F32)
        for h in range(MXU_COLS // LANES):
            hs = slice(h * LANES, (h + 1) * LANES)
            cols = slice(c * MXU_COLS + h * LANES, c * MXU_COLS + (h + 1) * LANES)
            y = acc_ref[:, hs]
            ms = jnp.mean(y * y, axis=-1, keepdims=True)
            scale_ref[:, hs] = jnp.broadcast_to(lax.rsqrt(ms + EPS), (tm, LANES)) * hg_ref[:, cols]
        for h in range(MXU_COLS // LANES):
            hs = slice(h * LANES, (h + 1) * LANES)
            y = acc_ref[:, hs] * scale_ref[:, hs]
            rot_ref[:, hs] = pltpu.roll(y * sin_ref[...], LANES // 2, 1)
        for h in range(MXU_COLS // LANES):
            hs = slice(h * LANES, (h + 1) * LANES)
            cols = slice(c * MXU_COLS + h * LANES, c * MXU_COLS + (h + 1) * LANES)
            y = acc_ref[:, hs] * scale_ref[:, hs]
            o_ref[:, cols] = (y * cos_ref[...] + rot_ref[:, hs]).astype(o_ref.dtype)


def _norm_matmul(x, gain, w, layer, out_dtype, *, tm=1024, tn=1024, rope=None):
    m, d = x.shape
    n = w.shape[-1]
    if rope is not None:
        tm, tn = 512, n
    grid = (m // tm, n // tn)
    x_spec = pl.BlockSpec((tm, d), lambda i, j: (i, 0))
    g_spec = pl.BlockSpec((1, d), lambda i, j: (0, 0))
    w_mode = dict(pipeline_mode=pl.Buffered(1)) if tn == n else {}
    w_spec = pl.BlockSpec((None, d, tn), lambda i, j: (layer, 0, j), **w_mode)
    o_spec = pl.BlockSpec((tm, tn), lambda i, j: (i, j))
    scratch = [pltpu.VMEM((tm, d), BF16)]
    out_shape = jax.ShapeDtypeStruct((m, n), out_dtype)
    if rope is None:
        return pl.pallas_call(
            _norm_matmul_kernel, grid=grid, in_specs=[x_spec, g_spec, w_spec],
            out_specs=o_spec, out_shape=out_shape, scratch_shapes=scratch,
            compiler_params=_params("parallel", "arbitrary"), name="norm_matmul",
        )(x, gain, w)
    head_gain, cos_t, sin_t = rope
    seq_blocks = cos_t.shape[0] // tm
    tab_spec = pl.BlockSpec((tm, LANES), lambda i, j: (i % seq_blocks, 0))
    hg_spec = pl.BlockSpec((1, tn), lambda i, j: (0, j))
    return pl.pallas_call(
        _norm_matmul_rope_kernel, grid=grid,
        in_specs=[x_spec, g_spec, w_spec, hg_spec, tab_spec, tab_spec],
        out_specs=o_spec, out_shape=out_shape,
        scratch_shapes=scratch + [pltpu.VMEM((tm, MXU_COLS), F32)] * 3,
        compiler_params=_params("parallel", "arbitrary"), name="norm_matmul_rope",
    )(x, gain, w, head_gain, cos_t, sin_t)


def _matmul_residual_kernel(a_ref, w_ref, x_ref, o_ref):
    o_ref[...] = x_ref[...] + jnp.dot(a_ref[...], w_ref[...], preferred_element_type=F32)


def _matmul_residual(a, w, layer, x, *, tm=1024):
    m, k = a.shape
    n = w.shape[-1]
    return pl.pallas_call(
        _matmul_residual_kernel, grid=(m // tm,),
        in_specs=[pl.BlockSpec((tm, k), lambda i: (i, 0)),
                  pl.BlockSpec((None, k, n), lambda i: (layer, 0, 0),
                               pipeline_mode=pl.Buffered(1)),
                  pl.BlockSpec((tm, n), lambda i: (i, 0))],
        out_specs=pl.BlockSpec((tm, n), lambda i: (i, 0)),
        out_shape=jax.ShapeDtypeStruct((m, n), F32),
        compiler_params=_params("parallel"), name="matmul_residual",
    )(a, w, x)


def _ffn_kernel(x_ref, g_ref, wg_ref, wu_ref, wo_ref, o_ref, xn_ref):
    j = pl.program_id(1)

    @pl.when(j == 0)
    def _():
        x = x_ref[...]
        xn_ref[...] = _rms_rows(x, g_ref[...]).astype(BF16)
        o_ref[...] = x

    xn = xn_ref[...]
    acts = []
    th = wg_ref.shape[1]
    for c in range(pl.cdiv(th, MXU_COLS)):
        cs = slice(c * MXU_COLS, min((c + 1) * MXU_COLS, th))
        gate = jnp.dot(xn, wg_ref[:, cs], preferred_element_type=F32)
        up = jnp.dot(xn, wu_ref[:, cs], preferred_element_type=F32)
        acts.append((gate / (1.0 + jnp.exp(-gate)) * up).astype(BF16))
    act = jnp.concatenate(acts, axis=1)
    o_ref[...] += jnp.dot(act, wo_ref[...], preferred_element_type=F32)


def _ffn(x, gain, w_in, w_out, layer, *, tm=1024, th=512):
    m, d = x.shape
    hidden = w_out.shape[1]
    n_h = hidden // th
    return pl.pallas_call(
        _ffn_kernel, grid=(m // tm, n_h),
        in_specs=[pl.BlockSpec((tm, d), lambda i, j: (i, 0)),
                  pl.BlockSpec((1, d), lambda i, j: (0, 0)),
                  pl.BlockSpec((None, d, th), lambda i, j: (layer, 0, j)),
                  pl.BlockSpec((None, d, th), lambda i, j: (layer, 0, n_h + j)),
                  pl.BlockSpec((None, th, d), lambda i, j: (layer, j, 0))],
        out_specs=pl.BlockSpec((tm, d), lambda i, j: (i, 0)),
        out_shape=jax.ShapeDtypeStruct((m, d), F32),
        scratch_shapes=[pltpu.VMEM((tm, d), BF16)],
        compiler_params=_params("parallel", "arbitrary"), name="ffn",
    )(x, gain, w_in, w_in, w_out)


def _split3_bf16(x):
    hi = x.astype(BF16)
    r1 = x - hi.astype(F32)
    mid = r1.astype(BF16)
    lo = (r1 - mid.astype(F32)).astype(BF16)
    return hi, mid, lo


def _hgrn_kernel(q_ref, z_ref, i_ref, g_ref, lb_ref, og_ref, o_ref, st_ref, c_scr, b_scr, *,
                 layer, n_chunks):
    C = A_CHUNK
    SUB = 8
    NB = C // SUB

    @pl.when(pl.program_id(2) == 0)
    def _():
        st_ref[...] = jnp.zeros_like(st_ref)

    a = lb_ref[...]
    e = jnp.exp(a - jnp.max(a, axis=0, keepdims=True))
    sm = e / jnp.sum(e, axis=0, keepdims=True)
    lb = jnp.sum(sm[:layer + 1], axis=0, keepdims=True) - sm[0:1]
    lb = jnp.clip(lb, 0.0, 1.0 - 1e-6)
    log_lb = jnp.log(lb)
    one_m_lb = 1.0 - lb
    log_one_m_lb = jnp.log(one_m_lb)
    o_gain = og_ref[...]

    tri = (lax.broadcasted_iota(jnp.int32, (C, C), 0)
           >= lax.broadcasted_iota(jnp.int32, (C, C), 1)).astype(BF16)
    ti = lax.broadcasted_iota(jnp.int32, (C, C), 0)
    si = lax.broadcasted_iota(jnp.int32, (C, C), 1)
    same_32 = (ti // 32) == (si // 32)
    same_16 = (ti // 16) == (si // 16)
    s_sub = lax.broadcasted_iota(jnp.int32, (SUB, C), 1)
    zero_blk = jnp.zeros((SUB, LANES), F32)

    def blocks(x):
        return [x[j * SUB:(j + 1) * SUB] for j in range(NB)]

    def padded(parts):
        return jnp.concatenate([zero_blk if p is None else p for p in parts],
                               axis=0).astype(BF16)

    def bcast_row(ref, r):
        return jnp.broadcast_to(ref[r:r + 1, :], (SUB, LANES))

    chunks = [slice(c * C, (c + 1) * C) for c in range(n_chunks)]

    ks_all, bs_all = [], []
    for rows in chunks:
        z = z_ref[rows, :]
        ez = jnp.exp(-jnp.abs(z))
        log_sig = jnp.minimum(z, 0.0) - jnp.log(1.0 + ez)
        k = one_m_lb * (jnp.where(z >= 0.0, ez, 1.0) / (1.0 + ez))
        y = log_one_m_lb + log_sig
        hi_ = jnp.maximum(log_lb, y)
        lo_ = jnp.minimum(log_lb, y)
        log_f = hi_ + jnp.log(1.0 + jnp.exp(lo_ - hi_))
        b = sum(jnp.dot(tri, part, preferred_element_type=F32)
                for part in _split3_bf16(log_f * LOG2_E))
        b_scr[rows, :] = b
        c_scr[rows, :] = b - (log_one_m_lb + log_sig - z) * LOG2_E
        ks_all.append(blocks(k))
        bs_all.append(blocks(b))

    levels_all, upd_all, qdec_all = [], [], []
    for rows, ks, bs in zip(chunks, ks_all, bs_all):
        qs = blocks(q_ref[rows, :])
        ends = [bcast_row(b_scr, rows.start + j * SUB + SUB - 1) for j in range(NB)]
        a_levels = []
        for size in (4, 2, 1):
            qp, kp = [None] * NB, [None] * NB
            for pair in range(NB // (2 * size)):
                lo = pair * 2 * size
                r = ends[lo + size - 1]
                for j in range(lo, lo + size):
                    kp[j] = ks[j] * jnp.exp2(r - bs[j])
                for j in range(lo + size, lo + 2 * size):
                    qp[j] = qs[j] * jnp.exp2(bs[j] - r)
            a_levels.append(lax.dot_general(padded(qp), padded(kp), NT_DIMS,
                                            preferred_element_type=F32))
        levels_all.append(a_levels)
        k_dec = jnp.concatenate([ks[j] * jnp.exp2(ends[NB - 1] - bs[j]) for j in range(NB)],
                                axis=0)
        upd_all.append(lax.dot_general(i_ref[rows, :].astype(BF16), k_dec.astype(BF16), TN_DIMS,
                                       preferred_element_type=F32))
        qdec_all.append(jnp.concatenate([qs[j] * jnp.exp2(bs[j]) for j in range(NB)],
                                        axis=0).astype(BF16))

    a_all = []
    for rows, bs, a_levels in zip(chunks, bs_all, levels_all):
        qs = blocks(q_ref[rows, :])
        diag = [jnp.zeros((SUB, C), F32)] * NB
        for s in range(SUB):
            for j in range(NB):
                r = j * SUB + s
                w = jnp.exp2(bs[j] - bcast_row(c_scr, rows.start + r))
                a_s = jnp.sum(qs[j] * w, axis=-1, keepdims=True)
                diag[j] = jnp.where(s_sub == r, a_s, diag[j])
        a = (a_levels[0] + jnp.where(same_32, a_levels[1], 0.0)
             + jnp.where(same_16, a_levels[2], 0.0)
             + jnp.where(ti >= si, jnp.concatenate(diag, axis=0), 0.0))
        a_all.append(a.astype(BF16))

    st = st_ref[...]
    for rows, a, upd, q_dec in zip(chunks, a_all, upd_all, qdec_all):
        o = (lax.dot_general(q_dec, st.astype(BF16), NT_DIMS, preferred_element_type=F32)
             + jnp.dot(a, i_ref[rows, :].astype(BF16), preferred_element_type=F32))
        st = st * jnp.exp2(b_scr[rows.stop - 1:rows.stop, :]) + upd
        g = g_ref[rows, :]
        out = _rms_rows(o, o_gain) * (g / (1.0 + jnp.exp(-g)))
        o_ref[rows, :] = out.astype(o_ref.dtype)

    st_ref[...] = st


def _hgrn(proj, a_lower_bounds, o_gain, layer, batch, seq, *, rows=1024):
    n_r = seq // rows
    h_ = A_HEADS

    def sec_spec(sec):
        return pl.BlockSpec((rows, A_DK), lambda b, h, r: (b * n_r + r, sec * h_ + h))

    kern = functools.partial(_hgrn_kernel, layer=layer, n_chunks=rows // A_CHUNK)
    return pl.pallas_call(
        kern, grid=(batch, h_, n_r),
        in_specs=[sec_spec(0), sec_spec(1), sec_spec(2), sec_spec(3),
                  pl.BlockSpec((N_A_LAYERS, A_DK), lambda b, h, r: (0, h)),
                  pl.BlockSpec((1, A_DK), lambda b, h, r: (0, 0))],
        out_specs=pl.BlockSpec((rows, A_DK), lambda b, h, r: (b * n_r + r, h)),
        out_shape=jax.ShapeDtypeStruct((batch * seq, D_MODEL), BF16),
        scratch_shapes=[pltpu.VMEM((A_DK, A_DK), F32), pltpu.VMEM((rows, A_DK), F32),
                        pltpu.VMEM((rows, A_DK), F32)],
        compiler_params=_params("parallel", "parallel", "arbitrary"), name="hgrn",
    )(proj, proj, proj, proj, a_lower_bounds, o_gain)


def _attn_kernel(q_ref, kc_ref, vc_ref, kp_ref, vp_ref, o_ref, l_ref, bias_ref, *,
                 planes, tile_rows, key_rows, den, block_rows):
    QB, P, R, KW = B_QBLOCK, planes, tile_rows, key_rows
    n_keys = P * (KW + R)
    n_edge = KW // R
    kc = lax.broadcasted_iota(jnp.int32, (n_keys, QB), 0)
    qa = lax.broadcasted_iota(jnp.int32, (n_keys, QB), 1)
    key_row = kc % (KW + R) - KW
    num = P * (qa % R - key_row) + (qa // R - kc // (KW + R))
    band = (num >= 0) & (num <= QB * den) & (num % den == 0)
    block_row0 = pl.program_id(2) * block_rows
    bias_ref[n_edge] = jnp.where(band, 0.0, MASK_VALUE)
    for t in range(n_edge):
        ok = band & (key_row >= -(block_row0 + t * R))
        bias_ref[t] = jnp.where(ok, 0.0, MASK_VALUE)

    def gather(ref, rows, cols):
        return jnp.concatenate([ref[0, m, rows, cols] for m in range(P)], axis=0)

    for t in range(block_rows // R):
        rows = slice(t * R, (t + 1) * R)
        first = t * R - KW

        def window(cur_ref, prev_ref, cols):
            if first >= 0:
                return gather(cur_ref, slice(first, (t + 1) * R), cols)
            parts = []
            for m in range(P):
                parts += [prev_ref[0, m, KW + first:KW, cols], cur_ref[0, m, 0:(t + 1) * R, cols]]
            return jnp.concatenate(parts, axis=0)

        def scores(h):
            cols = slice(h * B_HD, (h + 1) * B_HD)
            return lax.dot_general(window(kc_ref, kp_ref, cols), gather(q_ref, rows, cols),
                                   NT_DIMS, preferred_element_type=F32)

        lse_rows = []
        s_next = scores(0)
        for h in range(B_HEADS):
            cols = slice(h * B_HD, (h + 1) * B_HD)
            vals = window(vc_ref, vp_ref, cols)
            s = s_next + bias_ref[min(t, n_edge)]
            if h + 1 < B_HEADS:
                s_next = scores(h + 1)
            m = jnp.max(s, axis=0, keepdims=True)
            p = jnp.exp2(s - m)
            l = jnp.sum(p, axis=0, keepdims=True)
            o = lax.dot_general((p * (1.0 / l)).astype(BF16), vals, TN_DIMS,
                                preferred_element_type=F32).astype(o_ref.dtype)
            for m_ in range(P):
                o_ref[0, m_, rows, cols] = o[m_ * R:(m_ + 1) * R]
            lse_rows.append((m + jnp.log2(l)) * (1.0 / LOG2_E))
        pad = jnp.zeros((LANES - B_HEADS, QB), F32)
        lse_tile = jnp.concatenate(lse_rows + [pad], axis=0).T
        for m_ in range(P):
            l_ref[0, m_, rows, :] = lse_tile[m_ * R:(m_ + 1) * R]


def _attn_group(q, k, v, group):
    batch, n_planes, plane_rows, _ = q.shape
    window, dil = B_CONFIGS[group]
    assert window // dil == B_QBLOCK
    planes = max(n_planes // dil, 1)
    den = max(dil // n_planes, 1)
    n_sets = n_planes // planes
    tile_rows = B_QBLOCK // planes
    key_rows = tile_rows * den
    block_rows = plane_rows // planes
    D = D_MODEL

    def view(a):
        return a.reshape(batch, planes, n_sets, plane_rows, a.shape[-1])

    def prev_block(n):
        return jnp.maximum(n * (block_rows // key_rows) - 1, 0)

    cur = pl.BlockSpec((1, planes, None, block_rows, D), lambda b, s, n: (b, 0, s, n, group))
    before = pl.BlockSpec((1, planes, None, key_rows, D),
                          lambda b, s, n: (b, 0, s, prev_block(n), group))
    o_spec = pl.BlockSpec((1, planes, None, block_rows, D), lambda b, s, n: (b, 0, s, n, 0))
    l_spec = pl.BlockSpec((1, planes, None, block_rows, LANES), lambda b, s, n: (b, 0, s, n, 0))
    kern = functools.partial(
        _attn_kernel, planes=planes, tile_rows=tile_rows, key_rows=key_rows, den=den,
        block_rows=block_rows)
    o, lse = pl.pallas_call(
        kern, grid=(batch, n_sets, plane_rows // block_rows),
        in_specs=[cur, cur, cur, before, before], out_specs=[o_spec, l_spec],
        out_shape=[jax.ShapeDtypeStruct((batch, planes, n_sets, plane_rows, D), BF16),
                   jax.ShapeDtypeStruct((batch, planes, n_sets, plane_rows, LANES), F32)],
        scratch_shapes=[pltpu.VMEM((key_rows // tile_rows + 1,
                                    planes * (key_rows + tile_rows), B_QBLOCK), F32)],
        compiler_params=_params("parallel", "parallel", "arbitrary"),
        name=f"attn_g{group}",
    )(view(q), view(k), view(v), view(k), view(v))
    rows = batch * n_planes * plane_rows
    return o.reshape(rows, D), lse.reshape(rows, LANES)


def _mix_matmul_residual_kernel(*refs):
    o_refs, l_refs = refs[:N_GROUPS], refs[N_GROUPS:2 * N_GROUPS]
    w_ref, x_ref, out_ref = refs[2 * N_GROUPS:]
    lses = [l[...] for l in l_refs]
    top = functools.reduce(jnp.maximum, lses)
    es = [jnp.exp(l - top) for l in lses]
    inv = 1.0 / functools.reduce(jnp.add, es)
    spread = (lax.broadcasted_iota(jnp.int32, (LANES, D_MODEL), 1) // B_HD
              == lax.broadcasted_iota(jnp.int32, (LANES, D_MODEL), 0)).astype(BF16)
    mixed = None
    for o_ref, e in zip(o_refs, es):
        wts = jnp.dot((e * inv).astype(BF16), spread, preferred_element_type=F32)
        term = wts.astype(BF16) * o_ref[...]
        mixed = term if mixed is None else mixed + term
    out_ref[...] = x_ref[...] + jnp.dot(mixed, w_ref[...], preferred_element_type=F32)


def _mix_matmul_residual(outs, lses, w, layer, x, *, tm=512):
    m, d = x.shape
    row_spec = pl.BlockSpec((tm, d), lambda i: (i, 0))
    return pl.pallas_call(
        _mix_matmul_residual_kernel, grid=(m // tm,),
        in_specs=[row_spec] * N_GROUPS + [pl.BlockSpec((tm, LANES), lambda i: (i, 0))] * N_GROUPS
        + [pl.BlockSpec((None, d, d), lambda i: (layer, 0, 0), pipeline_mode=pl.Buffered(1)),
           row_spec],
        out_specs=row_spec, out_shape=jax.ShapeDtypeStruct((m, d), F32),
        compiler_params=_params("parallel"), name="mix_matmul_residual",
    )(*outs, *lses, w, x)


def _to_planes(a, batch):
    rows, c = a.shape
    return (a.reshape(batch, rows // batch // SEQ_PLANES, SEQ_PLANES, c)
            .transpose(0, 2, 1, 3).reshape(rows, c))


def _from_planes(a, batch):
    rows, c = a.shape
    return (a.reshape(batch, SEQ_PLANES, rows // batch // SEQ_PLANES, c)
            .transpose(0, 2, 1, 3).reshape(rows, c))


def _rope_dim_order(a):
    h, half = ROPE_HALF, LANES // 2
    return jnp.concatenate([a[..., :h], a[..., half:half + h], a[..., 2 * h:half],
                            a[..., h:2 * h], a[..., half + h:]], axis=-1)


def _rope_tables(seq):
    inv_freq = ROPE_THETA ** (-jnp.arange(ROPE_HALF, dtype=F32) * 2.0 / ROPE_DIM)
    ang = jnp.arange(seq).astype(F32)[:, None] * inv_freq[None, :]
    cos, sin = jnp.cos(ang), jnp.sin(ang)
    gap = LANES // 2 - ROPE_HALF
    cos_t = jnp.concatenate([cos, jnp.ones((seq, gap), F32), cos, jnp.ones((seq, gap), F32)], axis=1)
    sin_t = jnp.concatenate([sin, jnp.zeros((seq, gap), F32), -sin, jnp.zeros((seq, gap), F32)],
                            axis=1)
    return _to_planes(cos_t, 1), _to_planes(sin_t, 1)


def _head_weight_kernel(w_ref, o_ref):
    h, half = ROPE_HALF, LANES // 2
    src = lax.broadcasted_iota(jnp.int32, (MXU_COLS, MXU_COLS), 0)
    dst = lax.broadcasted_iota(jnp.int32, (MXU_COLS, MXU_COLS), 1)
    d = dst % LANES
    from_lane = jnp.where((d >= h) & (d < 2 * h), d + (half - h),
                          jnp.where((d >= half) & (d < half + h), d - (half - h), d))
    perm = (src == (dst - d) + from_lane).astype(BF16)
    for c in range(w_ref.shape[1] // MXU_COLS):
        cols = slice(c * MXU_COLS, (c + 1) * MXU_COLS)
        o_ref[:, cols] = jnp.dot(w_ref[:, cols].astype(BF16), perm,
                                 preferred_element_type=F32).astype(BF16)


def _head_weight_cols(w, n_cols, *, tr=1024, tn=2048):
    layers, d, _ = w.shape
    return pl.pallas_call(
        _head_weight_kernel, grid=(layers, d // tr, n_cols // tn),
        in_specs=[pl.BlockSpec((None, tr, tn), lambda l, i, j: (l, i, j))],
        out_specs=pl.BlockSpec((None, tr, tn), lambda l, i, j: (l, i, j)),
        out_shape=jax.ShapeDtypeStruct((layers, d, n_cols), BF16),
        compiler_params=_params("parallel", "parallel", "parallel"), name="head_weight_cols",
    )(w)


def _head_gain_cols(gain, scale=1.0):
    g = _rope_dim_order(gain * scale)
    return jnp.broadcast_to(g[:, None, :], (N_GROUPS, B_HEADS, B_HD)).reshape(1, -1)


def kernel(x, a_norm, a_w_in, a_o_gain, a_w_out, a_lower_bounds, kv_norm, w_kv, k_gain,
           b_norm, b_w_q, b_q_gain, b_w_out, ffn_norm, ffn_w_in, ffn_w_out):
    batch, seq, d = x.shape
    m = batch * seq
    xf = x.reshape(m, d)

    a_w_in_b, a_w_out_b = a_w_in.astype(BF16), a_w_out.astype(BF16)
    gd = N_GROUPS * D_MODEL
    w_k_b = _head_weight_cols(w_kv[None], gd)
    w_v_b = w_kv[:, gd:].astype(BF16)[None]
    b_w_q_b, b_w_out_b = _head_weight_cols(b_w_q, gd), b_w_out.astype(BF16)
    ffn_w_in_b, ffn_w_out_b = ffn_w_in.astype(BF16), ffn_w_out.astype(BF16)

    rope_tabs = _rope_tables(seq)
    plane_shape = (batch, SEQ_PLANES, seq // SEQ_PLANES, gd)
    k = v = None
    for layer in range(DEPTH):
        if layer < N_A_LAYERS:
            proj = _norm_matmul(xf, a_norm[layer][None], a_w_in_b, layer, F32, tn=2048)
            og = _hgrn(proj, a_lower_bounds, a_o_gain[layer][None], layer, batch, seq)
            xf = _matmul_residual(og, a_w_out_b, layer, xf)
        else:
            if layer == N_A_LAYERS:
                xf = _to_planes(xf, batch)
                k = _norm_matmul(xf, kv_norm[None], w_k_b, 0, BF16,
                                 rope=(_head_gain_cols(k_gain),) + rope_tabs)
                v = _norm_matmul(xf, kv_norm[None], w_v_b, 0, BF16, tn=2048)
                k, v = k.reshape(plane_shape), v.reshape(plane_shape)
            jb = layer - N_A_LAYERS
            q = _norm_matmul(xf, b_norm[jb][None], b_w_q_b, jb, BF16,
                             rope=(_head_gain_cols(b_q_gain[jb], B_HD ** -0.5 * LOG2_E),)
                             + rope_tabs)
            q = q.reshape(plane_shape)
            outs, lses = zip(*[_attn_group(q, k, v, group) for group in range(N_GROUPS)])
            xf = _mix_matmul_residual(outs, lses, b_w_out_b, jb, xf)
        xf = _ffn(xf, ffn_norm[layer][None], ffn_w_in_b, ffn_w_out_b, layer)
    return _from_planes(xf, batch).reshape(batch, seq, d)
```

```python
import functools

import jax
import jax.numpy as jnp
from jax import lax
from jax.experimental import pallas as pl
from jax.experimental.pallas import tpu as pltpu

F32 = jnp.float32
BF16 = jnp.bfloat16

D_MODEL = 2048
DEPTH = 4
N_A_LAYERS = DEPTH // 2
A_HEADS = 16
A_DK = D_MODEL // A_HEADS
A_CHUNK = 64
B_HEADS = 16
B_HD = D_MODEL // B_HEADS
B_CONFIGS = ((128, 1), (512, 4), (2048, 16))
N_GROUPS = len(B_CONFIGS)
B_QBLOCK = 128
ROPE_THETA = 500000.0
ROPE_DIM = B_HD // 4
ROPE_HALF = ROPE_DIM // 2
FFN_HIDDEN = ((8 * D_MODEL // 3 + 255) // 256) * 256
EPS = 1e-6

LANES = 128
MXU_COLS = 256
SEQ_PLANES = 8
VMEM_LIMIT_BYTES = 56 * 1024 * 1024
MASK_VALUE = -1e30
LOG2_E = 1.4426950408889634

NT_DIMS = (((1,), (1,)), ((), ()))
TN_DIMS = (((0,), (0,)), ((), ()))


def _params(*semantics):
    return pltpu.CompilerParams(dimension_semantics=semantics,
                                vmem_limit_bytes=VMEM_LIMIT_BYTES)


def _rms_rows(x, gain):
    ms = jnp.mean(x * x, axis=-1, keepdims=True)
    return x * lax.rsqrt(ms + EPS) * gain


def _norm_matmul_kernel(x_ref, g_ref, w_ref, o_ref, xn_ref):
    @pl.when(pl.program_id(1) == 0)
    def _():
        xn_ref[...] = _rms_rows(x_ref[...], g_ref[...]).astype(BF16)

    o_ref[...] = jnp.dot(xn_ref[...], w_ref[...],
                         preferred_element_type=F32).astype(o_ref.dtype)


def _norm_matmul_rope_kernel(x_ref, g_ref, w_ref, hg_ref, cos_ref, sin_ref, o_ref, xn_ref,
                             acc_ref, scale_ref, rot_ref):
    @pl.when(pl.program_id(1) == 0)
    def _():
        xn_ref[...] = _rms_rows(x_ref[...], g_ref[...]).astype(BF16)

    xn = xn_ref[...]
    tm = xn.shape[0]
    for c in range(o_ref.shape[1] // MXU_COLS):
        acc_ref[...] = jnp.dot(xn, w_ref[:, c * MXU_COLS:(c + 1) * MXU_COLS],
                               preferred_element_type=F32)
        for h in range(MXU_COLS // LANES):
            hs = slice(h * LANES, (h + 1) * LANES)
            cols = slice(c * MXU_COLS + h * LANES, c * MXU_COLS + (h + 1) * LANES)
            y = acc_ref[:, hs]
            ms = jnp.mean(y * y, axis=-1, keepdims=True)
            scale_ref[:, hs] = jnp.broadcast_to(lax.rsqrt(ms + EPS), (tm, LANES)) * hg_ref[:, cols]
        for h in range(MXU_COLS // LANES):
            hs = slice(h * LANES, (h + 1) * LANES)
            y = acc_ref[:, hs] * scale_ref[:, hs]
            rot_ref[:, hs] = pltpu.roll(y * sin_ref[...], LANES // 2, 1)
        for h in range(MXU_COLS // LANES):
            hs = slice(h * LANES, (h + 1) * LANES)
            cols = slice(c * MXU_COLS + h * LANES, c * MXU_COLS + (h + 1) * LANES)
            y = acc_ref[:, hs] * scale_ref[:, hs]
            o_ref[:, cols] = (y * cos_ref[...] + rot_ref[:, hs]).astype(o_ref.dtype)


def _norm_matmul(x, gain, w, layer, out_dtype, *, tm=1024, tn=1024, rope=None):
    m, d = x.shape
    n = w.shape[-1]
    if rope is not None:
        tm, tn = 512, n
    grid = (m // tm, n // tn)
    x_spec = pl.BlockSpec((tm, d), lambda i, j: (i, 0))
    g_spec = pl.BlockSpec((1, d), lambda i, j: (0, 0))
    w_mode = dict(pipeline_mode=pl.Buffered(1)) if tn == n else {}
    w_spec = pl.BlockSpec((None, d, tn), lambda i, j: (layer, 0, j), **w_mode)
    o_spec = pl.BlockSpec((tm, tn), lambda i, j: (i, j))
    scratch = [pltpu.VMEM((tm, d), BF16)]
    out_shape = jax.ShapeDtypeStruct((m, n), out_dtype)
    if rope is None:
        return pl.pallas_call(
            _norm_matmul_kernel, grid=grid, in_specs=[x_spec, g_spec, w_spec],
            out_specs=o_spec, out_shape=out_shape, scratch_shapes=scratch,
            compiler_params=_params("parallel", "arbitrary"), name="norm_matmul",
        )(x, gain, w)
    head_gain, cos_t, sin_t = rope
    seq_blocks = cos_t.shape[0] // tm
    tab_spec = pl.BlockSpec((tm, LANES), lambda i, j: (i % seq_blocks, 0))
    hg_spec = pl.BlockSpec((1, tn), lambda i, j: (0, j))
    return pl.pallas_call(
        _norm_matmul_rope_kernel, grid=grid,
        in_specs=[x_spec, g_spec, w_spec, hg_spec, tab_spec, tab_spec],
        out_specs=o_spec, out_shape=out_shape,
        scratch_shapes=scratch + [pltpu.VMEM((tm, MXU_COLS), F32)] * 3,
        compiler_params=_params("parallel", "arbitrary"), name="norm_matmul_rope",
    )(x, gain, w, head_gain, cos_t, sin_t)


def _matmul_residual_kernel(a_ref, w_ref, x_ref, o_ref):
    o_ref[...] = x_ref[...] + jnp.dot(a_ref[...], w_ref[...], preferred_element_type=F32)


def _matmul_residual(a, w, layer, x, *, tm=1024):
    m, k = a.shape
    n = w.shape[-1]
    return pl.pallas_call(
        _matmul_residual_kernel, grid=(m // tm,),
        in_specs=[pl.BlockSpec((tm, k), lambda i: (i, 0)),
                  pl.BlockSpec((None, k, n), lambda i: (layer, 0, 0),
                               pipeline_mode=pl.Buffered(1)),
                  pl.BlockSpec((tm, n), lambda i: (i, 0))],
        out_specs=pl.BlockSpec((tm, n), lambda i: (i, 0)),
        out_shape=jax.ShapeDtypeStruct((m, n), F32),
        compiler_params=_params("parallel"), name="matmul_residual",
    )(a, w, x)


def _ffn_kernel(x_ref, g_ref, wg_ref, wu_ref, wo_ref, o_ref, xn_ref):
    j = pl.program_id(1)

    @pl.when(j == 0)
    def _():
        x = x_ref[...]
        xn_ref[...] = _rms_rows(x, g_ref[...]).astype(BF16)
        o_ref[...] = x

    xn = xn_ref[...]
    acts = []
    th = wg_ref.shape[1]
    for c in range(pl.cdiv(th, MXU_COLS)):
        cs = slice(c * MXU_COLS, min((c + 1) * MXU_COLS, th))
        gate = jnp.dot(xn, wg_ref[:, cs], preferred_element_type=F32)
        up = jnp.dot(xn, wu_ref[:, cs], preferred_element_type=F32)
        acts.append((gate / (1.0 + jnp.exp(-gate)) * up).astype(BF16))
    act = jnp.concatenate(acts, axis=1)
    o_ref[...] += jnp.dot(act, wo_ref[...], preferred_element_type=F32)


def _ffn(x, gain, w_in, w_out, layer, *, tm=1024, th=512):
    m, d = x.shape
    hidden = w_out.shape[1]
    n_h = hidden // th
    return pl.pallas_call(
        _ffn_kernel, grid=(m // tm, n_h),
        in_specs=[pl.BlockSpec((tm, d), lambda i, j: (i, 0)),
                  pl.BlockSpec((1, d), lambda i, j: (0, 0)),
                  pl.BlockSpec((None, d, th), lambda i, j: (layer, 0, j)),
                  pl.BlockSpec((None, d, th), lambda i, j: (layer, 0, n_h + j)),
                  pl.BlockSpec((None, th, d), lambda i, j: (layer, j, 0))],
        out_specs=pl.BlockSpec((tm, d), lambda i, j: (i, 0)),
        out_shape=jax.ShapeDtypeStruct((m, d), F32),
        scratch_shapes=[pltpu.VMEM((tm, d), BF16)],
        compiler_params=_params("parallel", "arbitrary"), name="ffn",
    )(x, gain, w_in, w_in, w_out)


def _split3_bf16(x):
    hi = x.astype(BF16)
    r1 = x - hi.astype(F32)
    mid = r1.astype(BF16)
    lo = (r1 - mid.astype(F32)).astype(BF16)
    return hi, mid, lo


def _hgrn_kernel(q_ref, z_ref, i_ref, g_ref, lb_ref, og_ref, o_ref, st_ref, c_scr, b_scr, *,
                 layer, n_chunks):
    C = A_CHUNK
    SUB = 8
    NB = C // SUB

    @pl.when(pl.program_id(2) == 0)
    def _():
        st_ref[...] = jnp.zeros_like(st_ref)

    a = lb_ref[...]
    e = jnp.exp(a - jnp.max(a, axis=0, keepdims=True))
    sm = e / jnp.sum(e, axis=0, keepdims=True)
    lb = jnp.sum(sm[:layer + 1], axis=0, keepdims=True) - sm[0:1]
    lb = jnp.clip(lb, 0.0, 1.0 - 1e-6)
    log_lb = jnp.log(lb)
    one_m_lb = 1.0 - lb
    log_one_m_lb = jnp.log(one_m_lb)
    o_gain = og_ref[...]

    tri = (lax.broadcasted_iota(jnp.int32, (C, C), 0)
           >= lax.broadcasted_iota(jnp.int32, (C, C), 1)).astype(BF16)
    ti = lax.broadcasted_iota(jnp.int32, (C, C), 0)
    si = lax.broadcasted_iota(jnp.int32, (C, C), 1)
    same_32 = (ti // 32) == (si // 32)
    same_16 = (ti // 16) == (si // 16)
    s_sub = lax.broadcasted_iota(jnp.int32, (SUB, C), 1)
    zero_blk = jnp.zeros((SUB, LANES), F32)

    def blocks(x):
        return [x[j * SUB:(j + 1) * SUB] for j in range(NB)]

    def padded(parts):
        return jnp.concatenate([zero_blk if p is None else p for p in parts],
                               axis=0).astype(BF16)

    def bcast_row(ref, r):
        return jnp.broadcast_to(ref[r:r + 1, :], (SUB, LANES))

    chunks = [slice(c * C, (c + 1) * C) for c in range(n_chunks)]

    ks_all, bs_all = [], []
    for rows in chunks:
        z = z_ref[rows, :]
        ez = jnp.exp(-jnp.abs(z))
        log_sig = jnp.minimum(z, 0.0) - jnp.log(1.0 + ez)
        k = one_m_lb * (jnp.where(z >= 0.0, ez, 1.0) / (1.0 + ez))
        y = log_one_m_lb + log_sig
        hi_ = jnp.maximum(log_lb, y)
        lo_ = jnp.minimum(log_lb, y)
        log_f = hi_ + jnp.log(1.0 + jnp.exp(lo_ - hi_))
        b = sum(jnp.dot(tri, part, preferred_element_type=F32)
                for part in _split3_bf16(log_f * LOG2_E))
        b_scr[rows, :] = b
        c_scr[rows, :] = b - (log_one_m_lb + log_sig - z) * LOG2_E
        ks_all.append(blocks(k))
        bs_all.append(blocks(b))

    levels_all, upd_all, qdec_all = [], [], []
    for rows, ks, bs in zip(chunks, ks_all, bs_all):
        qs = blocks(q_ref[rows, :])
        ends = [bcast_row(b_scr, rows.start + j * SUB + SUB - 1) for j in range(NB)]
        a_levels = []
        for size in (4, 2, 1):
            qp, kp = [None] * NB, [None] * NB
            for pair in range(NB // (2 * size)):
                lo = pair * 2 * size
                r = ends[lo + size - 1]
                for j in range(lo, lo + size):
                    kp[j] = ks[j] * jnp.exp2(r - bs[j])
                for j in range(lo + size, lo + 2 * size):
                    qp[j] = qs[j] * jnp.exp2(bs[j] - r)
            a_levels.append(lax.dot_general(padded(qp), padded(kp), NT_DIMS,
                                            preferred_element_type=F32))
        levels_all.append(a_levels)
        k_dec = jnp.concatenate([ks[j] * jnp.exp2(ends[NB - 1] - bs[j]) for j in range(NB)],
                                axis=0)
        upd_all.append(lax.dot_general(i_ref[rows, :].astype(BF16), k_dec.astype(BF16), TN_DIMS,
                                       preferred_element_type=F32))
        qdec_all.append(jnp.concatenate([qs[j] * jnp.exp2(bs[j]) for j in range(NB)],
                                        axis=0).astype(BF16))

    a_all = []
    for rows, bs, a_levels in zip(chunks, bs_all, levels_all):
        qs = blocks(q_ref[rows, :])
        diag = [jnp.zeros((SUB, C), F32)] * NB
        for s in range(SUB):
            for j in range(NB):
                r = j * SUB + s
                w = jnp.exp2(bs[j] - bcast_row(c_scr, rows.start + r))
                a_s = jnp.sum(qs[j] * w, axis=-1, keepdims=True)
                diag[j] = jnp.where(s_sub == r, a_s, diag[j])
        a = (a_levels[0] + jnp.where(same_32, a_levels[1], 0.0)
             + jnp.where(same_16, a_levels[2], 0.0)
             + jnp.where(ti >= si, jnp.concatenate(diag, axis=0), 0.0))
        a_all.append(a.astype(BF16))

    st = st_ref[...]
    for rows, a, upd, q_dec in zip(chunks, a_all, upd_all, qdec_all):
        o = (lax.dot_general(q_dec, st.astype(BF16), NT_DIMS, preferred_element_type=F32)
             + jnp.dot(a, i_ref[rows, :].astype(BF16), preferred_element_type=F32))
        st = st * jnp.exp2(b_scr[rows.stop - 1:rows.stop, :]) + upd
        g = g_ref[rows, :]
        out = _rms_rows(o, o_gain) * (g / (1.0 + jnp.exp(-g)))
        o_ref[rows, :] = out.astype(o_ref.dtype)

    st_ref[...] = st


def _hgrn(proj, a_lower_bounds, o_gain, layer, batch, seq, *, rows=2048):
    n_r = seq // rows
    h_ = A_HEADS

    def sec_spec(sec):
        return pl.BlockSpec((rows, A_DK), lambda b, h, r: (b * n_r + r, sec * h_ + h))

    kern = functools.partial(_hgrn_kernel, layer=layer, n_chunks=rows // A_CHUNK)
    return pl.pallas_call(
        kern, grid=(batch, h_, n_r),
        in_specs=[sec_spec(0), sec_spec(1), sec_spec(2), sec_spec(3),
                  pl.BlockSpec((N_A_LAYERS, A_DK), lambda b, h, r: (0, h)),
                  pl.BlockSpec((1, A_DK), lambda b, h, r: (0, 0))],
        out_specs=pl.BlockSpec((rows, A_DK), lambda b, h, r: (b * n_r + r, h)),
        out_shape=jax.ShapeDtypeStruct((batch * seq, D_MODEL), BF16),
        scratch_shapes=[pltpu.VMEM((A_DK, A_DK), F32), pltpu.VMEM((rows, A_DK), F32),
                        pltpu.VMEM((rows, A_DK), F32)],
        compiler_params=_params("parallel", "parallel", "arbitrary"), name="hgrn",
    )(proj, proj, proj, proj, a_lower_bounds, o_gain)


def _attn_kernel(q_ref, kc_ref, vc_ref, kp_ref, vp_ref, o_ref, l_ref, bias_ref, *,
                 planes, tile_rows, key_rows, den, block_rows):
    QB, P, R, KW = B_QBLOCK, planes, tile_rows, key_rows
    n_keys = P * (KW + R)
    n_edge = KW // R
    kc = lax.broadcasted_iota(jnp.int32, (n_keys, QB), 0)
    qa = lax.broadcasted_iota(jnp.int32, (n_keys, QB), 1)
    key_row = kc % (KW + R) - KW
    num = P * (qa % R - key_row) + (qa // R - kc // (KW + R))
    band = (num >= 0) & (num <= QB * den) & (num % den == 0)
    block_row0 = pl.program_id(2) * block_rows
    bias_ref[n_edge] = jnp.where(band, 0.0, MASK_VALUE)
    for t in range(n_edge):
        ok = band & (key_row >= -(block_row0 + t * R))
        bias_ref[t] = jnp.where(ok, 0.0, MASK_VALUE)

    def gather(ref, rows, cols):
        return jnp.concatenate([ref[0, m, rows, cols] for m in range(P)], axis=0)

    for t in range(block_rows // R):
        rows = slice(t * R, (t + 1) * R)
        first = t * R - KW

        def window(cur_ref, prev_ref, cols):
            if first >= 0:
                return gather(cur_ref, slice(first, (t + 1) * R), cols)
            parts = []
            for m in range(P):
                parts += [prev_ref[0, m, KW + first:KW, cols], cur_ref[0, m, 0:(t + 1) * R, cols]]
            return jnp.concatenate(parts, axis=0)

        def scores(h):
            cols = slice(h * B_HD, (h + 1) * B_HD)
            return lax.dot_general(window(kc_ref, kp_ref, cols), gather(q_ref, rows, cols),
                                   NT_DIMS, preferred_element_type=F32)

        lse_rows = []
        s_next = scores(0)
        for h in range(B_HEADS):
            cols = slice(h * B_HD, (h + 1) * B_HD)
            vals = window(vc_ref, vp_ref, cols)
            s = s_next + bias_ref[min(t, n_edge)]
            if h + 1 < B_HEADS:
                s_next = scores(h + 1)
            m = jnp.max(s, axis=0, keepdims=True)
            p = jnp.exp2(s - m)
            l = jnp.sum(p, axis=0, keepdims=True)
            o = lax.dot_general((p * (1.0 / l)).astype(BF16), vals, TN_DIMS,
                                preferred_element_type=F32).astype(o_ref.dtype)
            for m_ in range(P):
                o_ref[0, m_, rows, cols] = o[m_ * R:(m_ + 1) * R]
            lse_rows.append((m + jnp.log2(l)) * (1.0 / LOG2_E))
        pad = jnp.zeros((LANES - B_HEADS, QB), F32)
        lse_tile = jnp.concatenate(lse_rows + [pad], axis=0).T
        for m_ in range(P):
            l_ref[0, m_, rows, :] = lse_tile[m_ * R:(m_ + 1) * R]


def _attn_group(q, k, v, group):
    batch, n_planes, plane_rows, _ = q.shape
    window, dil = B_CONFIGS[group]
    assert window // dil == B_QBLOCK
    planes = max(n_planes // dil, 1)
    den = max(dil // n_planes, 1)
    n_sets = n_planes // planes
    tile_rows = B_QBLOCK // planes
    key_rows = tile_rows * den
    block_rows = plane_rows // planes
    D = D_MODEL

    def view(a):
        return a.reshape(batch, planes, n_sets, plane_rows, a.shape[-1])

    def prev_block(n):
        return jnp.maximum(n * (block_rows // key_rows) - 1, 0)

    cur = pl.BlockSpec((1, planes, None, block_rows, D), lambda b, s, n: (b, 0, s, n, group))
    before = pl.BlockSpec((1, planes, None, key_rows, D),
                          lambda b, s, n: (b, 0, s, prev_block(n), group))
    o_spec = pl.BlockSpec((1, planes, None, block_rows, D), lambda b, s, n: (b, 0, s, n, 0))
    l_spec = pl.BlockSpec((1, planes, None, block_rows, LANES), lambda b, s, n: (b, 0, s, n, 0))
    kern = functools.partial(
        _attn_kernel, planes=planes, tile_rows=tile_rows, key_rows=key_rows, den=den,
        block_rows=block_rows)
    o, lse = pl.pallas_call(
        kern, grid=(batch, n_sets, plane_rows // block_rows),
        in_specs=[cur, cur, cur, before, before], out_specs=[o_spec, l_spec],
        out_shape=[jax.ShapeDtypeStruct((batch, planes, n_sets, plane_rows, D), BF16),
                   jax.ShapeDtypeStruct((batch, planes, n_sets, plane_rows, LANES), F32)],
        scratch_shapes=[pltpu.VMEM((key_rows // tile_rows + 1,
                                    planes * (key_rows + tile_rows), B_QBLOCK), F32)],
        compiler_params=_params("parallel", "parallel", "arbitrary"),
        name=f"attn_g{group}",
    )(view(q), view(k), view(v), view(k), view(v))
    rows = batch * n_planes * plane_rows
    return o.reshape(rows, D), lse.reshape(rows, LANES)


def _mix_matmul_residual_kernel(*refs):
    o_refs, l_refs = refs[:N_GROUPS], refs[N_GROUPS:2 * N_GROUPS]
    w_ref, x_ref, out_ref = refs[2 * N_GROUPS:]
    lses = [l[...] for l in l_refs]
    top = functools.reduce(jnp.maximum, lses)
    es = [jnp.exp(l - top) for l in lses]
    inv = 1.0 / functools.reduce(jnp.add, es)
    spread = (lax.broadcasted_iota(jnp.int32, (LANES, D_MODEL), 1) // B_HD
              == lax.broadcasted_iota(jnp.int32, (LANES, D_MODEL), 0)).astype(BF16)
    mixed = None
    for o_ref, e in zip(o_refs, es):
        wts = jnp.dot((e * inv).astype(BF16), spread, preferred_element_type=F32)
        term = wts.astype(BF16) * o_ref[...]
        mixed = term if mixed is None else mixed + term
    out_ref[...] = x_ref[...] + jnp.dot(mixed, w_ref[...], preferred_element_type=F32)


def _mix_matmul_residual(outs, lses, w, layer, x, *, tm=512):
    m, d = x.shape
    row_spec = pl.BlockSpec((tm, d), lambda i: (i, 0))
    return pl.pallas_call(
        _mix_matmul_residual_kernel, grid=(m // tm,),
        in_specs=[row_spec] * N_GROUPS + [pl.BlockSpec((tm, LANES), lambda i: (i, 0))] * N_GROUPS
        + [pl.BlockSpec((None, d, d), lambda i: (layer, 0, 0), pipeline_mode=pl.Buffered(1)),
           row_spec],
        out_specs=row_spec, out_shape=jax.ShapeDtypeStruct((m, d), F32),
        compiler_params=_params("parallel"), name="mix_matmul_residual",
    )(*outs, *lses, w, x)


def _to_planes(a, batch):
    rows, c = a.shape
    return (a.reshape(batch, rows // batch // SEQ_PLANES, SEQ_PLANES, c)
            .transpose(0, 2, 1, 3).reshape(rows, c))


def _from_planes(a, batch):
    rows, c = a.shape
    return (a.reshape(batch, SEQ_PLANES, rows // batch // SEQ_PLANES, c)
            .transpose(0, 2, 1, 3).reshape(rows, c))


def _rope_dim_order(a):
    h, half = ROPE_HALF, LANES // 2
    return jnp.concatenate([a[..., :h], a[..., half:half + h], a[..., 2 * h:half],
                            a[..., h:2 * h], a[..., half + h:]], axis=-1)


def _rope_tables(seq):
    inv_freq = ROPE_THETA ** (-jnp.arange(ROPE_HALF, dtype=F32) * 2.0 / ROPE_DIM)
    ang = jnp.arange(seq).astype(F32)[:, None] * inv_freq[None, :]
    cos, sin = jnp.cos(ang), jnp.sin(ang)
    gap = LANES // 2 - ROPE_HALF
    cos_t = jnp.concatenate([cos, jnp.ones((seq, gap), F32), cos, jnp.ones((seq, gap), F32)], axis=1)
    sin_t = jnp.concatenate([sin, jnp.zeros((seq, gap), F32), -sin, jnp.zeros((seq, gap), F32)],
                            axis=1)
    return _to_planes(cos_t, 1), _to_planes(sin_t, 1)


def _head_weight_kernel(w_ref, o_ref):
    h, half = ROPE_HALF, LANES // 2
    src = lax.broadcasted_iota(jnp.int32, (MXU_COLS, MXU_COLS), 0)
    dst = lax.broadcasted_iota(jnp.int32, (MXU_COLS, MXU_COLS), 1)
    d = dst % LANES
    from_lane = jnp.where((d >= h) & (d < 2 * h), d + (half - h),
                          jnp.where((d >= half) & (d < half + h), d - (half - h), d))
    perm = (src == (dst - d) + from_lane).astype(BF16)
    for c in range(w_ref.shape[1] // MXU_COLS):
        cols = slice(c * MXU_COLS, (c + 1) * MXU_COLS)
        o_ref[:, cols] = jnp.dot(w_ref[:, cols].astype(BF16), perm,
                                 preferred_element_type=F32).astype(BF16)


def _head_weight_cols(w, n_cols, *, tr=256, tn=6144):
    layers, d, _ = w.shape
    return pl.pallas_call(
        _head_weight_kernel, grid=(layers, d // tr, n_cols // tn),
        in_specs=[pl.BlockSpec((None, tr, tn), lambda l, i, j: (l, i, j))],
        out_specs=pl.BlockSpec((None, tr, tn), lambda l, i, j: (l, i, j)),
        out_shape=jax.ShapeDtypeStruct((layers, d, n_cols), BF16),
        compiler_params=_params("parallel", "parallel", "parallel"), name="head_weight_cols",
    )(w)


def _head_gain_cols(gain, scale=1.0):
    g = _rope_dim_order(gain * scale)
    return jnp.broadcast_to(g[:, None, :], (N_GROUPS, B_HEADS, B_HD)).reshape(1, -1)


def kernel(x, a_norm, a_w_in, a_o_gain, a_w_out, a_lower_bounds, kv_norm, w_kv, k_gain,
           b_norm, b_w_q, b_q_gain, b_w_out, ffn_norm, ffn_w_in, ffn_w_out):
    batch, seq, d = x.shape
    m = batch * seq
    xf = x.reshape(m, d)

    a_w_in_b, a_w_out_b = a_w_in.astype(BF16), a_w_out.astype(BF16)
    gd = N_GROUPS * D_MODEL
    w_k_b = _head_weight_cols(w_kv[None], gd)
    w_v_b = w_kv[:, gd:].astype(BF16)[None]
    b_w_q_b, b_w_out_b = _head_weight_cols(b_w_q, gd), b_w_out.astype(BF16)
    ffn_w_in_b, ffn_w_out_b = ffn_w_in.astype(BF16), ffn_w_out.astype(BF16)

    rope_tabs = _rope_tables(seq)
    plane_shape = (batch, SEQ_PLANES, seq // SEQ_PLANES, gd)
    k = v = None
    for layer in range(DEPTH):
        if layer < N_A_LAYERS:
            proj = _norm_matmul(xf, a_norm[layer][None], a_w_in_b, layer, F32, tn=2048)
            og = _hgrn(proj, a_lower_bounds, a_o_gain[layer][None], layer, batch, seq)
            xf = _matmul_residual(og, a_w_out_b, layer, xf)
        else:
            if layer == N_A_LAYERS:
                xf = _to_planes(xf, batch)
                k = _norm_matmul(xf, kv_norm[None], w_k_b, 0, BF16,
                                 rope=(_head_gain_cols(k_gain),) + rope_tabs)
                v = _norm_matmul(xf, kv_norm[None], w_v_b, 0, BF16, tn=2048)
                k, v = k.reshape(plane_shape), v.reshape(plane_shape)
            jb = layer - N_A_LAYERS
            q = _norm_matmul(xf, b_norm[jb][None], b_w_q_b, jb, BF16,
                             rope=(_head_gain_cols(b_q_gain[jb], B_HD ** -0.5 * LOG2_E),)
                             + rope_tabs)
            q = q.reshape(plane_shape)
            outs, lses = zip(*[_attn_group(q, k, v, group) for group in range(N_GROUPS)])
            xf = _mix_matmul_residual(outs, lses, b_w_out_b, jb, xf)
        xf = _ffn(xf, ffn_norm[layer][None], ffn_w_in_b, ffn_w_out_b, layer)
    return _from_planes(xf, batch).reshape(batch, seq, d)
```

```python
import functools

import jax
import jax.numpy as jnp
from jax import lax
from jax.experimental import pallas as pl
from jax.experimental.pallas import tpu as pltpu

F32 = jnp.float32
BF16 = jnp.bfloat16

D_MODEL = 2048
DEPTH = 4
N_A_LAYERS = DEPTH // 2
A_HEADS = 16
A_DK = D_MODEL // A_HEADS
A_CHUNK = 64
B_HEADS = 16
B_HD = D_MODEL // B_HEADS
B_CONFIGS = ((128, 1), (512, 4), (2048, 16))
N_GROUPS = len(B_CONFIGS)
B_QBLOCK = 128
ROPE_THETA = 500000.0
ROPE_DIM = B_HD // 4
ROPE_HALF = ROPE_DIM // 2
FFN_HIDDEN = ((8 * D_MODEL // 3 + 255) // 256) * 256
EPS = 1e-6

LANES = 128
MXU_COLS = 256
SEQ_PLANES = 8
VMEM_LIMIT_BYTES = 56 * 1024 * 1024
PROJ_TILE = (1024, 2048)
ROPE_PROJ_ROWS = 512
OUT_PROJ_ROWS = 1024
MIX_ROWS = 512
FFN_TILE = (1024, 512)
HGRN_ROWS = 2048
WEIGHT_ROWS = 256
MASK_VALUE = -1e30
LOG2_E = 1.4426950408889634

NT_DIMS = (((1,), (1,)), ((), ()))
TN_DIMS = (((0,), (0,)), ((), ()))


def _params(*semantics):
    return pltpu.CompilerParams(dimension_semantics=semantics,
                                vmem_limit_bytes=VMEM_LIMIT_BYTES)


def _rms_rows(x, gain):
    ms = jnp.mean(x * x, axis=-1, keepdims=True)
    return x * lax.rsqrt(ms + EPS) * gain


def _norm_matmul_kernel(x_ref, g_ref, w_ref, o_ref, xn_ref):
    @pl.when(pl.program_id(1) == 0)
    def _():
        xn_ref[...] = _rms_rows(x_ref[...], g_ref[...]).astype(BF16)

    o_ref[...] = jnp.dot(xn_ref[...], w_ref[...],
                         preferred_element_type=F32).astype(o_ref.dtype)


def _norm_matmul_rope_kernel(x_ref, g_ref, w_ref, hg_ref, cos_ref, sin_ref, o_ref, xn_ref,
                             acc_ref, scale_ref, rot_ref):
    @pl.when(pl.program_id(1) == 0)
    def _():
        xn_ref[...] = _rms_rows(x_ref[...], g_ref[...]).astype(BF16)

    xn = xn_ref[...]
    tm = xn.shape[0]
    for c in range(o_ref.shape[1] // MXU_COLS):
        acc_ref[...] = jnp.dot(xn, w_ref[:, c * MXU_COLS:(c + 1) * MXU_COLS],
                               preferred_element_type=F32)
        for h in range(MXU_COLS // LANES):
            hs = slice(h * LANES, (h + 1) * LANES)
            cols = slice(c * MXU_COLS + h * LANES, c * MXU_COLS + (h + 1) * LANES)
            y = acc_ref[:, hs]
            ms = jnp.mean(y * y, axis=-1, keepdims=True)
            scale_ref[:, hs] = jnp.broadcast_to(lax.rsqrt(ms + EPS), (tm, LANES)) * hg_ref[:, cols]
        for h in range(MXU_COLS // LANES):
            hs = slice(h * LANES, (h + 1) * LANES)
            y = acc_ref[:, hs] * scale_ref[:, hs]
            rot_ref[:, hs] = pltpu.roll(y * sin_ref[...], LANES // 2, 1)
        for h in range(MXU_COLS // LANES):
            hs = slice(h * LANES, (h + 1) * LANES)
            cols = slice(c * MXU_COLS + h * LANES, c * MXU_COLS + (h + 1) * LANES)
            y = acc_ref[:, hs] * scale_ref[:, hs]
            o_ref[:, cols] = (y * cos_ref[...] + rot_ref[:, hs]).astype(o_ref.dtype)


def _norm_matmul(x, gain, w, layer, out_dtype, *, rope=None):
    m, d = x.shape
    n = w.shape[-1]
    tm, tn = PROJ_TILE
    if rope is not None:
        tm, tn = ROPE_PROJ_ROWS, n
    grid = (m // tm, n // tn)
    x_spec = pl.BlockSpec((tm, d), lambda i, j: (i, 0))
    g_spec = pl.BlockSpec((1, d), lambda i, j: (0, 0))
    w_mode = dict(pipeline_mode=pl.Buffered(1)) if tn == n else {}
    w_spec = pl.BlockSpec((None, d, tn), lambda i, j: (layer, 0, j), **w_mode)
    o_spec = pl.BlockSpec((tm, tn), lambda i, j: (i, j))
    scratch = [pltpu.VMEM((tm, d), BF16)]
    out_shape = jax.ShapeDtypeStruct((m, n), out_dtype)
    if rope is None:
        return pl.pallas_call(
            _norm_matmul_kernel, grid=grid, in_specs=[x_spec, g_spec, w_spec],
            out_specs=o_spec, out_shape=out_shape, scratch_shapes=scratch,
            compiler_params=_params("parallel", "arbitrary"), name="norm_matmul",
        )(x, gain, w)
    head_gain, cos_t, sin_t = rope
    seq_blocks = cos_t.shape[0] // tm
    tab_spec = pl.BlockSpec((tm, LANES), lambda i, j: (i % seq_blocks, 0))
    hg_spec = pl.BlockSpec((1, tn), lambda i, j: (0, j))
    return pl.pallas_call(
        _norm_matmul_rope_kernel, grid=grid,
        in_specs=[x_spec, g_spec, w_spec, hg_spec, tab_spec, tab_spec],
        out_specs=o_spec, out_shape=out_shape,
        scratch_shapes=scratch + [pltpu.VMEM((tm, MXU_COLS), F32)] * 3,
        compiler_params=_params("parallel", "arbitrary"), name="norm_matmul_rope",
    )(x, gain, w, head_gain, cos_t, sin_t)


def _matmul_residual_kernel(a_ref, w_ref, x_ref, o_ref):
    o_ref[...] = x_ref[...] + jnp.dot(a_ref[...], w_ref[...], preferred_element_type=F32)


def _matmul_residual(a, w, layer, x):
    m, k = a.shape
    tm = OUT_PROJ_ROWS
    n = w.shape[-1]
    return pl.pallas_call(
        _matmul_residual_kernel, grid=(m // tm,),
        in_specs=[pl.BlockSpec((tm, k), lambda i: (i, 0)),
                  pl.BlockSpec((None, k, n), lambda i: (layer, 0, 0),
                               pipeline_mode=pl.Buffered(1)),
                  pl.BlockSpec((tm, n), lambda i: (i, 0))],
        out_specs=pl.BlockSpec((tm, n), lambda i: (i, 0)),
        out_shape=jax.ShapeDtypeStruct((m, n), F32),
        compiler_params=_params("parallel"), name="matmul_residual",
    )(a, w, x)


def _ffn_kernel(x_ref, g_ref, wg_ref, wu_ref, wo_ref, o_ref, xn_ref):
    j = pl.program_id(1)

    @pl.when(j == 0)
    def _():
        x = x_ref[...]
        xn_ref[...] = _rms_rows(x, g_ref[...]).astype(BF16)
        o_ref[...] = x

    xn = xn_ref[...]
    acts = []
    for c in range(wg_ref.shape[1] // MXU_COLS):
        cs = slice(c * MXU_COLS, (c + 1) * MXU_COLS)
        gate = jnp.dot(xn, wg_ref[:, cs], preferred_element_type=F32)
        up = jnp.dot(xn, wu_ref[:, cs], preferred_element_type=F32)
        acts.append((gate / (1.0 + jnp.exp(-gate)) * up).astype(BF16))
    act = jnp.concatenate(acts, axis=1)
    o_ref[...] += jnp.dot(act, wo_ref[...], preferred_element_type=F32)


def _ffn(x, gain, w_in, w_out, layer):
    m, d = x.shape
    tm, th = FFN_TILE
    hidden = w_out.shape[1]
    n_h = hidden // th
    return pl.pallas_call(
        _ffn_kernel, grid=(m // tm, n_h),
        in_specs=[pl.BlockSpec((tm, d), lambda i, j: (i, 0)),
                  pl.BlockSpec((1, d), lambda i, j: (0, 0)),
                  pl.BlockSpec((None, d, th), lambda i, j: (layer, 0, j)),
                  pl.BlockSpec((None, d, th), lambda i, j: (layer, 0, n_h + j)),
                  pl.BlockSpec((None, th, d), lambda i, j: (layer, j, 0))],
        out_specs=pl.BlockSpec((tm, d), lambda i, j: (i, 0)),
        out_shape=jax.ShapeDtypeStruct((m, d), F32),
        scratch_shapes=[pltpu.VMEM((tm, d), BF16)],
        compiler_params=_params("parallel", "arbitrary"), name="ffn",
    )(x, gain, w_in, w_in, w_out)


def _split3_bf16(x):
    hi = x.astype(BF16)
    r1 = x - hi.astype(F32)
    mid = r1.astype(BF16)
    lo = (r1 - mid.astype(F32)).astype(BF16)
    return hi, mid, lo


def _hgrn_kernel(q_ref, z_ref, i_ref, g_ref, lb_ref, og_ref, o_ref, st_ref, c_scr, b_scr, *,
                 layer, n_chunks):
    C = A_CHUNK
    SUB = 8
    NB = C // SUB

    @pl.when(pl.program_id(2) == 0)
    def _():
        st_ref[...] = jnp.zeros_like(st_ref)

    a = lb_ref[...]
    e = jnp.exp(a - jnp.max(a, axis=0, keepdims=True))
    sm = e / jnp.sum(e, axis=0, keepdims=True)
    lb = jnp.sum(sm[:layer + 1], axis=0, keepdims=True) - sm[0:1]
    lb = jnp.clip(lb, 0.0, 1.0 - 1e-6)
    log_lb = jnp.log(lb)
    one_m_lb = 1.0 - lb
    log_one_m_lb = jnp.log(one_m_lb)
    o_gain = og_ref[...]

    tri = (lax.broadcasted_iota(jnp.int32, (C, C), 0)
           >= lax.broadcasted_iota(jnp.int32, (C, C), 1)).astype(BF16)
    ti = lax.broadcasted_iota(jnp.int32, (C, C), 0)
    si = lax.broadcasted_iota(jnp.int32, (C, C), 1)
    same_32 = (ti // 32) == (si // 32)
    same_16 = (ti // 16) == (si // 16)
    s_sub = lax.broadcasted_iota(jnp.int32, (SUB, C), 1)
    zero_blk = jnp.zeros((SUB, LANES), F32)

    def blocks(x):
        return [x[j * SUB:(j + 1) * SUB] for j in range(NB)]

    def padded(parts):
        return jnp.concatenate([zero_blk if p is None else p for p in parts],
                               axis=0).astype(BF16)

    def bcast_row(ref, r):
        return jnp.broadcast_to(ref[r:r + 1, :], (SUB, LANES))

    chunks = [slice(c * C, (c + 1) * C) for c in range(n_chunks)]

    ks_all, bs_all = [], []
    for rows in chunks:
        z = z_ref[rows, :]
        ez = jnp.exp(-jnp.abs(z))
        log_sig = jnp.minimum(z, 0.0) - jnp.log(1.0 + ez)
        k = one_m_lb * (jnp.where(z >= 0.0, ez, 1.0) / (1.0 + ez))
        y = log_one_m_lb + log_sig
        hi_ = jnp.maximum(log_lb, y)
        lo_ = jnp.minimum(log_lb, y)
        log_f = hi_ + jnp.log(1.0 + jnp.exp(lo_ - hi_))
        b = sum(jnp.dot(tri, part, preferred_element_type=F32)
                for part in _split3_bf16(log_f * LOG2_E))
        b_scr[rows, :] = b
        c_scr[rows, :] = b - (log_one_m_lb + log_sig - z) * LOG2_E
        ks_all.append(blocks(k))
        bs_all.append(blocks(b))

    levels_all, upd_all, qdec_all = [], [], []
    for rows, ks, bs in zip(chunks, ks_all, bs_all):
        qs = blocks(q_ref[rows, :])
        ends = [bcast_row(b_scr, rows.start + j * SUB + SUB - 1) for j in range(NB)]
        a_levels = []
        for size in (4, 2, 1):
            qp, kp = [None] * NB, [None] * NB
            for pair in range(NB // (2 * size)):
                lo = pair * 2 * size
                r = ends[lo + size - 1]
                for j in range(lo, lo + size):
                    kp[j] = ks[j] * jnp.exp2(r - bs[j])
                for j in range(lo + size, lo + 2 * size):
                    qp[j] = qs[j] * jnp.exp2(bs[j] - r)
            a_levels.append(lax.dot_general(padded(qp), padded(kp), NT_DIMS,
                                            preferred_element_type=F32))
        levels_all.append(a_levels)
        k_dec = jnp.concatenate([ks[j] * jnp.exp2(ends[NB - 1] - bs[j]) for j in range(NB)],
                                axis=0)
        upd_all.append(lax.dot_general(i_ref[rows, :].astype(BF16), k_dec.astype(BF16), TN_DIMS,
                                       preferred_element_type=F32))
        qdec_all.append(jnp.concatenate([qs[j] * jnp.exp2(bs[j]) for j in range(NB)],
                                        axis=0).astype(BF16))

    a_all = []
    for rows, bs, a_levels in zip(chunks, bs_all, levels_all):
        qs = blocks(q_ref[rows, :])
        diag = [jnp.zeros((SUB, C), F32)] * NB
        for s in range(SUB):
            for j in range(NB):
                r = j * SUB + s
                w = jnp.exp2(bs[j] - bcast_row(c_scr, rows.start + r))
                a_s = jnp.sum(qs[j] * w, axis=-1, keepdims=True)
                diag[j] = jnp.where(s_sub == r, a_s, diag[j])
        a = (a_levels[0] + jnp.where(same_32, a_levels[1], 0.0)
             + jnp.where(same_16, a_levels[2], 0.0)
             + jnp.where(ti >= si, jnp.concatenate(diag, axis=0), 0.0))
        a_all.append(a.astype(BF16))

    st = st_ref[...]
    for rows, a, upd, q_dec in zip(chunks, a_all, upd_all, qdec_all):
        o = (lax.dot_general(q_dec, st.astype(BF16), NT_DIMS, preferred_element_type=F32)
             + jnp.dot(a, i_ref[rows, :].astype(BF16), preferred_element_type=F32))
        st = st * jnp.exp2(b_scr[rows.stop - 1:rows.stop, :]) + upd
        g = g_ref[rows, :]
        out = _rms_rows(o, o_gain) * (g / (1.0 + jnp.exp(-g)))
        o_ref[rows, :] = out.astype(o_ref.dtype)

    st_ref[...] = st


def _hgrn(proj, a_lower_bounds, o_gain, layer, batch, seq, *, rows=HGRN_ROWS):
    n_r = seq // rows
    h_ = A_HEADS

    def sec_spec(sec):
        return pl.BlockSpec((rows, A_DK), lambda b, h, r: (b * n_r + r, sec * h_ + h))

    kern = functools.partial(_hgrn_kernel, layer=layer, n_chunks=rows // A_CHUNK)
    return pl.pallas_call(
        kern, grid=(batch, h_, n_r),
        in_specs=[sec_spec(0), sec_spec(1), sec_spec(2), sec_spec(3),
                  pl.BlockSpec((N_A_LAYERS, A_DK), lambda b, h, r: (0, h)),
                  pl.BlockSpec((1, A_DK), lambda b, h, r: (0, 0))],
        out_specs=pl.BlockSpec((rows, A_DK), lambda b, h, r: (b * n_r + r, h)),
        out_shape=jax.ShapeDtypeStruct((batch * seq, D_MODEL), BF16),
        scratch_shapes=[pltpu.VMEM((A_DK, A_DK), F32), pltpu.VMEM((rows, A_DK), F32),
                        pltpu.VMEM((rows, A_DK), F32)],
        compiler_params=_params("parallel", "parallel", "arbitrary"), name="hgrn",
    )(proj, proj, proj, proj, a_lower_bounds, o_gain)


def _attn_kernel(q_ref, kc_ref, vc_ref, kp_ref, vp_ref, o_ref, l_ref, bias_ref, *,
                 planes, tile_rows, key_rows, den, block_rows):
    QB, P, R, KW = B_QBLOCK, planes, tile_rows, key_rows
    n_keys = P * (KW + R)
    n_edge = KW // R
    kc = lax.broadcasted_iota(jnp.int32, (n_keys, QB), 0)
    qa = lax.broadcasted_iota(jnp.int32, (n_keys, QB), 1)
    key_row = kc % (KW + R) - KW
    num = P * (qa % R - key_row) + (qa // R - kc // (KW + R))
    band = (num >= 0) & (num <= QB * den) & (num % den == 0)
    block_row0 = pl.program_id(2) * block_rows
    bias_ref[n_edge] = jnp.where(band, 0.0, MASK_VALUE)
    for t in range(n_edge):
        ok = band & (key_row >= -(block_row0 + t * R))
        bias_ref[t] = jnp.where(ok, 0.0, MASK_VALUE)

    def gather(ref, rows, cols):
        return jnp.concatenate([ref[0, m, rows, cols] for m in range(P)], axis=0)

    for t in range(block_rows // R):
        rows = slice(t * R, (t + 1) * R)
        first = t * R - KW

        def window(cur_ref, prev_ref, cols):
            if first >= 0:
                return gather(cur_ref, slice(first, (t + 1) * R), cols)
            parts = []
            for m in range(P):
                parts += [prev_ref[0, m, KW + first:KW, cols], cur_ref[0, m, 0:(t + 1) * R, cols]]
            return jnp.concatenate(parts, axis=0)

        def scores(h):
            cols = slice(h * B_HD, (h + 1) * B_HD)
            return lax.dot_general(window(kc_ref, kp_ref, cols), gather(q_ref, rows, cols),
                                   NT_DIMS, preferred_element_type=F32)

        lse_rows = []
        s_next = scores(0)
        for h in range(B_HEADS):
            cols = slice(h * B_HD, (h + 1) * B_HD)
            vals = window(vc_ref, vp_ref, cols)
            s = s_next + bias_ref[min(t, n_edge)]
            if h + 1 < B_HEADS:
                s_next = scores(h + 1)
            m = jnp.max(s, axis=0, keepdims=True)
            p = jnp.exp2(s - m)
            l = jnp.sum(p, axis=0, keepdims=True)
            o = lax.dot_general((p * (1.0 / l)).astype(BF16), vals, TN_DIMS,
                                preferred_element_type=F32).astype(o_ref.dtype)
            for m_ in range(P):
                o_ref[0, m_, rows, cols] = o[m_ * R:(m_ + 1) * R]
            lse_rows.append((m + jnp.log2(l)) * (1.0 / LOG2_E))
        pad = jnp.zeros((LANES - B_HEADS, QB), F32)
        lse_tile = jnp.concatenate(lse_rows + [pad], axis=0).T
        for m_ in range(P):
            l_ref[0, m_, rows, :] = lse_tile[m_ * R:(m_ + 1) * R]


def _attn_group(q, k, v, group):
    batch, n_planes, plane_rows, _ = q.shape
    window, dil = B_CONFIGS[group]
    assert window // dil == B_QBLOCK
    planes = max(n_planes // dil, 1)
    den = max(dil // n_planes, 1)
    n_sets = n_planes // planes
    tile_rows = B_QBLOCK // planes
    key_rows = tile_rows * den
    block_rows = plane_rows // planes
    D = D_MODEL

    def view(a):
        return a.reshape(batch, planes, n_sets, plane_rows, a.shape[-1])

    def prev_block(n):
        return jnp.maximum(n * (block_rows // key_rows) - 1, 0)

    cur = pl.BlockSpec((1, planes, None, block_rows, D), lambda b, s, n: (b, 0, s, n, group))
    before = pl.BlockSpec((1, planes, None, key_rows, D),
                          lambda b, s, n: (b, 0, s, prev_block(n), group))
    o_spec = pl.BlockSpec((1, planes, None, block_rows, D), lambda b, s, n: (b, 0, s, n, 0))
    l_spec = pl.BlockSpec((1, planes, None, block_rows, LANES), lambda b, s, n: (b, 0, s, n, 0))
    kern = functools.partial(
        _attn_kernel, planes=planes, tile_rows=tile_rows, key_rows=key_rows, den=den,
        block_rows=block_rows)
    o, lse = pl.pallas_call(
        kern, grid=(batch, n_sets, plane_rows // block_rows),
        in_specs=[cur, cur, cur, before, before], out_specs=[o_spec, l_spec],
        out_shape=[jax.ShapeDtypeStruct((batch, planes, n_sets, plane_rows, D), BF16),
                   jax.ShapeDtypeStruct((batch, planes, n_sets, plane_rows, LANES), F32)],
        scratch_shapes=[pltpu.VMEM((key_rows // tile_rows + 1,
                                    planes * (key_rows + tile_rows), B_QBLOCK), F32)],
        compiler_params=_params("parallel", "parallel", "arbitrary"),
        name=f"attn_g{group}",
    )(view(q), view(k), view(v), view(k), view(v))
    rows = batch * n_planes * plane_rows
    return o.reshape(rows, D), lse.reshape(rows, LANES)


def _mix_matmul_residual_kernel(*refs):
    o_refs, l_refs = refs[:N_GROUPS], refs[N_GROUPS:2 * N_GROUPS]
    w_ref, x_ref, out_ref = refs[2 * N_GROUPS:]
    lses = [l[...] for l in l_refs]
    top = functools.reduce(jnp.maximum, lses)
    es = [jnp.exp(l - top) for l in lses]
    inv = 1.0 / functools.reduce(jnp.add, es)
    spread = (lax.broadcasted_iota(jnp.int32, (LANES, D_MODEL), 1) // B_HD
              == lax.broadcasted_iota(jnp.int32, (LANES, D_MODEL), 0)).astype(BF16)
    mixed = None
    for o_ref, e in zip(o_refs, es):
        wts = jnp.dot((e * inv).astype(BF16), spread, preferred_element_type=F32)
        term = wts.astype(BF16) * o_ref[...]
        mixed = term if mixed is None else mixed + term
    out_ref[...] = x_ref[...] + jnp.dot(mixed, w_ref[...], preferred_element_type=F32)


def _mix_matmul_residual(outs, lses, w, layer, x, *, tm=MIX_ROWS):
    m, d = x.shape
    row_spec = pl.BlockSpec((tm, d), lambda i: (i, 0))
    return pl.pallas_call(
        _mix_matmul_residual_kernel, grid=(m // tm,),
        in_specs=[row_spec] * N_GROUPS + [pl.BlockSpec((tm, LANES), lambda i: (i, 0))] * N_GROUPS
        + [pl.BlockSpec((None, d, d), lambda i: (layer, 0, 0), pipeline_mode=pl.Buffered(1)),
           row_spec],
        out_specs=row_spec, out_shape=jax.ShapeDtypeStruct((m, d), F32),
        compiler_params=_params("parallel"), name="mix_matmul_residual",
    )(*outs, *lses, w, x)


def _to_planes(a, batch):
    rows, c = a.shape
    return (a.reshape(batch, rows // batch // SEQ_PLANES, SEQ_PLANES, c)
            .transpose(0, 2, 1, 3).reshape(rows, c))


def _from_planes(a, batch):
    rows, c = a.shape
    return (a.reshape(batch, SEQ_PLANES, rows // batch // SEQ_PLANES, c)
            .transpose(0, 2, 1, 3).reshape(rows, c))


def _rope_dim_order(a):
    h, half = ROPE_HALF, LANES // 2
    return jnp.concatenate([a[..., :h], a[..., half:half + h], a[..., 2 * h:half],
                            a[..., h:2 * h], a[..., half + h:]], axis=-1)


def _rope_tables(seq):
    inv_freq = ROPE_THETA ** (-jnp.arange(ROPE_HALF, dtype=F32) * 2.0 / ROPE_DIM)
    ang = jnp.arange(seq).astype(F32)[:, None] * inv_freq[None, :]
    cos, sin = jnp.cos(ang), jnp.sin(ang)
    gap = LANES // 2 - ROPE_HALF
    cos_t = jnp.concatenate([cos, jnp.ones((seq, gap), F32), cos, jnp.ones((seq, gap), F32)], axis=1)
    sin_t = jnp.concatenate([sin, jnp.zeros((seq, gap), F32), -sin, jnp.zeros((seq, gap), F32)],
                            axis=1)
    return _to_planes(cos_t, 1), _to_planes(sin_t, 1)


def _head_weight_kernel(w_ref, o_ref):
    h, half = ROPE_HALF, LANES // 2
    src = lax.broadcasted_iota(jnp.int32, (MXU_COLS, MXU_COLS), 0)
    dst = lax.broadcasted_iota(jnp.int32, (MXU_COLS, MXU_COLS), 1)
    d = dst % LANES
    from_lane = jnp.where((d >= h) & (d < 2 * h), d + (half - h),
                          jnp.where((d >= half) & (d < half + h), d - (half - h), d))
    perm = (src == (dst - d) + from_lane).astype(BF16)
    for c in range(w_ref.shape[1] // MXU_COLS):
        cols = slice(c * MXU_COLS, (c + 1) * MXU_COLS)
        o_ref[:, cols] = jnp.dot(w_ref[:, cols].astype(BF16), perm,
                                 preferred_element_type=F32).astype(BF16)


def _head_weight_cols(w, n_cols, *, tr=WEIGHT_ROWS):
    layers, d, _ = w.shape
    spec = pl.BlockSpec((None, tr, n_cols), lambda l, i: (l, i, 0))
    return pl.pallas_call(
        _head_weight_kernel, grid=(layers, d // tr), in_specs=[spec], out_specs=spec,
        out_shape=jax.ShapeDtypeStruct((layers, d, n_cols), BF16),
        compiler_params=_params("parallel", "parallel"), name="head_weight_cols",
    )(w)


def _head_gain_cols(gain, scale=1.0):
    g = _rope_dim_order(gain * scale)
    return jnp.broadcast_to(g[:, None, :], (N_GROUPS, B_HEADS, B_HD)).reshape(1, -1)


def kernel(x, a_norm, a_w_in, a_o_gain, a_w_out, a_lower_bounds, kv_norm, w_kv, k_gain,
           b_norm, b_w_q, b_q_gain, b_w_out, ffn_norm, ffn_w_in, ffn_w_out):
    batch, seq, d = x.shape
    m = batch * seq
    xf = x.reshape(m, d)

    a_w_in_b, a_w_out_b = a_w_in.astype(BF16), a_w_out.astype(BF16)
    gd = N_GROUPS * D_MODEL
    w_k_b = _head_weight_cols(w_kv[None], gd)
    w_v_b = w_kv[:, gd:].astype(BF16)[None]
    b_w_q_b, b_w_out_b = _head_weight_cols(b_w_q, gd), b_w_out.astype(BF16)
    ffn_w_in_b, ffn_w_out_b = ffn_w_in.astype(BF16), ffn_w_out.astype(BF16)

    rope_tabs = _rope_tables(seq)
    plane_shape = (batch, SEQ_PLANES, seq // SEQ_PLANES, gd)
    k = v = None
    for layer in range(DEPTH):
        if layer < N_A_LAYERS:
            proj = _norm_matmul(xf, a_norm[layer][None], a_w_in_b, layer, F32)
            og = _hgrn(proj, a_lower_bounds, a_o_gain[layer][None], layer, batch, seq)
            xf = _matmul_residual(og, a_w_out_b, layer, xf)
        else:
            if layer == N_A_LAYERS:
                xf = _to_planes(xf, batch)
                k = _norm_matmul(xf, kv_norm[None], w_k_b, 0, BF16,
                                 rope=(_head_gain_cols(k_gain),) + rope_tabs)
                v = _norm_matmul(xf, kv_norm[None], w_v_b, 0, BF16)
                k, v = k.reshape(plane_shape), v.reshape(plane_shape)
            jb = layer - N_A_LAYERS
            q = _norm_matmul(xf, b_norm[jb][None], b_w_q_b, jb, BF16,
                             rope=(_head_gain_cols(b_q_gain[jb], B_HD ** -0.5 * LOG2_E),)
                             + rope_tabs)
            q = q.reshape(plane_shape)
            outs, lses = zip(*[_attn_group(q, k, v, group) for group in range(N_GROUPS)])
            xf = _mix_matmul_residual(outs, lses, b_w_out_b, jb, xf)
        xf = _ffn(xf, ffn_norm[layer][None], ffn_w_in_b, ffn_w_out_b, layer)
    return _from_planes(xf, batch).reshape(batch, seq, d)
```

```python
import functools

import jax
import jax.numpy as jnp
from jax import lax
from jax.experimental import pallas as pl
from jax.experimental.pallas import tpu as pltpu

F32 = jnp.float32
BF16 = jnp.bfloat16

D_MODEL = 2048
DEPTH = 4
N_A_LAYERS = DEPTH // 2
A_HEADS = 16
A_DK = D_MODEL // A_HEADS
A_CHUNK = 64
B_HEADS = 16
B_HD = D_MODEL // B_HEADS
B_CONFIGS = ((128, 1), (512, 4), (2048, 16))
N_GROUPS = len(B_CONFIGS)
B_QBLOCK = 128
ROPE_THETA = 500000.0
ROPE_DIM = B_HD // 4
ROPE_HALF = ROPE_DIM // 2
FFN_HIDDEN = ((8 * D_MODEL // 3 + 255) // 256) * 256
EPS = 1e-6

LANES = 128
MXU_COLS = 256
SEQ_PLANES = 8
VMEM_LIMIT_BYTES = 56 * 1024 * 1024
PROJ_TILE = (1024, 2048)
ROPE_PROJ_ROWS = 512
OUT_PROJ_ROWS = 1024
MIX_ROWS = 512
FFN_TILE = (1024, 512)
HGRN_ROWS = 2048
WEIGHT_ROWS = 256
MASK_VALUE = -1e30
LOG2_E = 1.4426950408889634

NT_DIMS = (((1,), (1,)), ((), ()))
TN_DIMS = (((0,), (0,)), ((), ()))


def _params(*semantics):
    return pltpu.CompilerParams(dimension_semantics=semantics,
                                vmem_limit_bytes=VMEM_LIMIT_BYTES)


def _rms_rows(x, gain):
    ms = jnp.mean(x * x, axis=-1, keepdims=True)
    return x * lax.rsqrt(ms + EPS) * gain


def _norm_matmul_kernel(x_ref, g_ref, w_ref, o_ref, xn_ref):
    @pl.when(pl.program_id(1) == 0)
    def _():
        xn_ref[...] = _rms_rows(x_ref[...], g_ref[...]).astype(BF16)

    o_ref[...] = jnp.dot(xn_ref[...], w_ref[...],
                         preferred_element_type=F32).astype(o_ref.dtype)


def _norm_matmul_rope_kernel(x_ref, g_ref, w_ref, hg_ref, cos_ref, sin_ref, o_ref, xn_ref,
                             acc_ref, scale_ref, rot_ref):
    @pl.when(pl.program_id(1) == 0)
    def _():
        xn_ref[...] = _rms_rows(x_ref[...], g_ref[...]).astype(BF16)

    xn = xn_ref[...]
    tm = xn.shape[0]
    for c in range(o_ref.shape[1] // MXU_COLS):
        acc_ref[...] = jnp.dot(xn, w_ref[:, c * MXU_COLS:(c + 1) * MXU_COLS],
                               preferred_element_type=F32)
        for h in range(MXU_COLS // LANES):
            hs = slice(h * LANES, (h + 1) * LANES)
            cols = slice(c * MXU_COLS + h * LANES, c * MXU_COLS + (h + 1) * LANES)
            y = acc_ref[:, hs]
            ms = jnp.mean(y * y, axis=-1, keepdims=True)
            scale_ref[:, hs] = jnp.broadcast_to(lax.rsqrt(ms + EPS), (tm, LANES)) * hg_ref[:, cols]
        for h in range(MXU_COLS // LANES):
            hs = slice(h * LANES, (h + 1) * LANES)
            y = acc_ref[:, hs] * scale_ref[:, hs]
            rot_ref[:, hs] = pltpu.roll(y * sin_ref[...], LANES // 2, 1)
        for h in range(MXU_COLS // LANES):
            hs = slice(h * LANES, (h + 1) * LANES)
            cols = slice(c * MXU_COLS + h * LANES, c * MXU_COLS + (h + 1) * LANES)
            y = acc_ref[:, hs] * scale_ref[:, hs]
            o_ref[:, cols] = (y * cos_ref[...] + rot_ref[:, hs]).astype(o_ref.dtype)


def _norm_matmul(x, gain, w, layer, out_dtype, *, rope=None):
    m, d = x.shape
    n = w.shape[-1]
    tm, tn = PROJ_TILE
    if rope is not None:
        tm, tn = ROPE_PROJ_ROWS, n
    grid = (m // tm, n // tn)
    x_spec = pl.BlockSpec((tm, d), lambda i, j: (i, 0))
    g_spec = pl.BlockSpec((1, d), lambda i, j: (0, 0))
    w_mode = dict(pipeline_mode=pl.Buffered(1)) if tn == n else {}
    w_spec = pl.BlockSpec((None, d, tn), lambda i, j: (layer, 0, j), **w_mode)
    o_spec = pl.BlockSpec((tm, tn), lambda i, j: (i, j))
    scratch = [pltpu.VMEM((tm, d), BF16)]
    out_shape = jax.ShapeDtypeStruct((m, n), out_dtype)
    if rope is None:
        return pl.pallas_call(
            _norm_matmul_kernel, grid=grid, in_specs=[x_spec, g_spec, w_spec],
            out_specs=o_spec, out_shape=out_shape, scratch_shapes=scratch,
            compiler_params=_params("parallel", "arbitrary"), name="norm_matmul",
        )(x, gain, w)
    head_gain, cos_t, sin_t = rope
    seq_blocks = cos_t.shape[0] // tm
    tab_spec = pl.BlockSpec((tm, LANES), lambda i, j: (i % seq_blocks, 0))
    hg_spec = pl.BlockSpec((1, tn), lambda i, j: (0, j))
    return pl.pallas_call(
        _norm_matmul_rope_kernel, grid=grid,
        in_specs=[x_spec, g_spec, w_spec, hg_spec, tab_spec, tab_spec],
        out_specs=o_spec, out_shape=out_shape,
        scratch_shapes=scratch + [pltpu.VMEM((tm, MXU_COLS), F32)] * 3,
        compiler_params=_params("parallel", "arbitrary"), name="norm_matmul_rope",
    )(x, gain, w, head_gain, cos_t, sin_t)


def _matmul_residual_kernel(a_ref, w_ref, x_ref, o_ref):
    o_ref[...] = x_ref[...] + jnp.dot(a_ref[...], w_ref[...], preferred_element_type=F32)


def _matmul_residual(a, w, layer, x):
    m, k = a.shape
    tm = OUT_PROJ_ROWS
    n = w.shape[-1]
    return pl.pallas_call(
        _matmul_residual_kernel, grid=(m // tm,),
        in_specs=[pl.BlockSpec((tm, k), lambda i: (i, 0)),
                  pl.BlockSpec((None, k, n), lambda i: (layer, 0, 0),
                               pipeline_mode=pl.Buffered(1)),
                  pl.BlockSpec((tm, n), lambda i: (i, 0))],
        out_specs=pl.BlockSpec((tm, n), lambda i: (i, 0)),
        out_shape=jax.ShapeDtypeStruct((m, n), F32),
        compiler_params=_params("parallel"), name="matmul_residual",
    )(a, w, x)


def _ffn_kernel(x_ref, g_ref, wg_ref, wu_ref, wo_ref, o_ref, xn_ref):
    j = pl.program_id(1)

    @pl.when(j == 0)
    def _():
        x = x_ref[...]
        xn_ref[...] = _rms_rows(x, g_ref[...]).astype(BF16)
        o_ref[...] = x

    xn = xn_ref[...]
    acts = []
    for c in range(wg_ref.shape[1] // MXU_COLS):
        cs = slice(c * MXU_COLS, (c + 1) * MXU_COLS)
        gate = jnp.dot(xn, wg_ref[:, cs], preferred_element_type=F32)
        up = jnp.dot(xn, wu_ref[:, cs], preferred_element_type=F32)
        acts.append((gate / (1.0 + jnp.exp(-gate)) * up).astype(BF16))
    act = jnp.concatenate(acts, axis=1)
    o_ref[...] += jnp.dot(act, wo_ref[...], preferred_element_type=F32)


def _ffn(x, gain, w_in, w_out, layer):
    m, d = x.shape
    tm, th = FFN_TILE
    hidden = w_out.shape[1]
    n_h = hidden // th
    return pl.pallas_call(
        _ffn_kernel, grid=(m // tm, n_h),
        in_specs=[pl.BlockSpec((tm, d), lambda i, j: (i, 0)),
                  pl.BlockSpec((1, d), lambda i, j: (0, 0)),
                  pl.BlockSpec((None, d, th), lambda i, j: (layer, 0, j)),
                  pl.BlockSpec((None, d, th), lambda i, j: (layer, 0, n_h + j)),
                  pl.BlockSpec((None, th, d), lambda i, j: (layer, j, 0))],
        out_specs=pl.BlockSpec((tm, d), lambda i, j: (i, 0)),
        out_shape=jax.ShapeDtypeStruct((m, d), F32),
        scratch_shapes=[pltpu.VMEM((tm, d), BF16)],
        compiler_params=_params("parallel", "arbitrary"), name="ffn",
    )(x, gain, w_in, w_in, w_out)


def _split3_bf16(x):
    hi = x.astype(BF16)
    r1 = x - hi.astype(F32)
    mid = r1.astype(BF16)
    lo = (r1 - mid.astype(F32)).astype(BF16)
    return hi, mid, lo


def _hgrn_kernel(p_ref, lb_ref, og_ref, o_ref, st_ref, c_scr, b_scr, *, layer, n_chunks):
    C = A_CHUNK
    q_cols, z_cols, i_cols, g_cols = (slice(n * A_DK, (n + 1) * A_DK) for n in range(4))
    SUB = 8
    NB = C // SUB

    @pl.when(pl.program_id(2) == 0)
    def _():
        st_ref[...] = jnp.zeros_like(st_ref)

    a = lb_ref[...]
    e = jnp.exp(a - jnp.max(a, axis=0, keepdims=True))
    sm = e / jnp.sum(e, axis=0, keepdims=True)
    lb = jnp.sum(sm[:layer + 1], axis=0, keepdims=True) - sm[0:1]
    lb = jnp.clip(lb, 0.0, 1.0 - 1e-6)
    log_lb = jnp.log(lb)
    one_m_lb = 1.0 - lb
    log_one_m_lb = jnp.log(one_m_lb)
    o_gain = og_ref[...]

    tri = (lax.broadcasted_iota(jnp.int32, (C, C), 0)
           >= lax.broadcasted_iota(jnp.int32, (C, C), 1)).astype(BF16)
    ti = lax.broadcasted_iota(jnp.int32, (C, C), 0)
    si = lax.broadcasted_iota(jnp.int32, (C, C), 1)
    same_32 = (ti // 32) == (si // 32)
    same_16 = (ti // 16) == (si // 16)
    s_sub = lax.broadcasted_iota(jnp.int32, (SUB, C), 1)
    zero_blk = jnp.zeros((SUB, LANES), F32)

    def blocks(x):
        return [x[j * SUB:(j + 1) * SUB] for j in range(NB)]

    def padded(parts):
        return jnp.concatenate([zero_blk if p is None else p for p in parts],
                               axis=0).astype(BF16)

    def bcast_row(ref, r):
        return jnp.broadcast_to(ref[r:r + 1, :], (SUB, LANES))

    chunks = [slice(c * C, (c + 1) * C) for c in range(n_chunks)]

    ks_all, bs_all = [], []
    for rows in chunks:
        z = p_ref[rows, z_cols]
        ez = jnp.exp(-jnp.abs(z))
        log_sig = jnp.minimum(z, 0.0) - jnp.log(1.0 + ez)
        k = one_m_lb * (jnp.where(z >= 0.0, ez, 1.0) / (1.0 + ez))
        y = log_one_m_lb + log_sig
        hi_ = jnp.maximum(log_lb, y)
        lo_ = jnp.minimum(log_lb, y)
        log_f = hi_ + jnp.log(1.0 + jnp.exp(lo_ - hi_))
        b = sum(jnp.dot(tri, part, preferred_element_type=F32)
                for part in _split3_bf16(log_f * LOG2_E))
        b_scr[rows, :] = b
        c_scr[rows, :] = b - (log_one_m_lb + log_sig - z) * LOG2_E
        ks_all.append(blocks(k))
        bs_all.append(blocks(b))

    levels_all, upd_all, qdec_all = [], [], []
    for rows, ks, bs in zip(chunks, ks_all, bs_all):
        qs = blocks(p_ref[rows, q_cols])
        ends = [bcast_row(b_scr, rows.start + j * SUB + SUB - 1) for j in range(NB)]
        a_levels = []
        for size in (4, 2, 1):
            qp, kp = [None] * NB, [None] * NB
            for pair in range(NB // (2 * size)):
                lo = pair * 2 * size
                r = ends[lo + size - 1]
                for j in range(lo, lo + size):
                    kp[j] = ks[j] * jnp.exp2(r - bs[j])
                for j in range(lo + size, lo + 2 * size):
                    qp[j] = qs[j] * jnp.exp2(bs[j] - r)
            a_levels.append(lax.dot_general(padded(qp), padded(kp), NT_DIMS,
                                            preferred_element_type=F32))
        levels_all.append(a_levels)
        k_dec = jnp.concatenate([ks[j] * jnp.exp2(ends[NB - 1] - bs[j]) for j in range(NB)],
                                axis=0)
        upd_all.append(lax.dot_general(p_ref[rows, i_cols].astype(BF16), k_dec.astype(BF16), TN_DIMS,
                                       preferred_element_type=F32))
        qdec_all.append(jnp.concatenate([qs[j] * jnp.exp2(bs[j]) for j in range(NB)],
                                        axis=0).astype(BF16))

    a_all = []
    for rows, bs, a_levels in zip(chunks, bs_all, levels_all):
        qs = blocks(p_ref[rows, q_cols])
        diag = [jnp.zeros((SUB, C), F32)] * NB
        for s in range(SUB):
            for j in range(NB):
                r = j * SUB + s
                w = jnp.exp2(bs[j] - bcast_row(c_scr, rows.start + r))
                a_s = jnp.sum(qs[j] * w, axis=-1, keepdims=True)
                diag[j] = jnp.where(s_sub == r, a_s, diag[j])
        a = (a_levels[0] + jnp.where(same_32, a_levels[1], 0.0)
             + jnp.where(same_16, a_levels[2], 0.0)
             + jnp.where(ti >= si, jnp.concatenate(diag, axis=0), 0.0))
        a_all.append(a.astype(BF16))

    st = st_ref[...]
    for rows, a, upd, q_dec in zip(chunks, a_all, upd_all, qdec_all):
        o = (lax.dot_general(q_dec, st.astype(BF16), NT_DIMS, preferred_element_type=F32)
             + jnp.dot(a, p_ref[rows, i_cols].astype(BF16), preferred_element_type=F32))
        st = st * jnp.exp2(b_scr[rows.stop - 1:rows.stop, :]) + upd
        g = p_ref[rows, g_cols]
        out = _rms_rows(o, o_gain) * (g / (1.0 + jnp.exp(-g)))
        o_ref[rows, :] = out.astype(o_ref.dtype)

    st_ref[...] = st


def _hgrn(proj, a_lower_bounds, o_gain, layer, batch, seq, *, rows=HGRN_ROWS):
    n_r = seq // rows
    h_ = A_HEADS
    kern = functools.partial(_hgrn_kernel, layer=layer, n_chunks=rows // A_CHUNK)
    return pl.pallas_call(
        kern, grid=(batch, h_, n_r),
        in_specs=[pl.BlockSpec((rows, 4 * A_DK), lambda b, h, r: (b * n_r + r, h)),
                  pl.BlockSpec((N_A_LAYERS, A_DK), lambda b, h, r: (0, h)),
                  pl.BlockSpec((1, A_DK), lambda b, h, r: (0, 0))],
        out_specs=pl.BlockSpec((rows, A_DK), lambda b, h, r: (b * n_r + r, h)),
        out_shape=jax.ShapeDtypeStruct((batch * seq, D_MODEL), BF16),
        scratch_shapes=[pltpu.VMEM((A_DK, A_DK), F32), pltpu.VMEM((rows, A_DK), F32),
                        pltpu.VMEM((rows, A_DK), F32)],
        compiler_params=_params("parallel", "parallel", "arbitrary"), name="hgrn",
    )(proj, a_lower_bounds, o_gain)


def _attn_kernel(q_ref, kc_ref, vc_ref, kp_ref, vp_ref, o_ref, l_ref, bias_ref, *,
                 planes, tile_rows, key_rows, den, block_rows):
    QB, P, R, KW = B_QBLOCK, planes, tile_rows, key_rows
    n_keys = P * (KW + R)
    n_edge = KW // R
    kc = lax.broadcasted_iota(jnp.int32, (n_keys, QB), 0)
    qa = lax.broadcasted_iota(jnp.int32, (n_keys, QB), 1)
    key_row = kc % (KW + R) - KW
    num = P * (qa % R - key_row) + (qa // R - kc // (KW + R))
    band = (num >= 0) & (num <= QB * den) & (num % den == 0)
    block_row0 = pl.program_id(2) * block_rows
    bias_ref[n_edge] = jnp.where(band, 0.0, MASK_VALUE)
    for t in range(n_edge):
        ok = band & (key_row >= -(block_row0 + t * R))
        bias_ref[t] = jnp.where(ok, 0.0, MASK_VALUE)

    def gather(ref, rows, cols):
        return jnp.concatenate([ref[0, m, rows, cols] for m in range(P)], axis=0)

    for t in range(block_rows // R):
        rows = slice(t * R, (t + 1) * R)
        first = t * R - KW

        def window(cur_ref, prev_ref, cols):
            if first >= 0:
                return gather(cur_ref, slice(first, (t + 1) * R), cols)
            parts = []
            for m in range(P):
                parts += [prev_ref[0, m, KW + first:KW, cols], cur_ref[0, m, 0:(t + 1) * R, cols]]
            return jnp.concatenate(parts, axis=0)

        def scores(h):
            cols = slice(h * B_HD, (h + 1) * B_HD)
            return lax.dot_general(window(kc_ref, kp_ref, cols), gather(q_ref, rows, cols),
                                   NT_DIMS, preferred_element_type=F32)

        lse_rows = []
        s_next = scores(0)
        for h in range(B_HEADS):
            cols = slice(h * B_HD, (h + 1) * B_HD)
            vals = window(vc_ref, vp_ref, cols)
            s = s_next + bias_ref[min(t, n_edge)]
            if h + 1 < B_HEADS:
                s_next = scores(h + 1)
            m = jnp.max(s, axis=0, keepdims=True)
            p = jnp.exp2(s - m)
            l = jnp.sum(p, axis=0, keepdims=True)
            o = lax.dot_general((p * (1.0 / l)).astype(BF16), vals, TN_DIMS,
                                preferred_element_type=F32).astype(o_ref.dtype)
            for m_ in range(P):
                o_ref[0, m_, rows, cols] = o[m_ * R:(m_ + 1) * R]
            lse_rows.append((m + jnp.log2(l)) * (1.0 / LOG2_E))
        pad = jnp.zeros((LANES - B_HEADS, QB), F32)
        lse_tile = jnp.concatenate(lse_rows + [pad], axis=0).T
        for m_ in range(P):
            l_ref[0, m_, rows, :] = lse_tile[m_ * R:(m_ + 1) * R]


def _attn_group(q, k, v, group):
    batch, n_planes, plane_rows, _ = q.shape
    window, dil = B_CONFIGS[group]
    assert window // dil == B_QBLOCK
    planes = max(n_planes // dil, 1)
    den = max(dil // n_planes, 1)
    n_sets = n_planes // planes
    tile_rows = B_QBLOCK // planes
    key_rows = tile_rows * den
    block_rows = plane_rows // planes
    D = D_MODEL

    def view(a):
        return a.reshape(batch, planes, n_sets, plane_rows, a.shape[-1])

    def prev_block(n):
        return jnp.maximum(n * (block_rows // key_rows) - 1, 0)

    cur = pl.BlockSpec((1, planes, None, block_rows, D), lambda b, s, n: (b, 0, s, n, group))
    before = pl.BlockSpec((1, planes, None, key_rows, D),
                          lambda b, s, n: (b, 0, s, prev_block(n), group))
    o_spec = pl.BlockSpec((1, planes, None, block_rows, D), lambda b, s, n: (b, 0, s, n, 0))
    l_spec = pl.BlockSpec((1, planes, None, block_rows, LANES), lambda b, s, n: (b, 0, s, n, 0))
    kern = functools.partial(
        _attn_kernel, planes=planes, tile_rows=tile_rows, key_rows=key_rows, den=den,
        block_rows=block_rows)
    o, lse = pl.pallas_call(
        kern, grid=(batch, n_sets, plane_rows // block_rows),
        in_specs=[cur, cur, cur, before, before], out_specs=[o_spec, l_spec],
        out_shape=[jax.ShapeDtypeStruct((batch, planes, n_sets, plane_rows, D), BF16),
                   jax.ShapeDtypeStruct((batch, planes, n_sets, plane_rows, LANES), F32)],
        scratch_shapes=[pltpu.VMEM((key_rows // tile_rows + 1,
                                    planes * (key_rows + tile_rows), B_QBLOCK), F32)],
        compiler_params=_params("parallel", "parallel", "arbitrary"),
        name=f"attn_g{group}",
    )(view(q), view(k), view(v), view(k), view(v))
    rows = batch * n_planes * plane_rows
    return o.reshape(rows, D), lse.reshape(rows, LANES)


def _mix_matmul_residual_kernel(*refs):
    o_refs, l_refs = refs[:N_GROUPS], refs[N_GROUPS:2 * N_GROUPS]
    w_ref, x_ref, out_ref = refs[2 * N_GROUPS:]
    lses = [l[...] for l in l_refs]
    top = functools.reduce(jnp.maximum, lses)
    es = [jnp.exp(l - top) for l in lses]
    inv = 1.0 / functools.reduce(jnp.add, es)
    spread = (lax.broadcasted_iota(jnp.int32, (LANES, D_MODEL), 1) // B_HD
              == lax.broadcasted_iota(jnp.int32, (LANES, D_MODEL), 0)).astype(BF16)
    mixed = None
    for o_ref, e in zip(o_refs, es):
        wts = jnp.dot((e * inv).astype(BF16), spread, preferred_element_type=F32)
        term = wts.astype(BF16) * o_ref[...]
        mixed = term if mixed is None else mixed + term
    out_ref[...] = x_ref[...] + jnp.dot(mixed, w_ref[...], preferred_element_type=F32)


def _mix_matmul_residual(outs, lses, w, layer, x, *, tm=MIX_ROWS):
    m, d = x.shape
    row_spec = pl.BlockSpec((tm, d), lambda i: (i, 0))
    return pl.pallas_call(
        _mix_matmul_residual_kernel, grid=(m // tm,),
        in_specs=[row_spec] * N_GROUPS + [pl.BlockSpec((tm, LANES), lambda i: (i, 0))] * N_GROUPS
        + [pl.BlockSpec((None, d, d), lambda i: (layer, 0, 0), pipeline_mode=pl.Buffered(1)),
           row_spec],
        out_specs=row_spec, out_shape=jax.ShapeDtypeStruct((m, d), F32),
        compiler_params=_params("parallel"), name="mix_matmul_residual",
    )(*outs, *lses, w, x)


def _to_planes(a, batch):
    rows, c = a.shape
    return (a.reshape(batch, rows // batch // SEQ_PLANES, SEQ_PLANES, c)
            .transpose(0, 2, 1, 3).reshape(rows, c))


def _from_planes(a, batch):
    rows, c = a.shape
    return (a.reshape(batch, SEQ_PLANES, rows // batch // SEQ_PLANES, c)
            .transpose(0, 2, 1, 3).reshape(rows, c))


def _rope_dim_order(a):
    h, half = ROPE_HALF, LANES // 2
    return jnp.concatenate([a[..., :h], a[..., half:half + h], a[..., 2 * h:half],
                            a[..., h:2 * h], a[..., half + h:]], axis=-1)


def _rope_tables(seq):
    inv_freq = ROPE_THETA ** (-jnp.arange(ROPE_HALF, dtype=F32) * 2.0 / ROPE_DIM)
    ang = jnp.arange(seq).astype(F32)[:, None] * inv_freq[None, :]
    cos, sin = jnp.cos(ang), jnp.sin(ang)
    gap = LANES // 2 - ROPE_HALF
    cos_t = jnp.concatenate([cos, jnp.ones((seq, gap), F32), cos, jnp.ones((seq, gap), F32)], axis=1)
    sin_t = jnp.concatenate([sin, jnp.zeros((seq, gap), F32), -sin, jnp.zeros((seq, gap), F32)],
                            axis=1)
    return _to_planes(cos_t, 1), _to_planes(sin_t, 1)


def _head_weight_kernel(w_ref, o_ref):
    h, half = ROPE_HALF, LANES // 2
    src = lax.broadcasted_iota(jnp.int32, (MXU_COLS, MXU_COLS), 0)
    dst = lax.broadcasted_iota(jnp.int32, (MXU_COLS, MXU_COLS), 1)
    d = dst % LANES
    from_lane = jnp.where((d >= h) & (d < 2 * h), d + (half - h),
                          jnp.where((d >= half) & (d < half + h), d - (half - h), d))
    perm = (src == (dst - d) + from_lane).astype(BF16)
    for c in range(w_ref.shape[1] // MXU_COLS):
        cols = slice(c * MXU_COLS, (c + 1) * MXU_COLS)
        o_ref[:, cols] = jnp.dot(w_ref[:, cols].astype(BF16), perm,
                                 preferred_element_type=F32).astype(BF16)


def _head_weight_cols(w, n_cols, *, tr=WEIGHT_ROWS):
    layers, d, _ = w.shape
    spec = pl.BlockSpec((None, tr, n_cols), lambda l, i: (l, i, 0))
    return pl.pallas_call(
        _head_weight_kernel, grid=(layers, d // tr), in_specs=[spec], out_specs=spec,
        out_shape=jax.ShapeDtypeStruct((layers, d, n_cols), BF16),
        compiler_params=_params("parallel", "parallel"), name="head_weight_cols",
    )(w)


def _head_major_cols(w):
    layers, d, n = w.shape
    return (w.reshape(layers, d, 4, A_HEADS, A_DK).transpose(0, 1, 3, 2, 4)
            .reshape(layers, d, n).astype(BF16))


def _head_gain_cols(gain, scale=1.0):
    g = _rope_dim_order(gain * scale)
    return jnp.broadcast_to(g[:, None, :], (N_GROUPS, B_HEADS, B_HD)).reshape(1, -1)


def kernel(x, a_norm, a_w_in, a_o_gain, a_w_out, a_lower_bounds, kv_norm, w_kv, k_gain,
           b_norm, b_w_q, b_q_gain, b_w_out, ffn_norm, ffn_w_in, ffn_w_out):
    batch, seq, d = x.shape
    m = batch * seq
    xf = x.reshape(m, d)

    a_w_in_b, a_w_out_b = _head_major_cols(a_w_in), a_w_out.astype(BF16)
    gd = N_GROUPS * D_MODEL
    w_k_b = _head_weight_cols(w_kv[None], gd)
    w_v_b = w_kv[:, gd:].astype(BF16)[None]
    b_w_q_b, b_w_out_b = _head_weight_cols(b_w_q, gd), b_w_out.astype(BF16)
    ffn_w_in_b, ffn_w_out_b = ffn_w_in.astype(BF16), ffn_w_out.astype(BF16)

    rope_tabs = _rope_tables(seq)
    plane_shape = (batch, SEQ_PLANES, seq // SEQ_PLANES, gd)
    k = v = None
    for layer in range(DEPTH):
        if layer < N_A_LAYERS:
            proj = _norm_matmul(xf, a_norm[layer][None], a_w_in_b, layer, F32)
            og = _hgrn(proj, a_lower_bounds, a_o_gain[layer][None], layer, batch, seq)
            xf = _matmul_residual(og, a_w_out_b, layer, xf)
        else:
            if layer == N_A_LAYERS:
                xf = _to_planes(xf, batch)
                k = _norm_matmul(xf, kv_norm[None], w_k_b, 0, BF16,
                                 rope=(_head_gain_cols(k_gain),) + rope_tabs)
                v = _norm_matmul(xf, kv_norm[None], w_v_b, 0, BF16)
                k, v = k.reshape(plane_shape), v.reshape(plane_shape)
            jb = layer - N_A_LAYERS
            q = _norm_matmul(xf, b_norm[jb][None], b_w_q_b, jb, BF16,
                             rope=(_head_gain_cols(b_q_gain[jb], B_HD ** -0.5 * LOG2_E),)
                             + rope_tabs)
            q = q.reshape(plane_shape)
            outs, lses = zip(*[_attn_group(q, k, v, group) for group in range(N_GROUPS)])
            xf = _mix_matmul_residual(outs, lses, b_w_out_b, jb, xf)
        xf = _ffn(xf, ffn_norm[layer][None], ffn_w_in_b, ffn_w_out_b, layer)
    return _from_planes(xf, batch).reshape(batch, seq, d)
```

```python
import functools

import jax
import jax.numpy as jnp
from jax import lax
from jax.experimental import pallas as pl
from jax.experimental.pallas import tpu as pltpu

F32 = jnp.float32
BF16 = jnp.bfloat16

D_MODEL = 2048
DEPTH = 4
N_A_LAYERS = DEPTH // 2
A_HEADS = 16
A_DK = D_MODEL // A_HEADS
A_CHUNK = 64
B_HEADS = 16
B_HD = D_MODEL // B_HEADS
B_CONFIGS = ((128, 1), (512, 4), (2048, 16))
N_GROUPS = len(B_CONFIGS)
B_QBLOCK = 128
ROPE_THETA = 500000.0
ROPE_DIM = B_HD // 4
ROPE_HALF = ROPE_DIM // 2
FFN_HIDDEN = ((8 * D_MODEL // 3 + 255) // 256) * 256
EPS = 1e-6

LANES = 128
MXU_COLS = 256
SEQ_PLANES = 8
VMEM_LIMIT_BYTES = 56 * 1024 * 1024
PROJ_TILE = (1024, 2048)
ROPE_PROJ_ROWS = 512
OUT_PROJ_ROWS = 1024
MIX_ROWS = 512
FFN_TILE = (1024, 512)
HGRN_ROWS = 2048
WEIGHT_ROWS = 256
MASK_VALUE = -1e30
LOG2_E = 1.4426950408889634

NT_DIMS = (((1,), (1,)), ((), ()))
TN_DIMS = (((0,), (0,)), ((), ()))


def _params(*semantics):
    return pltpu.CompilerParams(dimension_semantics=semantics,
                                vmem_limit_bytes=VMEM_LIMIT_BYTES)


def _rms_rows(x, gain):
    ms = jnp.mean(x * x, axis=-1, keepdims=True)
    return x * lax.rsqrt(ms + EPS) * gain


def _norm_matmul_kernel(x_ref, g_ref, w_ref, o_ref, xn_ref):
    @pl.when(pl.program_id(1) == 0)
    def _():
        xn_ref[...] = _rms_rows(x_ref[...], g_ref[...]).astype(BF16)

    o_ref[...] = jnp.dot(xn_ref[...], w_ref[...],
                         preferred_element_type=F32).astype(o_ref.dtype)


def _norm_matmul_rope_kernel(x_ref, g_ref, w_ref, hg_ref, cos_ref, sin_ref, o_ref, xn_ref,
                             acc_ref, scale_ref, rot_ref):
    @pl.when(pl.program_id(1) == 0)
    def _():
        xn_ref[...] = _rms_rows(x_ref[...], g_ref[...]).astype(BF16)

    xn = xn_ref[...]
    tm = xn.shape[0]
    for c in range(o_ref.shape[1] // MXU_COLS):
        acc_ref[...] = jnp.dot(xn, w_ref[:, c * MXU_COLS:(c + 1) * MXU_COLS],
                               preferred_element_type=F32)
        for h in range(MXU_COLS // LANES):
            hs = slice(h * LANES, (h + 1) * LANES)
            cols = slice(c * MXU_COLS + h * LANES, c * MXU_COLS + (h + 1) * LANES)
            y = acc_ref[:, hs]
            ms = jnp.mean(y * y, axis=-1, keepdims=True)
            scale_ref[:, hs] = jnp.broadcast_to(lax.rsqrt(ms + EPS), (tm, LANES)) * hg_ref[:, cols]
        for h in range(MXU_COLS // LANES):
            hs = slice(h * LANES, (h + 1) * LANES)
            y = acc_ref[:, hs] * scale_ref[:, hs]
            rot_ref[:, hs] = pltpu.roll(y * sin_ref[...], LANES // 2, 1)
        for h in range(MXU_COLS // LANES):
            hs = slice(h * LANES, (h + 1) * LANES)
            cols = slice(c * MXU_COLS + h * LANES, c * MXU_COLS + (h + 1) * LANES)
            y = acc_ref[:, hs] * scale_ref[:, hs]
            o_ref[:, cols] = (y * cos_ref[...] + rot_ref[:, hs]).astype(o_ref.dtype)


def _norm_matmul(x, gain, w, layer, out_dtype, *, rope=None):
    m, d = x.shape
    n = w.shape[-1]
    tm, tn = PROJ_TILE
    if rope is not None:
        tm, tn = ROPE_PROJ_ROWS, n
    grid = (m // tm, n // tn)
    x_spec = pl.BlockSpec((tm, d), lambda i, j: (i, 0))
    g_spec = pl.BlockSpec((1, d), lambda i, j: (0, 0))
    w_mode = dict(pipeline_mode=pl.Buffered(1)) if tn == n else {}
    w_spec = pl.BlockSpec((None, d, tn), lambda i, j: (layer, 0, j), **w_mode)
    o_spec = pl.BlockSpec((tm, tn), lambda i, j: (i, j))
    scratch = [pltpu.VMEM((tm, d), BF16)]
    out_shape = jax.ShapeDtypeStruct((m, n), out_dtype)
    if rope is None:
        return pl.pallas_call(
            _norm_matmul_kernel, grid=grid, in_specs=[x_spec, g_spec, w_spec],
            out_specs=o_spec, out_shape=out_shape, scratch_shapes=scratch,
            compiler_params=_params("parallel", "arbitrary"), name="norm_matmul",
        )(x, gain, w)
    head_gain, cos_t, sin_t = rope
    seq_blocks = cos_t.shape[0] // tm
    tab_spec = pl.BlockSpec((tm, LANES), lambda i, j: (i % seq_blocks, 0))
    hg_spec = pl.BlockSpec((1, tn), lambda i, j: (0, j))
    return pl.pallas_call(
        _norm_matmul_rope_kernel, grid=grid,
        in_specs=[x_spec, g_spec, w_spec, hg_spec, tab_spec, tab_spec],
        out_specs=o_spec, out_shape=out_shape,
        scratch_shapes=scratch + [pltpu.VMEM((tm, MXU_COLS), F32)] * 3,
        compiler_params=_params("parallel", "arbitrary"), name="norm_matmul_rope",
    )(x, gain, w, head_gain, cos_t, sin_t)


def _matmul_residual_kernel(a_ref, w_ref, x_ref, o_ref):
    o_ref[...] = x_ref[...] + jnp.dot(a_ref[...], w_ref[...], preferred_element_type=F32)


def _matmul_residual(a, w, layer, x):
    m, k = a.shape
    tm = OUT_PROJ_ROWS
    n = w.shape[-1]
    return pl.pallas_call(
        _matmul_residual_kernel, grid=(m // tm,),
        in_specs=[pl.BlockSpec((tm, k), lambda i: (i, 0)),
                  pl.BlockSpec((None, k, n), lambda i: (layer, 0, 0),
                               pipeline_mode=pl.Buffered(1)),
                  pl.BlockSpec((tm, n), lambda i: (i, 0))],
        out_specs=pl.BlockSpec((tm, n), lambda i: (i, 0)),
        out_shape=jax.ShapeDtypeStruct((m, n), F32),
        compiler_params=_params("parallel"), name="matmul_residual",
    )(a, w, x)


def _ffn_kernel(x_ref, g_ref, wg_ref, wu_ref, wo_ref, o_ref, xn_ref):
    j = pl.program_id(1)

    @pl.when(j == 0)
    def _():
        x = x_ref[...]
        xn_ref[...] = _rms_rows(x, g_ref[...]).astype(BF16)
        o_ref[...] = x

    xn = xn_ref[...]
    acts = []
    for c in range(wg_ref.shape[1] // MXU_COLS):
        cs = slice(c * MXU_COLS, (c + 1) * MXU_COLS)
        gate = jnp.dot(xn, wg_ref[:, cs], preferred_element_type=F32)
        up = jnp.dot(xn, wu_ref[:, cs], preferred_element_type=F32)
        acts.append((gate / (1.0 + jnp.exp(-gate)) * up).astype(BF16))
    act = jnp.concatenate(acts, axis=1)
    o_ref[...] += jnp.dot(act, wo_ref[...], preferred_element_type=F32)


def _ffn(x, gain, w_in, w_out, layer):
    m, d = x.shape
    tm, th = FFN_TILE
    hidden = w_out.shape[1]
    n_h = hidden // th
    return pl.pallas_call(
        _ffn_kernel, grid=(m // tm, n_h),
        in_specs=[pl.BlockSpec((tm, d), lambda i, j: (i, 0)),
                  pl.BlockSpec((1, d), lambda i, j: (0, 0)),
                  pl.BlockSpec((None, d, th), lambda i, j: (layer, 0, j)),
                  pl.BlockSpec((None, d, th), lambda i, j: (layer, 0, n_h + j)),
                  pl.BlockSpec((None, th, d), lambda i, j: (layer, j, 0))],
        out_specs=pl.BlockSpec((tm, d), lambda i, j: (i, 0)),
        out_shape=jax.ShapeDtypeStruct((m, d), F32),
        scratch_shapes=[pltpu.VMEM((tm, d), BF16)],
        compiler_params=_params("parallel", "arbitrary"), name="ffn",
    )(x, gain, w_in, w_in, w_out)


def _split3_bf16(x):
    hi = x.astype(BF16)
    r1 = x - hi.astype(F32)
    mid = r1.astype(BF16)
    lo = (r1 - mid.astype(F32)).astype(BF16)
    return hi, mid, lo


def _hgrn_kernel(q_ref, z_ref, i_ref, g_ref, lb_ref, og_ref, o_ref, st_ref, c_scr, b_scr, *,
                 layer, n_chunks):
    C = A_CHUNK
    SUB = 8
    NB = C // SUB

    @pl.when(pl.program_id(2) == 0)
    def _():
        st_ref[...] = jnp.zeros_like(st_ref)

    a = lb_ref[...]
    e = jnp.exp(a - jnp.max(a, axis=0, keepdims=True))
    sm = e / jnp.sum(e, axis=0, keepdims=True)
    lb = jnp.sum(sm[:layer + 1], axis=0, keepdims=True) - sm[0:1]
    lb = jnp.clip(lb, 0.0, 1.0 - 1e-6)
    log_lb = jnp.log(lb)
    one_m_lb = 1.0 - lb
    log_one_m_lb = jnp.log(one_m_lb)
    o_gain = og_ref[...]

    tri = (lax.broadcasted_iota(jnp.int32, (C, C), 0)
           >= lax.broadcasted_iota(jnp.int32, (C, C), 1)).astype(BF16)
    ti = lax.broadcasted_iota(jnp.int32, (C, C), 0)
    si = lax.broadcasted_iota(jnp.int32, (C, C), 1)
    same_32 = (ti // 32) == (si // 32)
    same_16 = (ti // 16) == (si // 16)
    s_sub = lax.broadcasted_iota(jnp.int32, (SUB, C), 1)
    zero_blk = jnp.zeros((SUB, LANES), F32)

    def blocks(x):
        return [x[j * SUB:(j + 1) * SUB] for j in range(NB)]

    def padded(parts):
        return jnp.concatenate([zero_blk if p is None else p for p in parts],
                               axis=0).astype(BF16)

    def bcast_row(ref, r):
        return jnp.broadcast_to(ref[r:r + 1, :], (SUB, LANES))

    chunks = [slice(c * C, (c + 1) * C) for c in range(n_chunks)]

    ks_all, bs_all = [], []
    for rows in chunks:
        z = z_ref[rows, :]
        ez = jnp.exp(-jnp.abs(z))
        log_sig = jnp.minimum(z, 0.0) - jnp.log(1.0 + ez)
        k = one_m_lb * (jnp.where(z >= 0.0, ez, 1.0) / (1.0 + ez))
        y = log_one_m_lb + log_sig
        hi_ = jnp.maximum(log_lb, y)
        lo_ = jnp.minimum(log_lb, y)
        log_f = hi_ + jnp.log(1.0 + jnp.exp(lo_ - hi_))
        b = sum(jnp.dot(tri, part, preferred_element_type=F32)
                for part in _split3_bf16(log_f * LOG2_E))
        b_scr[rows, :] = b
        c_scr[rows, :] = b - (log_one_m_lb + log_sig - z) * LOG2_E
        ks_all.append(blocks(k))
        bs_all.append(blocks(b))

    levels_all, upd_all, qdec_all = [], [], []
    for rows, ks, bs in zip(chunks, ks_all, bs_all):
        qs = blocks(q_ref[rows, :])
        ends = [bcast_row(b_scr, rows.start + j * SUB + SUB - 1) for j in range(NB)]
        a_levels = []
        for size in (4, 2, 1):
            qp, kp = [None] * NB, [None] * NB
            for pair in range(NB // (2 * size)):
                lo = pair * 2 * size
                r = ends[lo + size - 1]
                for j in range(lo, lo + size):
                    kp[j] = ks[j] * jnp.exp2(r - bs[j])
                for j in range(lo + size, lo + 2 * size):
                    qp[j] = qs[j] * jnp.exp2(bs[j] - r)
            a_levels.append(lax.dot_general(padded(qp), padded(kp), NT_DIMS,
                                            preferred_element_type=F32))
        levels_all.append(a_levels)
        k_dec = jnp.concatenate([ks[j] * jnp.exp2(ends[NB - 1] - bs[j]) for j in range(NB)],
                                axis=0)
        upd_all.append(lax.dot_general(i_ref[rows, :].astype(BF16), k_dec.astype(BF16), TN_DIMS,
                                       preferred_element_type=F32))
        qdec_all.append(jnp.concatenate([qs[j] * jnp.exp2(bs[j]) for j in range(NB)],
                                        axis=0).astype(BF16))

    a_all = []
    for rows, bs, a_levels in zip(chunks, bs_all, levels_all):
        qs = blocks(q_ref[rows, :])
        diag = [jnp.zeros((SUB, C), F32)] * NB
        for s in range(SUB):
            for j in range(NB):
                r = j * SUB + s
                w = jnp.exp2(bs[j] - bcast_row(c_scr, rows.start + r))
                a_s = jnp.sum(qs[j] * w, axis=-1, keepdims=True)
                diag[j] = jnp.where(s_sub == r, a_s, diag[j])
        a = (a_levels[0] + jnp.where(same_32, a_levels[1], 0.0)
             + jnp.where(same_16, a_levels[2], 0.0)
             + jnp.where(ti >= si, jnp.concatenate(diag, axis=0), 0.0))
        a_all.append(a.astype(BF16))

    st = st_ref[...]
    for rows, a, upd, q_dec in zip(chunks, a_all, upd_all, qdec_all):
        o = (lax.dot_general(q_dec, st.astype(BF16), NT_DIMS, preferred_element_type=F32)
             + jnp.dot(a, i_ref[rows, :].astype(BF16), preferred_element_type=F32))
        st = st * jnp.exp2(b_scr[rows.stop - 1:rows.stop, :]) + upd
        g = g_ref[rows, :]
        out = _rms_rows(o, o_gain) * (g / (1.0 + jnp.exp(-g)))
        o_ref[rows, :] = out.astype(o_ref.dtype)

    st_ref[...] = st


def _hgrn(proj, a_lower_bounds, o_gain, layer, batch, seq, *, rows=HGRN_ROWS):
    n_r = seq // rows
    h_ = A_HEADS

    def sec_spec(sec):
        return pl.BlockSpec((rows, A_DK), lambda b, h, r: (b * n_r + r, sec * h_ + h))

    kern = functools.partial(_hgrn_kernel, layer=layer, n_chunks=rows // A_CHUNK)
    return pl.pallas_call(
        kern, grid=(batch, h_, n_r),
        in_specs=[sec_spec(0), sec_spec(1), sec_spec(2), sec_spec(3),
                  pl.BlockSpec((N_A_LAYERS, A_DK), lambda b, h, r: (0, h)),
                  pl.BlockSpec((1, A_DK), lambda b, h, r: (0, 0))],
        out_specs=pl.BlockSpec((rows, A_DK), lambda b, h, r: (b * n_r + r, h)),
        out_shape=jax.ShapeDtypeStruct((batch * seq, D_MODEL), BF16),
        scratch_shapes=[pltpu.VMEM((A_DK, A_DK), F32), pltpu.VMEM((rows, A_DK), F32),
                        pltpu.VMEM((rows, A_DK), F32)],
        compiler_params=_params("parallel", "parallel", "arbitrary"), name="hgrn",
    )(proj, proj, proj, proj, a_lower_bounds, o_gain)


def _attn_kernel(q_ref, kc_ref, vc_ref, kp_ref, vp_ref, o_ref, l_ref, bias_ref, *,
                 planes, tile_rows, key_rows, den, block_rows):
    QB, P, R, KW = B_QBLOCK, planes, tile_rows, key_rows
    n_keys = P * (KW + R)
    n_edge = KW // R
    kc = lax.broadcasted_iota(jnp.int32, (n_keys, QB), 0)
    qa = lax.broadcasted_iota(jnp.int32, (n_keys, QB), 1)
    key_row = kc % (KW + R) - KW
    num = P * (qa % R - key_row) + (qa // R - kc // (KW + R))
    band = (num >= 0) & (num <= QB * den) & (num % den == 0)
    block_row0 = pl.program_id(2) * block_rows
    bias_ref[n_edge] = jnp.where(band, 0.0, MASK_VALUE)
    for t in range(n_edge):
        ok = band & (key_row >= -(block_row0 + t * R))
        bias_ref[t] = jnp.where(ok, 0.0, MASK_VALUE)

    def gather(ref, rows, cols):
        return jnp.concatenate([ref[0, m, rows, cols] for m in range(P)], axis=0)

    for t in range(block_rows // R):
        rows = slice(t * R, (t + 1) * R)
        first = t * R - KW

        def window(cur_ref, prev_ref, cols):
            if first >= 0:
                return gather(cur_ref, slice(first, (t + 1) * R), cols)
            parts = []
            for m in range(P):
                parts += [prev_ref[0, m, KW + first:KW, cols], cur_ref[0, m, 0:(t + 1) * R, cols]]
            return jnp.concatenate(parts, axis=0)

        def scores(h):
            cols = slice(h * B_HD, (h + 1) * B_HD)
            return lax.dot_general(window(kc_ref, kp_ref, cols), gather(q_ref, rows, cols),
                                   NT_DIMS, preferred_element_type=F32)

        lse_rows = []
        s_next = scores(0)
        for h in range(B_HEADS):
            cols = slice(h * B_HD, (h + 1) * B_HD)
            vals = window(vc_ref, vp_ref, cols)
            s = s_next + bias_ref[min(t, n_edge)]
            if h + 1 < B_HEADS:
                s_next = scores(h + 1)
            m = jnp.max(s, axis=0, keepdims=True)
            p = jnp.exp2(s - m)
            l = jnp.sum(p, axis=0, keepdims=True)
            o = lax.dot_general((p * (1.0 / l)).astype(BF16), vals, TN_DIMS,
                                preferred_element_type=F32).astype(o_ref.dtype)
            for m_ in range(P):
                o_ref[0, m_, rows, cols] = o[m_ * R:(m_ + 1) * R]
            lse_rows.append((m + jnp.log2(l)) * (1.0 / LOG2_E))
        pad = jnp.zeros((LANES - B_HEADS, QB), F32)
        lse_tile = jnp.concatenate(lse_rows + [pad], axis=0).T
        for m_ in range(P):
            l_ref[0, m_, rows, :] = lse_tile[m_ * R:(m_ + 1) * R]


def _attn_group(q, k, v, group):
    batch, n_planes, plane_rows, _ = q.shape
    window, dil = B_CONFIGS[group]
    assert window // dil == B_QBLOCK
    planes = max(n_planes // dil, 1)
    den = max(dil // n_planes, 1)
    n_sets = n_planes // planes
    tile_rows = B_QBLOCK // planes
    key_rows = tile_rows * den
    block_rows = min(plane_rows, 2 * plane_rows // planes)
    D = D_MODEL

    def view(a):
        return a.reshape(batch, planes, n_sets, plane_rows, a.shape[-1])

    def prev_block(n):
        return jnp.maximum(n * (block_rows // key_rows) - 1, 0)

    cur = pl.BlockSpec((1, planes, None, block_rows, D), lambda b, s, n: (b, 0, s, n, group))
    before = pl.BlockSpec((1, planes, None, key_rows, D),
                          lambda b, s, n: (b, 0, s, prev_block(n), group))
    o_spec = pl.BlockSpec((1, planes, None, block_rows, D), lambda b, s, n: (b, 0, s, n, 0))
    l_spec = pl.BlockSpec((1, planes, None, block_rows, LANES), lambda b, s, n: (b, 0, s, n, 0))
    kern = functools.partial(
        _attn_kernel, planes=planes, tile_rows=tile_rows, key_rows=key_rows, den=den,
        block_rows=block_rows)
    o, lse = pl.pallas_call(
        kern, grid=(batch, n_sets, plane_rows // block_rows),
        in_specs=[cur, cur, cur, before, before], out_specs=[o_spec, l_spec],
        out_shape=[jax.ShapeDtypeStruct((batch, planes, n_sets, plane_rows, D), BF16),
                   jax.ShapeDtypeStruct((batch, planes, n_sets, plane_rows, LANES), F32)],
        scratch_shapes=[pltpu.VMEM((key_rows // tile_rows + 1,
                                    planes * (key_rows + tile_rows), B_QBLOCK), F32)],
        compiler_params=_params("parallel", "parallel", "arbitrary"),
        name=f"attn_g{group}",
    )(view(q), view(k), view(v), view(k), view(v))
    rows = batch * n_planes * plane_rows
    return o.reshape(rows, D), lse.reshape(rows, LANES)


def _mix_matmul_residual_kernel(*refs):
    o_refs, l_refs = refs[:N_GROUPS], refs[N_GROUPS:2 * N_GROUPS]
    w_ref, x_ref, out_ref = refs[2 * N_GROUPS:]
    lses = [l[...] for l in l_refs]
    top = functools.reduce(jnp.maximum, lses)
    es = [jnp.exp(l - top) for l in lses]
    inv = 1.0 / functools.reduce(jnp.add, es)
    spread = (lax.broadcasted_iota(jnp.int32, (LANES, D_MODEL), 1) // B_HD
              == lax.broadcasted_iota(jnp.int32, (LANES, D_MODEL), 0)).astype(BF16)
    mixed = None
    for o_ref, e in zip(o_refs, es):
        wts = jnp.dot((e * inv).astype(BF16), spread, preferred_element_type=F32)
        term = wts.astype(BF16) * o_ref[...]
        mixed = term if mixed is None else mixed + term
    out_ref[...] = x_ref[...] + jnp.dot(mixed, w_ref[...], preferred_element_type=F32)


def _mix_matmul_residual(outs, lses, w, layer, x, *, tm=MIX_ROWS):
    m, d = x.shape
    row_spec = pl.BlockSpec((tm, d), lambda i: (i, 0))
    return pl.pallas_call(
        _mix_matmul_residual_kernel, grid=(m // tm,),
        in_specs=[row_spec] * N_GROUPS + [pl.BlockSpec((tm, LANES), lambda i: (i, 0))] * N_GROUPS
        + [pl.BlockSpec((None, d, d), lambda i: (layer, 0, 0), pipeline_mode=pl.Buffered(1)),
           row_spec],
        out_specs=row_spec, out_shape=jax.ShapeDtypeStruct((m, d), F32),
        compiler_params=_params("parallel"), name="mix_matmul_residual",
    )(*outs, *lses, w, x)


def _to_planes(a, batch):
    rows, c = a.shape
    return (a.reshape(batch, rows // batch // SEQ_PLANES, SEQ_PLANES, c)
            .transpose(0, 2, 1, 3).reshape(rows, c))


def _from_planes(a, batch):
    rows, c = a.shape
    return (a.reshape(batch, SEQ_PLANES, rows // batch // SEQ_PLANES, c)
            .transpose(0, 2, 1, 3).reshape(rows, c))


def _rope_dim_order(a):
    h, half = ROPE_HALF, LANES // 2
    return jnp.concatenate([a[..., :h], a[..., half:half + h], a[..., 2 * h:half],
                            a[..., h:2 * h], a[..., half + h:]], axis=-1)


def _rope_tables(seq):
    inv_freq = ROPE_THETA ** (-jnp.arange(ROPE_HALF, dtype=F32) * 2.0 / ROPE_DIM)
    ang = jnp.arange(seq).astype(F32)[:, None] * inv_freq[None, :]
    cos, sin = jnp.cos(ang), jnp.sin(ang)
    gap = LANES // 2 - ROPE_HALF
    cos_t = jnp.concatenate([cos, jnp.ones((seq, gap), F32), cos, jnp.ones((seq, gap), F32)], axis=1)
    sin_t = jnp.concatenate([sin, jnp.zeros((seq, gap), F32), -sin, jnp.zeros((seq, gap), F32)],
                            axis=1)
    return _to_planes(cos_t, 1), _to_planes(sin_t, 1)


def _head_weight_kernel(w_ref, o_ref):
    h, half = ROPE_HALF, LANES // 2
    src = lax.broadcasted_iota(jnp.int32, (MXU_COLS, MXU_COLS), 0)
    dst = lax.broadcasted_iota(jnp.int32, (MXU_COLS, MXU_COLS), 1)
    d = dst % LANES
    from_lane = jnp.where((d >= h) & (d < 2 * h), d + (half - h),
                          jnp.where((d >= half) & (d < half + h), d - (half - h), d))
    perm = (src == (dst - d) + from_lane).astype(BF16)
    for c in range(w_ref.shape[1] // MXU_COLS):
        cols = slice(c * MXU_COLS, (c + 1) * MXU_COLS)
        o_ref[:, cols] = jnp.dot(w_ref[:, cols].astype(BF16), perm,
                                 preferred_element_type=F32).astype(BF16)


def _head_weight_cols(w, n_cols, *, tr=WEIGHT_ROWS):
    layers, d, _ = w.shape
    spec = pl.BlockSpec((None, tr, n_cols), lambda l, i: (l, i, 0))
    return pl.pallas_call(
        _head_weight_kernel, grid=(layers, d // tr), in_specs=[spec], out_specs=spec,
        out_shape=jax.ShapeDtypeStruct((layers, d, n_cols), BF16),
        compiler_params=_params("parallel", "parallel"), name="head_weight_cols",
    )(w)


def _head_gain_cols(gain, scale=1.0):
    g = _rope_dim_order(gain * scale)
    return jnp.broadcast_to(g[:, None, :], (N_GROUPS, B_HEADS, B_HD)).reshape(1, -1)


def kernel(x, a_norm, a_w_in, a_o_gain, a_w_out, a_lower_bounds, kv_norm, w_kv, k_gain,
           b_norm, b_w_q, b_q_gain, b_w_out, ffn_norm, ffn_w_in, ffn_w_out):
    batch, seq, d = x.shape
    m = batch * seq
    xf = x.reshape(m, d)

    a_w_in_b, a_w_out_b = a_w_in.astype(BF16), a_w_out.astype(BF16)
    gd = N_GROUPS * D_MODEL
    w_k_b = _head_weight_cols(w_kv[None], gd)
    w_v_b = w_kv[:, gd:].astype(BF16)[None]
    b_w_q_b, b_w_out_b = _head_weight_cols(b_w_q, gd), b_w_out.astype(BF16)
    ffn_w_in_b, ffn_w_out_b = ffn_w_in.astype(BF16), ffn_w_out.astype(BF16)

    rope_tabs = _rope_tables(seq)
    plane_shape = (batch, SEQ_PLANES, seq // SEQ_PLANES, gd)
    k = v = None
    for layer in range(DEPTH):
        if layer < N_A_LAYERS:
            proj = _norm_matmul(xf, a_norm[layer][None], a_w_in_b, layer, F32)
            og = _hgrn(proj, a_lower_bounds, a_o_gain[layer][None], layer, batch, seq)
            xf = _matmul_residual(og, a_w_out_b, layer, xf)
        else:
            if layer == N_A_LAYERS:
                xf = _to_planes(xf, batch)
                k = _norm_matmul(xf, kv_norm[None], w_k_b, 0, BF16,
                                 rope=(_head_gain_cols(k_gain),) + rope_tabs)
                v = _norm_matmul(xf, kv_norm[None], w_v_b, 0, BF16)
                k, v = k.reshape(plane_shape), v.reshape(plane_shape)
            jb = layer - N_A_LAYERS
            q = _norm_matmul(xf, b_norm[jb][None], b_w_q_b, jb, BF16,
                             rope=(_head_gain_cols(b_q_gain[jb], B_HD ** -0.5 * LOG2_E),)
                             + rope_tabs)
            q = q.reshape(plane_shape)
            outs, lses = zip(*[_attn_group(q, k, v, group) for group in range(N_GROUPS)])
            xf = _mix_matmul_residual(outs, lses, b_w_out_b, jb, xf)
        xf = _ffn(xf, ffn_norm[layer][None], ffn_w_in_b, ffn_w_out_b, layer)
    return _from_planes(xf, batch).reshape(batch, seq, d)
```

```python
import functools

import jax
import jax.numpy as jnp
from jax import lax
from jax.experimental import pallas as pl
from jax.experimental.pallas import tpu as pltpu

F32 = jnp.float32
BF16 = jnp.bfloat16

D_MODEL = 2048
DEPTH = 4
N_A_LAYERS = DEPTH // 2
A_HEADS = 16
A_DK = D_MODEL // A_HEADS
A_CHUNK = 64
B_HEADS = 16
B_HD = D_MODEL // B_HEADS
B_CONFIGS = ((128, 1), (512, 4), (2048, 16))
N_GROUPS = len(B_CONFIGS)
B_QBLOCK = 128
ROPE_THETA = 500000.0
ROPE_DIM = B_HD // 4
ROPE_HALF = ROPE_DIM // 2
FFN_HIDDEN = ((8 * D_MODEL // 3 + 255) // 256) * 256
EPS = 1e-6

LANES = 128
MXU_COLS = 256
SEQ_PLANES = 8
VMEM_LIMIT_BYTES = 56 * 1024 * 1024
PROJ_TILE = (1024, 2048)
ROPE_PROJ_ROWS = 512
OUT_PROJ_ROWS = 1024
MIX_ROWS = 512
FFN_TILE = (1024, 512)
HGRN_ROWS = 2048
WEIGHT_ROWS = 256
MASK_VALUE = -1e30
LOG2_E = 1.4426950408889634

NT_DIMS = (((1,), (1,)), ((), ()))
TN_DIMS = (((0,), (0,)), ((), ()))


def _params(*semantics):
    return pltpu.CompilerParams(dimension_semantics=semantics,
                                vmem_limit_bytes=VMEM_LIMIT_BYTES)


def _rms_rows(x, gain):
    ms = jnp.mean(x * x, axis=-1, keepdims=True)
    return x * lax.rsqrt(ms + EPS) * gain


def _norm_matmul_kernel(x_ref, g_ref, w_ref, o_ref, xn_ref):
    @pl.when(pl.program_id(1) == 0)
    def _():
        xn_ref[...] = _rms_rows(x_ref[...], g_ref[...]).astype(BF16)

    o_ref[...] = jnp.dot(xn_ref[...], w_ref[...],
                         preferred_element_type=F32).astype(o_ref.dtype)


def _norm_matmul_rope_kernel(x_ref, g_ref, w_ref, hg_ref, cos_ref, sin_ref, o_ref, xn_ref,
                             acc_ref, scale_ref, rot_ref):
    @pl.when(pl.program_id(1) == 0)
    def _():
        xn_ref[...] = _rms_rows(x_ref[...], g_ref[...]).astype(BF16)

    xn = xn_ref[...]
    tm = xn.shape[0]
    for c in range(o_ref.shape[1] // MXU_COLS):
        acc_ref[...] = jnp.dot(xn, w_ref[:, c * MXU_COLS:(c + 1) * MXU_COLS],
                               preferred_element_type=F32)
        for h in range(MXU_COLS // LANES):
            hs = slice(h * LANES, (h + 1) * LANES)
            cols = slice(c * MXU_COLS + h * LANES, c * MXU_COLS + (h + 1) * LANES)
            y = acc_ref[:, hs]
            ms = jnp.mean(y * y, axis=-1, keepdims=True)
            scale_ref[:, hs] = jnp.broadcast_to(lax.rsqrt(ms + EPS), (tm, LANES)) * hg_ref[:, cols]
        for h in range(MXU_COLS // LANES):
            hs = slice(h * LANES, (h + 1) * LANES)
            y = acc_ref[:, hs] * scale_ref[:, hs]
            rot_ref[:, hs] = pltpu.roll(y * sin_ref[...], LANES // 2, 1)
        for h in range(MXU_COLS // LANES):
            hs = slice(h * LANES, (h + 1) * LANES)
            cols = slice(c * MXU_COLS + h * LANES, c * MXU_COLS + (h + 1) * LANES)
            y = acc_ref[:, hs] * scale_ref[:, hs]
            o_ref[:, cols] = (y * cos_ref[...] + rot_ref[:, hs]).astype(o_ref.dtype)


def _norm_matmul(x, gain, w, layer, out_dtype, *, rope=None):
    m, d = x.shape
    n = w.shape[-1]
    tm, tn = PROJ_TILE
    if rope is not None or out_dtype == BF16:
        tm, tn = ROPE_PROJ_ROWS, n
    grid = (m // tm, n // tn)
    x_spec = pl.BlockSpec((tm, d), lambda i, j: (i, 0))
    g_spec = pl.BlockSpec((1, d), lambda i, j: (0, 0))
    w_mode = dict(pipeline_mode=pl.Buffered(1)) if tn == n else {}
    w_spec = pl.BlockSpec((None, d, tn), lambda i, j: (layer, 0, j), **w_mode)
    o_spec = pl.BlockSpec((tm, tn), lambda i, j: (i, j))
    scratch = [pltpu.VMEM((tm, d), BF16)]
    out_shape = jax.ShapeDtypeStruct((m, n), out_dtype)
    if rope is None:
        return pl.pallas_call(
            _norm_matmul_kernel, grid=grid, in_specs=[x_spec, g_spec, w_spec],
            out_specs=o_spec, out_shape=out_shape, scratch_shapes=scratch,
            compiler_params=_params("parallel", "arbitrary"), name="norm_matmul",
        )(x, gain, w)
    head_gain, cos_t, sin_t = rope
    seq_blocks = cos_t.shape[0] // tm
    tab_spec = pl.BlockSpec((tm, LANES), lambda i, j: (i % seq_blocks, 0))
    hg_spec = pl.BlockSpec((1, tn), lambda i, j: (0, j))
    return pl.pallas_call(
        _norm_matmul_rope_kernel, grid=grid,
        in_specs=[x_spec, g_spec, w_spec, hg_spec, tab_spec, tab_spec],
        out_specs=o_spec, out_shape=out_shape,
        scratch_shapes=scratch + [pltpu.VMEM((tm, MXU_COLS), F32)] * 3,
        compiler_params=_params("parallel", "arbitrary"), name="norm_matmul_rope",
    )(x, gain, w, head_gain, cos_t, sin_t)


def _matmul_residual_kernel(a_ref, w_ref, x_ref, o_ref):
    o_ref[...] = x_ref[...] + jnp.dot(a_ref[...], w_ref[...], preferred_element_type=F32)


def _matmul_residual(a, w, layer, x):
    m, k = a.shape
    tm = OUT_PROJ_ROWS
    n = w.shape[-1]
    return pl.pallas_call(
        _matmul_residual_kernel, grid=(m // tm,),
        in_specs=[pl.BlockSpec((tm, k), lambda i: (i, 0)),
                  pl.BlockSpec((None, k, n), lambda i: (layer, 0, 0),
                               pipeline_mode=pl.Buffered(1)),
                  pl.BlockSpec((tm, n), lambda i: (i, 0))],
        out_specs=pl.BlockSpec((tm, n), lambda i: (i, 0)),
        out_shape=jax.ShapeDtypeStruct((m, n), F32),
        compiler_params=_params("parallel"), name="matmul_residual",
    )(a, w, x)


def _ffn_kernel(x_ref, g_ref, wg_ref, wu_ref, wo_ref, o_ref, xn_ref):
    j = pl.program_id(1)

    @pl.when(j == 0)
    def _():
        x = x_ref[...]
        xn_ref[...] = _rms_rows(x, g_ref[...]).astype(BF16)
        o_ref[...] = x

    xn = xn_ref[...]
    acts = []
    for c in range(wg_ref.shape[1] // MXU_COLS):
        cs = slice(c * MXU_COLS, (c + 1) * MXU_COLS)
        gate = jnp.dot(xn, wg_ref[:, cs], preferred_element_type=F32)
        up = jnp.dot(xn, wu_ref[:, cs], preferred_element_type=F32)
        acts.append((gate / (1.0 + jnp.exp(-gate)) * up).astype(BF16))
    act = jnp.concatenate(acts, axis=1)
    o_ref[...] += jnp.dot(act, wo_ref[...], preferred_element_type=F32)


def _ffn(x, gain, w_in, w_out, layer):
    m, d = x.shape
    tm, th = FFN_TILE
    hidden = w_out.shape[1]
    n_h = hidden // th
    return pl.pallas_call(
        _ffn_kernel, grid=(m // tm, n_h),
        in_specs=[pl.BlockSpec((tm, d), lambda i, j: (i, 0)),
                  pl.BlockSpec((1, d), lambda i, j: (0, 0)),
                  pl.BlockSpec((None, d, th), lambda i, j: (layer, 0, j)),
                  pl.BlockSpec((None, d, th), lambda i, j: (layer, 0, n_h + j)),
                  pl.BlockSpec((None, th, d), lambda i, j: (layer, j, 0))],
        out_specs=pl.BlockSpec((tm, d), lambda i, j: (i, 0)),
        out_shape=jax.ShapeDtypeStruct((m, d), F32),
        scratch_shapes=[pltpu.VMEM((tm, d), BF16)],
        compiler_params=_params("parallel", "arbitrary"), name="ffn",
    )(x, gain, w_in, w_in, w_out)


def _split3_bf16(x):
    hi = x.astype(BF16)
    r1 = x - hi.astype(F32)
    mid = r1.astype(BF16)
    lo = (r1 - mid.astype(F32)).astype(BF16)
    return hi, mid, lo


def _hgrn_kernel(q_ref, z_ref, i_ref, g_ref, lb_ref, og_ref, o_ref, st_ref, c_scr, b_scr, *,
                 layer, n_chunks):
    C = A_CHUNK
    SUB = 8
    NB = C // SUB

    @pl.when(pl.program_id(2) == 0)
    def _():
        st_ref[...] = jnp.zeros_like(st_ref)

    a = lb_ref[...]
    e = jnp.exp(a - jnp.max(a, axis=0, keepdims=True))
    sm = e / jnp.sum(e, axis=0, keepdims=True)
    lb = jnp.sum(sm[:layer + 1], axis=0, keepdims=True) - sm[0:1]
    lb = jnp.clip(lb, 0.0, 1.0 - 1e-6)
    log_lb = jnp.log(lb)
    one_m_lb = 1.0 - lb
    log_one_m_lb = jnp.log(one_m_lb)
    o_gain = og_ref[...]

    tri = (lax.broadcasted_iota(jnp.int32, (C, C), 0)
           >= lax.broadcasted_iota(jnp.int32, (C, C), 1)).astype(BF16)
    ti = lax.broadcasted_iota(jnp.int32, (C, C), 0)
    si = lax.broadcasted_iota(jnp.int32, (C, C), 1)
    same_32 = (ti // 32) == (si // 32)
    same_16 = (ti // 16) == (si // 16)
    s_sub = lax.broadcasted_iota(jnp.int32, (SUB, C), 1)
    zero_blk = jnp.zeros((SUB, LANES), F32)

    def blocks(x):
        return [x[j * SUB:(j + 1) * SUB] for j in range(NB)]

    def padded(parts):
        return jnp.concatenate([zero_blk if p is None else p for p in parts],
                               axis=0).astype(BF16)

    def bcast_row(ref, r):
        return jnp.broadcast_to(ref[r:r + 1, :], (SUB, LANES))

    chunks = [slice(c * C, (c + 1) * C) for c in range(n_chunks)]

    ks_all, bs_all = [], []
    for rows in chunks:
        z = z_ref[rows, :]
        ez = jnp.exp(-jnp.abs(z))
        log_sig = jnp.minimum(z, 0.0) - jnp.log(1.0 + ez)
        k = one_m_lb * (jnp.where(z >= 0.0, ez, 1.0) / (1.0 + ez))
        y = log_one_m_lb + log_sig
        hi_ = jnp.maximum(log_lb, y)
        lo_ = jnp.minimum(log_lb, y)
        log_f = hi_ + jnp.log(1.0 + jnp.exp(lo_ - hi_))
        b = sum(jnp.dot(tri, part, preferred_element_type=F32)
                for part in _split3_bf16(log_f * LOG2_E))
        b_scr[rows, :] = b
        c_scr[rows, :] = b - (log_one_m_lb + log_sig - z) * LOG2_E
        ks_all.append(blocks(k))
        bs_all.append(blocks(b))

    levels_all, upd_all, qdec_all = [], [], []
    for rows, ks, bs in zip(chunks, ks_all, bs_all):
        qs = blocks(q_ref[rows, :])
        ends = [bcast_row(b_scr, rows.start + j * SUB + SUB - 1) for j in range(NB)]
        a_levels = []
        for size in (4, 2, 1):
            qp, kp = [None] * NB, [None] * NB
            for pair in range(NB // (2 * size)):
                lo = pair * 2 * size
                r = ends[lo + size - 1]
                for j in range(lo, lo + size):
                    kp[j] = ks[j] * jnp.exp2(r - bs[j])
                for j in range(lo + size, lo + 2 * size):
                    qp[j] = qs[j] * jnp.exp2(bs[j] - r)
            a_levels.append(lax.dot_general(padded(qp), padded(kp), NT_DIMS,
                                            preferred_element_type=F32))
        levels_all.append(a_levels)
        k_dec = jnp.concatenate([ks[j] * jnp.exp2(ends[NB - 1] - bs[j]) for j in range(NB)],
                                axis=0)
        upd_all.append(lax.dot_general(i_ref[rows, :].astype(BF16), k_dec.astype(BF16), TN_DIMS,
                                       preferred_element_type=F32))
        qdec_all.append(jnp.concatenate([qs[j] * jnp.exp2(bs[j]) for j in range(NB)],
                                        axis=0).astype(BF16))

    a_all = []
    for rows, bs, a_levels in zip(chunks, bs_all, levels_all):
        qs = blocks(q_ref[rows, :])
        diag = [jnp.zeros((SUB, C), F32)] * NB
        for s in range(SUB):
            for j in range(NB):
                r = j * SUB + s
                w = jnp.exp2(bs[j] - bcast_row(c_scr, rows.start + r))
                a_s = jnp.sum(qs[j] * w, axis=-1, keepdims=True)
                diag[j] = jnp.where(s_sub == r, a_s, diag[j])
        a = (a_levels[0] + jnp.where(same_32, a_levels[1], 0.0)
             + jnp.where(same_16, a_levels[2], 0.0)
             + jnp.where(ti >= si, jnp.concatenate(diag, axis=0), 0.0))
        a_all.append(a.astype(BF16))

    st = st_ref[...]
    for rows, a, upd, q_dec in zip(chunks, a_all, upd_all, qdec_all):
        o = (lax.dot_general(q_dec, st.astype(BF16), NT_DIMS, preferred_element_type=F32)
             + jnp.dot(a, i_ref[rows, :].astype(BF16), preferred_element_type=F32))
        st = st * jnp.exp2(b_scr[rows.stop - 1:rows.stop, :]) + upd
        g = g_ref[rows, :]
        out = _rms_rows(o, o_gain) * (g / (1.0 + jnp.exp(-g)))
        o_ref[rows, :] = out.astype(o_ref.dtype)

    st_ref[...] = st


def _hgrn(proj, a_lower_bounds, o_gain, layer, batch, seq, *, rows=HGRN_ROWS):
    n_r = seq // rows
    h_ = A_HEADS

    def sec_spec(sec):
        return pl.BlockSpec((rows, A_DK), lambda b, h, r: (b * n_r + r, sec * h_ + h))

    kern = functools.partial(_hgrn_kernel, layer=layer, n_chunks=rows // A_CHUNK)
    return pl.pallas_call(
        kern, grid=(batch, h_, n_r),
        in_specs=[sec_spec(0), sec_spec(1), sec_spec(2), sec_spec(3),
                  pl.BlockSpec((N_A_LAYERS, A_DK), lambda b, h, r: (0, h)),
                  pl.BlockSpec((1, A_DK), lambda b, h, r: (0, 0))],
        out_specs=pl.BlockSpec((rows, A_DK), lambda b, h, r: (b * n_r + r, h)),
        out_shape=jax.ShapeDtypeStruct((batch * seq, D_MODEL), BF16),
        scratch_shapes=[pltpu.VMEM((A_DK, A_DK), F32), pltpu.VMEM((rows, A_DK), F32),
                        pltpu.VMEM((rows, A_DK), F32)],
        compiler_params=_params("parallel", "parallel", "arbitrary"), name="hgrn",
    )(proj, proj, proj, proj, a_lower_bounds, o_gain)


def _attn_kernel(q_ref, kc_ref, vc_ref, kp_ref, vp_ref, o_ref, l_ref, bias_ref, *,
                 planes, tile_rows, key_rows, den, block_rows):
    QB, P, R, KW = B_QBLOCK, planes, tile_rows, key_rows
    n_keys = P * (KW + R)
    n_edge = KW // R
    kc = lax.broadcasted_iota(jnp.int32, (n_keys, QB), 0)
    qa = lax.broadcasted_iota(jnp.int32, (n_keys, QB), 1)
    key_row = kc % (KW + R) - KW
    num = P * (qa % R - key_row) + (qa // R - kc // (KW + R))
    band = (num >= 0) & (num <= QB * den) & (num % den == 0)
    block_row0 = pl.program_id(2) * block_rows
    bias_ref[n_edge] = jnp.where(band, 0.0, MASK_VALUE)
    for t in range(n_edge):
        ok = band & (key_row >= -(block_row0 + t * R))
        bias_ref[t] = jnp.where(ok, 0.0, MASK_VALUE)

    def gather(ref, rows, cols):
        return jnp.concatenate([ref[0, m, rows, cols] for m in range(P)], axis=0)

    for t in range(block_rows // R):
        rows = slice(t * R, (t + 1) * R)
        first = t * R - KW

        def window(cur_ref, prev_ref, cols):
            if first >= 0:
                return gather(cur_ref, slice(first, (t + 1) * R), cols)
            parts = []
            for m in range(P):
                parts += [prev_ref[0, m, KW + first:KW, cols], cur_ref[0, m, 0:(t + 1) * R, cols]]
            return jnp.concatenate(parts, axis=0)

        def scores(h):
            cols = slice(h * B_HD, (h + 1) * B_HD)
            return lax.dot_general(window(kc_ref, kp_ref, cols), gather(q_ref, rows, cols),
                                   NT_DIMS, preferred_element_type=F32)

        lse_rows = []
        s_next = scores(0)
        for h in range(B_HEADS):
            cols = slice(h * B_HD, (h + 1) * B_HD)
            vals = window(vc_ref, vp_ref, cols)
            s = s_next + bias_ref[min(t, n_edge)]
            if h + 1 < B_HEADS:
                s_next = scores(h + 1)
            m = jnp.max(s, axis=0, keepdims=True)
            p = jnp.exp2(s - m)
            l = jnp.sum(p, axis=0, keepdims=True)
            o = lax.dot_general((p * (1.0 / l)).astype(BF16), vals, TN_DIMS,
                                preferred_element_type=F32).astype(o_ref.dtype)
            for m_ in range(P):
                o_ref[0, m_, rows, cols] = o[m_ * R:(m_ + 1) * R]
            lse_rows.append((m + jnp.log2(l)) * (1.0 / LOG2_E))
        pad = jnp.zeros((LANES - B_HEADS, QB), F32)
        lse_tile = jnp.concatenate(lse_rows + [pad], axis=0).T
        for m_ in range(P):
            l_ref[0, m_, rows, :] = lse_tile[m_ * R:(m_ + 1) * R]


def _attn_group(q, k, v, group):
    batch, n_planes, plane_rows, _ = q.shape
    window, dil = B_CONFIGS[group]
    assert window // dil == B_QBLOCK
    planes = max(n_planes // dil, 1)
    den = max(dil // n_planes, 1)
    n_sets = n_planes // planes
    tile_rows = B_QBLOCK // planes
    key_rows = tile_rows * den
    block_rows = min(plane_rows, 2 * plane_rows // planes)
    D = D_MODEL

    def view(a):
        return a.reshape(batch, planes, n_sets, plane_rows, a.shape[-1])

    def prev_block(n):
        return jnp.maximum(n * (block_rows // key_rows) - 1, 0)

    cur = pl.BlockSpec((1, planes, None, block_rows, D), lambda b, s, n: (b, 0, s, n, group))
    before = pl.BlockSpec((1, planes, None, key_rows, D),
                          lambda b, s, n: (b, 0, s, prev_block(n), group))
    o_spec = pl.BlockSpec((1, planes, None, block_rows, D), lambda b, s, n: (b, 0, s, n, 0))
    l_spec = pl.BlockSpec((1, planes, None, block_rows, LANES), lambda b, s, n: (b, 0, s, n, 0))
    kern = functools.partial(
        _attn_kernel, planes=planes, tile_rows=tile_rows, key_rows=key_rows, den=den,
        block_rows=block_rows)
    o, lse = pl.pallas_call(
        kern, grid=(batch, n_sets, plane_rows // block_rows),
        in_specs=[cur, cur, cur, before, before], out_specs=[o_spec, l_spec],
        out_shape=[jax.ShapeDtypeStruct((batch, planes, n_sets, plane_rows, D), BF16),
                   jax.ShapeDtypeStruct((batch, planes, n_sets, plane_rows, LANES), F32)],
        scratch_shapes=[pltpu.VMEM((key_rows // tile_rows + 1,
                                    planes * (key_rows + tile_rows), B_QBLOCK), F32)],
        compiler_params=_params("parallel", "parallel", "arbitrary"),
        name=f"attn_g{group}",
    )(view(q), view(k), view(v), view(k), view(v))
    rows = batch * n_planes * plane_rows
    return o.reshape(rows, D), lse.reshape(rows, LANES)


def _mix_matmul_residual_kernel(*refs):
    o_refs, l_refs = refs[:N_GROUPS], refs[N_GROUPS:2 * N_GROUPS]
    w_ref, x_ref, out_ref = refs[2 * N_GROUPS:]
    lses = [l[...] for l in l_refs]
    top = functools.reduce(jnp.maximum, lses)
    es = [jnp.exp(l - top) for l in lses]
    inv = 1.0 / functools.reduce(jnp.add, es)
    spread = (lax.broadcasted_iota(jnp.int32, (LANES, D_MODEL), 1) // B_HD
              == lax.broadcasted_iota(jnp.int32, (LANES, D_MODEL), 0)).astype(BF16)
    mixed = None
    for o_ref, e in zip(o_refs, es):
        wts = jnp.dot((e * inv).astype(BF16), spread, preferred_element_type=F32)
        term = wts.astype(BF16) * o_ref[...]
        mixed = term if mixed is None else mixed + term
    out_ref[...] = x_ref[...] + jnp.dot(mixed, w_ref[...], preferred_element_type=F32)


def _mix_matmul_residual(outs, lses, w, layer, x, *, tm=MIX_ROWS):
    m, d = x.shape
    row_spec = pl.BlockSpec((tm, d), lambda i: (i, 0))
    return pl.pallas_call(
        _mix_matmul_residual_kernel, grid=(m // tm,),
        in_specs=[row_spec] * N_GROUPS + [pl.BlockSpec((tm, LANES), lambda i: (i, 0))] * N_GROUPS
        + [pl.BlockSpec((None, d, d), lambda i: (layer, 0, 0), pipeline_mode=pl.Buffered(1)),
           row_spec],
        out_specs=row_spec, out_shape=jax.ShapeDtypeStruct((m, d), F32),
        compiler_params=_params("parallel"), name="mix_matmul_residual",
    )(*outs, *lses, w, x)


def _to_planes(a, batch):
    rows, c = a.shape
    return (a.reshape(batch, rows // batch // SEQ_PLANES, SEQ_PLANES, c)
            .transpose(0, 2, 1, 3).reshape(rows, c))


def _from_planes(a, batch):
    rows, c = a.shape
    return (a.reshape(batch, SEQ_PLANES, rows // batch // SEQ_PLANES, c)
            .transpose(0, 2, 1, 3).reshape(rows, c))


def _rope_dim_order(a):
    h, half = ROPE_HALF, LANES // 2
    return jnp.concatenate([a[..., :h], a[..., half:half + h], a[..., 2 * h:half],
                            a[..., h:2 * h], a[..., half + h:]], axis=-1)


def _rope_tables(seq):
    inv_freq = ROPE_THETA ** (-jnp.arange(ROPE_HALF, dtype=F32) * 2.0 / ROPE_DIM)
    ang = jnp.arange(seq).astype(F32)[:, None] * inv_freq[None, :]
    cos, sin = jnp.cos(ang), jnp.sin(ang)
    gap = LANES // 2 - ROPE_HALF
    cos_t = jnp.concatenate([cos, jnp.ones((seq, gap), F32), cos, jnp.ones((seq, gap), F32)], axis=1)
    sin_t = jnp.concatenate([sin, jnp.zeros((seq, gap), F32), -sin, jnp.zeros((seq, gap), F32)],
                            axis=1)
    return _to_planes(cos_t, 1), _to_planes(sin_t, 1)


def _head_weight_kernel(w_ref, o_ref):
    h, half = ROPE_HALF, LANES // 2
    src = lax.broadcasted_iota(jnp.int32, (MXU_COLS, MXU_COLS), 0)
    dst = lax.broadcasted_iota(jnp.int32, (MXU_COLS, MXU_COLS), 1)
    d = dst % LANES
    from_lane = jnp.where((d >= h) & (d < 2 * h), d + (half - h),
                          jnp.where((d >= half) & (d < half + h), d - (half - h), d))
    perm = (src == (dst - d) + from_lane).astype(BF16)
    for c in range(w_ref.shape[1] // MXU_COLS):
        cols = slice(c * MXU_COLS, (c + 1) * MXU_COLS)
        o_ref[:, cols] = jnp.dot(w_ref[:, cols].astype(BF16), perm,
                                 preferred_element_type=F32).astype(BF16)


def _head_weight_cols(w, n_cols, *, tr=WEIGHT_ROWS):
    layers, d, _ = w.shape
    spec = pl.BlockSpec((None, tr, n_cols), lambda l, i: (l, i, 0))
    return pl.pallas_call(
        _head_weight_kernel, grid=(layers, d // tr), in_specs=[spec], out_specs=spec,
        out_shape=jax.ShapeDtypeStruct((layers, d, n_cols), BF16),
        compiler_params=_params("parallel", "parallel"), name="head_weight_cols",
    )(w)


def _head_gain_cols(gain, scale=1.0):
    g = _rope_dim_order(gain * scale)
    return jnp.broadcast_to(g[:, None, :], (N_GROUPS, B_HEADS, B_HD)).reshape(1, -1)


def kernel(x, a_norm, a_w_in, a_o_gain, a_w_out, a_lower_bounds, kv_norm, w_kv, k_gain,
           b_norm, b_w_q, b_q_gain, b_w_out, ffn_norm, ffn_w_in, ffn_w_out):
    batch, seq, d = x.shape
    m = batch * seq
    xf = x.reshape(m, d)

    a_w_in_b, a_w_out_b = a_w_in.astype(BF16), a_w_out.astype(BF16)
    gd = N_GROUPS * D_MODEL
    w_k_b = _head_weight_cols(w_kv[None], gd)
    w_v_b = w_kv[:, gd:].astype(BF16)[None]
    b_w_q_b, b_w_out_b = _head_weight_cols(b_w_q, gd), b_w_out.astype(BF16)
    ffn_w_in_b, ffn_w_out_b = ffn_w_in.astype(BF16), ffn_w_out.astype(BF16)

    rope_tabs = _rope_tables(seq)
    plane_shape = (batch, SEQ_PLANES, seq // SEQ_PLANES, gd)
    k = v = None
    for layer in range(DEPTH):
        if layer < N_A_LAYERS:
            proj = _norm_matmul(xf, a_norm[layer][None], a_w_in_b, layer, F32)
            og = _hgrn(proj, a_lower_bounds, a_o_gain[layer][None], layer, batch, seq)
            xf = _matmul_residual(og, a_w_out_b, layer, xf)
        else:
            if layer == N_A_LAYERS:
                xf = _to_planes(xf, batch)
                k = _norm_matmul(xf, kv_norm[None], w_k_b, 0, BF16,
                                 rope=(_head_gain_cols(k_gain),) + rope_tabs)
                v = _norm_matmul(xf, kv_norm[None], w_v_b, 0, BF16)
                k, v = k.reshape(plane_shape), v.reshape(plane_shape)
            jb = layer - N_A_LAYERS
            q = _norm_matmul(xf, b_norm[jb][None], b_w_q_b, jb, BF16,
                             rope=(_head_gain_cols(b_q_gain[jb], B_HD ** -0.5 * LOG2_E),)
                             + rope_tabs)
            q = q.reshape(plane_shape)
            outs, lses = zip(*[_attn_group(q, k, v, group) for group in range(N_GROUPS)])
            xf = _mix_matmul_residual(outs, lses, b_w_out_b, jb, xf)
        xf = _ffn(xf, ffn_norm[layer][None], ffn_w_in_b, ffn_w_out_b, layer)
    return _from_planes(xf, batch).reshape(batch, seq, d)
```
